```python
import math
import jax, jax.numpy as jnp
from jax import lax
import numpy as np

D_MODEL = 1024
BATCH = 32
SEQ = 256
DEPTH = 4
DEC_BATCH = 8
DEC_SEQ = 4096
PAST_LEN = 256

GRID_W = 64
N_HEADS = 8
N_KV_HEADS = 2
HEAD_DIM = 64
ATTN_W = N_HEADS * HEAD_DIM
KV_W = N_KV_HEADS * HEAD_DIM
HYENA_W = 256
CONV_W = 256
MIX_W = ATTN_W + HYENA_W + CONV_W
WINDOW = 128
BLOCK = 128
HYENA_ORDER = 2
FILTER_EMB = 33
FILTER_HIDDEN = 64
HYENA_TARGET = 1e-2
FAST_DECAY_PCT = 0.3
SLOW_DECAY_PCT = 1.5
ROPE_BASE = 10000.0
RMS_EPS = 1e-6
NEG_INF = -1e30
IN_COLS = ATTN_W + 2 * KV_W + ATTN_W + 3 * HYENA_W + HYENA_W + 3 * CONV_W + CONV_W
SPLITS = (ATTN_W,
          ATTN_W + KV_W,
          ATTN_W + 2 * KV_W,
          2 * ATTN_W + 2 * KV_W,
          2 * ATTN_W + 2 * KV_W + 3 * HYENA_W,
          2 * ATTN_W + 2 * KV_W + 4 * HYENA_W,
          2 * ATTN_W + 2 * KV_W + 4 * HYENA_W + 3 * CONV_W)

kernel_name = 'hybrid_diffusion_parallel_heads_step'


def rms_norm(x, g):
    xf = x.astype(jnp.float32)
    y = xf * lax.rsqrt(jnp.mean(xf * xf, axis=-1, keepdims=True) + RMS_EPS)
    return (y * g.astype(jnp.float32)).astype(x.dtype)


def dwconv3(u, w):
    L = u.shape[1]
    up = jnp.pad(u, ((0, 0), (1, 1), (0, 0)))
    return up[:, :L] * w[0] + up[:, 1:L + 1] * w[1] + up[:, 2:] * w[2]


def axial_rope(L, dtype):
    t = jnp.arange(L)
    n_freq = HEAD_DIM // 4
    inv = ROPE_BASE ** (-jnp.arange(n_freq, dtype=jnp.float32) / n_freq)
    row = (t // GRID_W).astype(jnp.float32)[:, None] * inv
    col = (t % GRID_W).astype(jnp.float32)[:, None] * inv
    ang = jnp.stack([row, col], axis=1)
    return jnp.cos(ang).astype(dtype), jnp.sin(ang).astype(dtype)


def apply_rope(x, cos, sin):
    xs = x.reshape(x.shape[:-1] + (2, 2, HEAD_DIM // 4))
    xa, xb = xs[..., 0, :], xs[..., 1, :]
    c = cos[:, None]
    s = sin[:, None]
    out = jnp.stack([xa * c - xb * s, xb * c + xa * s], axis=-2)
    return out.reshape(x.shape)


def context_attention(q, k, v, sink):
    B, Lc = q.shape[:2]
    G = N_HEADS // N_KV_HEADS
    nb = Lc // BLOCK
    scale = HEAD_DIM ** -0.5
    sink_logit = sink.astype(jnp.float32).reshape(1, N_KV_HEADS, G, 1, 1)

    def block(n):
        qb = lax.dynamic_slice_in_dim(q, n * BLOCK, BLOCK, 1).reshape(B, BLOCK, N_KV_HEADS, G, HEAD_DIM)
        s = jnp.einsum('bqhgd,bkhd->bhgqk', qb, k).astype(jnp.float32) * scale
        s_snk = jnp.broadcast_to(sink_logit, s.shape[:-1] + (1,))
        p = jax.nn.softmax(jnp.concatenate([s, s_snk], axis=-1), axis=-1)[..., :Lc].astype(v.dtype)
        o = jnp.einsum('bhgqk,bkhd->bqhgd', p, v)
        return o.reshape(B, BLOCK, ATTN_W)

    out = lax.map(block, jnp.arange(nb))
    return jnp.moveaxis(out, 0, 1).reshape(B, Lc, ATTN_W)


def latent_attention(q_rot, q_raw, k_rot, v, ck, cv, sink):
    B, L = q_rot.shape[:2]
    Lc = ck.shape[1]
    G = N_HEADS // N_KV_HEADS
    nb = L // BLOCK
    scale = HEAD_DIM ** -0.5
    pad = ((0, 0), (BLOCK, BLOCK), (0, 0), (0, 0))
    kp = jnp.pad(k_rot, pad)
    vp = jnp.pad(v, pad)
    qi = jnp.arange(BLOCK)[:, None]
    kj = jnp.arange(3 * BLOCK)[None, :]
    in_window = jnp.abs(kj - BLOCK - qi) <= WINDOW
    sink_logit = sink.astype(jnp.float32).reshape(1, N_KV_HEADS, G, 1, 1)

    def block(n):
        start = n * BLOCK
        qb = lax.dynamic_slice_in_dim(q_rot, start, BLOCK, 1).reshape(B, BLOCK, N_KV_HEADS, G, HEAD_DIM)
        q0 = lax.dynamic_slice_in_dim(q_raw, start, BLOCK, 1).reshape(B, BLOCK, N_KV_HEADS, G, HEAD_DIM)
        kb = lax.dynamic_slice_in_dim(kp, start, 3 * BLOCK, 1)
        vb = lax.dynamic_slice_in_dim(vp, start, 3 * BLOCK, 1)
        kpos = start - BLOCK + kj
        mask = in_window & (kpos >= 0) & (kpos < L)
        s_loc = jnp.einsum('bqhgd,bkhd->bhgqk', qb, kb).astype(jnp.float32) * scale
        s_loc = jnp.where(mask, s_loc, NEG_INF)
        s_ctx = jnp.einsum('bqhgd,bchd->bhgqc', q0, ck).astype(jnp.float32) * scale
        s_snk = jnp.broadcast_to(sink_logit, s_loc.shape[:-1] + (1,))
        p = jax.nn.softmax(jnp.concatenate([s_loc, s_ctx, s_snk], axis=-1), axis=-1).astype(v.dtype)
        o = (jnp.einsum('bhgqk,bkhd->bqhgd', p[..., :3 * BLOCK], vb)
             + jnp.einsum('bhgqc,bchd->bqhgd', p[..., 3 * BLOCK:3 * BLOCK + Lc], cv))
        return o.reshape(B, BLOCK, ATTN_W)

    out = lax.map(block, jnp.arange(nb))
    return jnp.moveaxis(out, 0, 1).reshape(B, L, ATTN_W)


def hyena_filters(L, w1, b1, w2, b2, w3, freq):
    f32 = jnp.float32
    t = jnp.linspace(0.0, 1.0, L, dtype=f32)[:, None]
    bands = (FILTER_EMB - 1) // 2
    ang = (2.0 * math.pi / L) * jnp.arange(L, dtype=f32)[:, None]
    fr = jnp.linspace(1e-4, bands - 1, bands, dtype=f32)[None, :]
    feats = jnp.concatenate([t, jnp.cos(fr * ang), -jnp.sin(fr * ang)], axis=-1)
    fq = freq.astype(f32)
    h = jnp.sin(fq * (feats @ w1.astype(f32) + b1.astype(f32)))
    h = jnp.sin(fq * (h @ w2.astype(f32) + b2.astype(f32)))
    h = (h @ w3.astype(f32)).reshape(L, 2, HYENA_ORDER, HYENA_W)
    max_decay = math.log(HYENA_TARGET) / FAST_DECAY_PCT
    min_decay = math.log(HYENA_TARGET) / SLOW_DECAY_PCT
    deltas = jnp.abs(jnp.linspace(min_decay, max_decay, HYENA_W, dtype=f32))
    h = h * jnp.exp(-t * deltas)[:, None, None, :]
    fwd, bwd = h[:, 0], h[:, 1]
    k = jnp.concatenate([fwd, jnp.zeros_like(fwd[:1]), jnp.flip(bwd[1:], axis=0)], axis=0)
    k = k / jnp.sum(jnp.abs(k), axis=0, keepdims=True)
    return jnp.fft.rfft(k, axis=0)


def long_conv(u, kf, d):
    L = u.shape[1]
    uf32 = u.astype(jnp.float32)
    uf = jnp.fft.rfft(uf32, n=2 * L, axis=1)
    y = jnp.fft.irfft(uf * kf[None], n=2 * L, axis=1)[:, :L]
    return (y + uf32 * d.astype(jnp.float32)).astype(u.dtype)


def trunk_layer(x, mod, ck, cv, norm_g, w_in, attn_sink, hy_conv_w, hy_filt_w1, hy_filt_b1,
                hy_filt_w2, hy_filt_b2, hy_filt_w3, hy_filt_freq, hy_d, sc_conv_w, w_out):
    B, L = x.shape[:2]
    shift, scale, gate = jnp.split(mod, 3, axis=-1)
    h = rms_norm(x, norm_g) * (1.0 + scale[:, None]) + shift[:, None]
    proj = h @ w_in
    q, k, v, g_a, hy, g_h, sc, g_c = jnp.split(proj, SPLITS, axis=-1)
    q = q.reshape(B, L, N_HEADS, HEAD_DIM)
    k = k.reshape(B, L, N_KV_HEADS, HEAD_DIM)
    v = v.reshape(B, L, N_KV_HEADS, HEAD_DIM)
    if ck is None:
        attn = context_attention(q, k, v, attn_sink)
    else:
        cos, sin = axial_rope(L, q.dtype)
        attn = latent_attention(apply_rope(q, cos, sin), q, apply_rope(k, cos, sin), v, ck, cv, attn_sink)
    hy = dwconv3(hy, hy_conv_w)
    hv, hx1, hx2 = jnp.split(hy, 3, axis=-1)
    kf = hyena_filters(L, hy_filt_w1, hy_filt_b1, hy_filt_w2, hy_filt_b2, hy_filt_w3, hy_filt_freq)
    z = hx1 * long_conv(hv, kf[:, 0], hy_d[0])
    z = hx2 * long_conv(z, kf[:, 1], hy_d[1])
    sb, scg, sx = jnp.split(sc, 3, axis=-1)
    sc_out = sb * dwconv3(scg * sx, sc_conv_w)
    mixed = jnp.concatenate([attn * jax.nn.silu(g_a), z * jax.nn.silu(g_h), sc_out * jax.nn.silu(g_c)], axis=-1)
    return x + gate[:, None] * (mixed @ w_out), k, v


def setup_inputs(seed: int = 0) -> dict:
    key = jax.random.key(seed)
    ks = jax.random.split(key, 22)
    f32 = jnp.float32

    def nrm(k, shape, s):
        return jax.random.normal(k, shape, f32) * s

    cache_shape = (DEC_BATCH, DEPTH, PAST_LEN, N_KV_HEADS, HEAD_DIM)
    return {
        'x_prompt': nrm(ks[0], (BATCH, SEQ, D_MODEL), 1.0),
        'x_sample': nrm(ks[1], (DEC_BATCH, DEC_SEQ, D_MODEL), 1.0),
        'cache_k': nrm(ks[2], cache_shape, 1.0),
        'cache_v': nrm(ks[3], cache_shape, 1.0),
        'c': nrm(ks[4], (DEC_BATCH, D_MODEL), 1.0),
        'c_ctx': nrm(ks[5], (D_MODEL,), 1.0),
        'norm_g': 1.0 + nrm(ks[6], (DEPTH, D_MODEL), 0.1),
        'mod_w': nrm(ks[7], (DEPTH, D_MODEL, 3 * D_MODEL), D_MODEL ** -0.5),
        'mod_b': nrm(ks[8], (DEPTH, 3 * D_MODEL), 0.02),
        'w_in': nrm(ks[9], (DEPTH, D_MODEL, IN_COLS), D_MODEL ** -0.5),
        'attn_sink': nrm(ks[10], (DEPTH, N_HEADS), 0.5),
        'hy_conv_w': nrm(ks[11], (DEPTH, 3, 3 * HYENA_W), 3 ** -0.5),
        'hy_filt_w1': nrm(ks[12], (DEPTH, FILTER_EMB, FILTER_HIDDEN), FILTER_EMB ** -0.5),
        'hy_filt_b1': nrm(ks[13], (DEPTH, FILTER_HIDDEN), 0.02),
        'hy_filt_w2': nrm(ks[14], (DEPTH, FILTER_HIDDEN, FILTER_HIDDEN), FILTER_HIDDEN ** -0.5),
        'hy_filt_b2': nrm(ks[15], (DEPTH, FILTER_HIDDEN), 0.02),
        'hy_filt_w3': nrm(ks[16], (DEPTH, FILTER_HIDDEN, 2 * HYENA_ORDER * HYENA_W), FILTER_HIDDEN ** -0.5),
        'hy_filt_freq': 1.0 + nrm(ks[17], (DEPTH, FILTER_HIDDEN), 0.1),
        'hy_d': nrm(ks[18], (DEPTH, HYENA_ORDER, HYENA_W), 0.5),
        'sc_conv_w': nrm(ks[19], (DEPTH, 3, CONV_W), 3 ** -0.5),
        'w_out': nrm(ks[20], (DEPTH, MIX_W, D_MODEL), MIX_W ** -0.5),
        'final_g': 1.0 + nrm(ks[21], (D_MODEL,), 0.1),
    }


def reference(x_prompt, x_sample, cache_k, cache_v, c, c_ctx, norm_g, mod_w, mod_b, w_in, attn_sink,
              hy_conv_w, hy_filt_w1, hy_filt_b1, hy_filt_w2, hy_filt_b2, hy_filt_w3, hy_filt_freq, hy_d,
              sc_conv_w, w_out, final_g):
    cond_ctx = jax.nn.silu(c_ctx)[None, :]
    cond_lat = jax.nn.silu(c)
    xp, xs = x_prompt, x_sample
    ks_new, vs_new = [], []
    for l in range(DEPTH):
        lw = (norm_g[l], w_in[l], attn_sink[l], hy_conv_w[l], hy_filt_w1[l], hy_filt_b1[l], hy_filt_w2[l],
              hy_filt_b2[l], hy_filt_w3[l], hy_filt_freq[l], hy_d[l], sc_conv_w[l], w_out[l])
        mod_p = cond_ctx @ mod_w[l] + mod_b[l]
        mod_s = cond_lat @ mod_w[l] + mod_b[l]
        xp, k_l, v_l = trunk_layer(xp, mod_p, None, None, *lw)
        xs, _, _ = trunk_layer(xs, mod_s, cache_k[:, l], cache_v[:, l], *lw)
        ks_new.append(k_l)
        vs_new.append(v_l)
    new_cache_k = jnp.stack(ks_new, axis=1)
    new_cache_v = jnp.stack(vs_new, axis=1)
    y_prompt = rms_norm(xp, final_g)
    y_sample = rms_norm(xs, final_g)
    return (y_prompt, y_sample, new_cache_k, new_cache_v)
```

```python
import functools
import math

import numpy as np
import jax
import jax.numpy as jnp
from jax import lax
from jax.experimental import pallas as pl
from jax.experimental.pallas import tpu as pltpu

F32 = jnp.float32
BF16 = jnp.bfloat16

D_MODEL = 1024
DEPTH = 4
GRID_W = 64
N_HEADS = 8
N_KV_HEADS = 2
HEAD_DIM = 64
ATTN_W = N_HEADS * HEAD_DIM
KV_W = N_KV_HEADS * HEAD_DIM
HYENA_W = 256
CONV_W = 256
WINDOW = 128
BLOCK = 128
FILTER_EMB = 33
FILTER_HIDDEN = 64
HYENA_TARGET = 1e-2
FAST_DECAY_PCT = 0.3
SLOW_DECAY_PCT = 1.5
ROPE_BASE = 10000.0
RMS_EPS = 1e-6
NEG_INF = -1e30
IN_COLS = 3328
C_Q, C_K, C_V, C_GA, C_CV = 0, 512, 640, 768, 1280
CV_W = 2048

LANE = 128
ROWS = 1024
N2_STEP = 16
VMEM_LIMIT = 56 * 1024 * 1024
PITCH_PAD = 8


class Group:
    def __init__(self, batch, seq, n1, n2):
        self.B, self.L, self.N1, self.N2 = batch, seq, n1, n2
        self.N1h = n1 // 2
        assert self.N1h * n2 == seq
        self.two_stage = n2 > 1


def _mm(a, b):
    return jnp.dot(a, b, preferred_element_type=F32)


def _mm_hi(a, b):
    return jnp.dot(a, b, preferred_element_type=F32, precision=lax.Precision.HIGHEST)


def _silu(x):
    return x * (1.0 / (1.0 + jnp.exp(-x)))


def _params(*sem):
    return pltpu.CompilerParams(dimension_semantics=sem, vmem_limit_bytes=VMEM_LIMIT)


def _rope_tables(seq):
    t = np.arange(seq)
    n_freq = HEAD_DIM // 4
    inv = ROPE_BASE ** (-np.arange(n_freq, dtype=np.float64) / n_freq)
    row = (t // GRID_W)[:, None] * inv
    col = (t % GRID_W)[:, None] * inv
    cos = np.concatenate([np.cos(row), np.cos(row), np.cos(col), np.cos(col)], axis=1)
    sin = np.concatenate([-np.sin(row), np.sin(row), -np.sin(col), np.sin(col)], axis=1)
    return (np.tile(cos, (1, 2)).astype(np.float32), np.tile(sin, (1, 2)).astype(np.float32))


def _conv_order(grp, table):
    return table.reshape(grp.N1h, grp.N2, -1).transpose(1, 0, 2).reshape(grp.L, -1)


def _filter_features(grp):
    L = grp.L
    t = np.linspace(0.0, 1.0, L)[:, None]
    bands = (FILTER_EMB - 1) // 2
    ang = (2.0 * math.pi / L) * np.arange(L)[:, None]
    fr = np.linspace(1e-4, bands - 1, bands)[None, :]
    feats = np.concatenate([t, np.cos(fr * ang), -np.sin(fr * ang)], axis=-1)
    feats = np.pad(feats, ((0, 0), (0, LANE - FILTER_EMB)))
    fwd = _conv_order(grp, feats)
    lag = (L - np.arange(L)) % L
    bwd = _conv_order(grp, feats[lag])
    return fwd.astype(np.float32), bwd.astype(np.float32)


def _decay_rates():
    max_decay = math.log(HYENA_TARGET) / FAST_DECAY_PCT
    min_decay = math.log(HYENA_TARGET) / SLOW_DECAY_PCT
    return np.abs(np.linspace(min_decay, max_decay, HYENA_W))[None, :].astype(np.float32)


def _dft_tables(grp):
    n1, n2 = grp.N1, grp.N2
    n = n1 * n2
    k = np.arange(n1)
    f = np.exp(-2j * np.pi * ((k[:, None] * k[None, :]) % n1) / n1)
    fh = f[:, : grp.N1h]
    fa = np.block([[fh.real, -fh.imag], [fh.imag, fh.real]])
    faf = np.concatenate([f.real, f.imag], axis=0) / n
    g = None
    if grp.two_stage:
        j = np.arange(n2)
        ph = (k[:, None, None] * j[None, None, :] + n1 * j[None, :, None] * j[None, None, :]) % n
        gc = np.exp(-2j * np.pi * ph / n)
        g = np.concatenate([np.concatenate([gc.real, -gc.imag], axis=2),
                            np.concatenate([gc.imag, gc.real], axis=2)], axis=1)
        g = g.astype(np.float32)
    return fa.astype(np.float32), faf.astype(np.float32), g


def _mod_kernel(c_ref, w_ref, b_ref, o_ref):
    o_ref[...] = _mm_hi(_silu(c_ref[...]), w_ref[...]) + b_ref[...]


def _modulation(cond, mod_w, mod_b):
    nb = 3 * D_MODEL // 1024
    return pl.pallas_call(
        _mod_kernel,
        grid=(DEPTH, nb),
        in_specs=[
            pl.BlockSpec((16, D_MODEL), lambda l, j: (0, 0)),
            pl.BlockSpec((None, D_MODEL, 1024), lambda l, j: (l, 0, j)),
            pl.BlockSpec((None, 1, 1024), lambda l, j: (l, 0, j)),
        ],
        out_specs=pl.BlockSpec((None, 16, 1024), lambda l, j: (l, 0, j)),
        out_shape=jax.ShapeDtypeStruct((DEPTH, 16, 3 * D_MODEL), F32),
        compiler_params=_params("arbitrary", "arbitrary"),
        name="modulation",
    )(cond, mod_w, mod_b.reshape(DEPTH, 1, 3 * D_MODEL))


def _lane_lt64(shape):
    return lax.broadcasted_iota(jnp.int32, shape, 1) < HEAD_DIM


def _rope(x, cos, sin):
    lane = lax.broadcasted_iota(jnp.int32, x.shape, 1)
    first = (lane % 32) < 16
    partner = jnp.where(first, pltpu.roll(x, LANE - 16, axis=1), pltpu.roll(x, 16, axis=1))
    return x * cos + partner * sin


def _in_proj_kernel(*refs, rope, permute):
    if rope:
        x_ref, mod_ref, g_ref, w_ref, cos_ref, sin_ref = refs[:6]
        outs = refs[6:]
    else:
        x_ref, mod_ref, g_ref, w_ref = refs[:4]
        outs = refs[4:]
    if permute:
        qq_ref, kk_ref, vv_ref, ga_ref, cv_ref, tmp_ref = outs
    else:
        qq_ref, kk_ref, vv_ref, ga_ref, cv_ref, kf_ref, vf_ref = outs

    x = x_ref[...].reshape(ROWS, D_MODEL)
    ms = jnp.mean(x * x, axis=-1, keepdims=True)
    y = x * lax.rsqrt(ms + RMS_EPS) * g_ref[...]
    shift = mod_ref[:, 0:D_MODEL]
    scale = mod_ref[:, D_MODEL:2 * D_MODEL]
    h = (y * (1.0 + scale) + shift).astype(BF16)

    if rope:
        cos = cos_ref[...].reshape(ROWS, LANE)
        sin = sin_ref[...].reshape(ROWS, LANE)
    lt64 = _lane_lt64((ROWS, LANE))

    def put(ref, lo, val):
        lead = ref.shape[:-1]
        width = val.shape[-1]
        ref[(slice(None),) * len(lead) + (slice(lo, lo + width),)] = val.reshape(lead + (width,))

    yq = _mm(h, w_ref[:, C_Q:C_Q + ATTN_W]) * (HEAD_DIM ** -0.5)
    for j in range(ATTN_W // LANE):
        xj = yq[:, j * LANE:(j + 1) * LANE]
        rj = _rope(xj, cos, sin) if rope else xj
        even = jnp.where(lt64, rj, pltpu.roll(xj, HEAD_DIM, axis=1))
        odd = jnp.where(lt64, pltpu.roll(rj, HEAD_DIM, axis=1), xj)
        put(qq_ref, (2 * j) * LANE, even.astype(BF16))
        put(qq_ref, (2 * j + 1) * LANE, odd.astype(BF16))

    yk = _mm(h, w_ref[:, C_K:C_K + KV_W])
    rk = _rope(yk, cos, sin) if rope else yk
    put(kk_ref, 0, jnp.where(lt64, rk, 0.0).astype(BF16))
    put(kk_ref, LANE, jnp.where(lt64, pltpu.roll(rk, HEAD_DIM, axis=1), 0.0).astype(BF16))
    yv = _mm(h, w_ref[:, C_V:C_V + KV_W])
    put(vv_ref, 0, yv.astype(BF16))
    if not permute:
        put(kf_ref, 0, yk)
        put(vf_ref, 0, yv)

    put(ga_ref, 0, _silu(_mm(h, w_ref[:, C_GA:C_GA + ATTN_W])).astype(BF16))

    chunk = 512
    for c in range(CV_W // chunk):
        yc = _mm(h, w_ref[:, C_CV + c * chunk:C_CV + (c + 1) * chunk])
        if c % 2 == 1:
            yc = jnp.concatenate([yc[:, :256], _silu(yc[:, 256:])], axis=1)
        if not permute:
            put(cv_ref, c * chunk, yc.astype(BF16))
            continue
        for sl in range(chunk // LANE):
            tmp_ref[sl] = yc[:, sl * LANE:(sl + 1) * LANE]
        n1h = ROWS // N2_STEP
        for s in range(N2_STEP):
            for sl in range(chunk // LANE):
                lo = c * chunk + sl * LANE
                cv_ref[s, :, lo:lo + LANE] = tmp_ref[sl, pl.ds(s, n1h, stride=N2_STEP), :].astype(BF16)


def _in_proj(grp, x, mod, g, w, rope_tabs):
    B, L = grp.B, grp.L
    const2 = lambda *_: (0, 0)
    if grp.two_stage:
        n1h, n2 = grp.N1h, grp.N2
        grid = (B, n2 // N2_STEP)
        xv = x.reshape(B, n1h, n2, D_MODEL)
        nat = lambda wd: pl.BlockSpec((None, n1h, N2_STEP, wd), lambda b, j: (b, 0, j, 0))
        nat_shape = lambda wd, dt: jax.ShapeDtypeStruct((B, n1h, n2, wd), dt)
        in_specs = [nat(D_MODEL),
                    pl.BlockSpec((None, 1, 3 * D_MODEL), lambda b, j: (b, 0, 0)),
                    pl.BlockSpec((1, D_MODEL), const2),
                    pl.BlockSpec((D_MODEL, IN_COLS), const2),
                    pl.BlockSpec((n1h, N2_STEP, LANE), lambda b, j: (0, j, 0)),
                    pl.BlockSpec((n1h, N2_STEP, LANE), lambda b, j: (0, j, 0))]
        args = [xv, mod, g, w, rope_tabs[0].reshape(n1h, n2, LANE), rope_tabs[1].reshape(n1h, n2, LANE)]
        out_specs = [nat(2 * ATTN_W), nat(2 * LANE), nat(KV_W), nat(ATTN_W),
                     pl.BlockSpec((None, N2_STEP, n1h, CV_W), lambda b, j: (b, j, 0, 0))]
        out_shape = [nat_shape(2 * ATTN_W, BF16), nat_shape(2 * LANE, BF16), nat_shape(KV_W, BF16),
                     nat_shape(ATTN_W, BF16), jax.ShapeDtypeStruct((B, n2, n1h, CV_W), BF16)]
        scratch = [pltpu.VMEM((4, ROWS, LANE), F32)]
        kern = functools.partial(_in_proj_kernel, rope=True, permute=True)
        sem = ("arbitrary", "arbitrary")
    else:
        bb = ROWS // L
        grid = (B // bb,)
        nat = lambda wd: pl.BlockSpec((bb, L, wd), lambda i: (i, 0, 0))
        nat_shape = lambda wd, dt: jax.ShapeDtypeStruct((B, L, wd), dt)
        in_specs = [nat(D_MODEL),
                    pl.BlockSpec((None, 1, 3 * D_MODEL), lambda i: (0, 0, 0)),
                    pl.BlockSpec((1, D_MODEL), const2),
                    pl.BlockSpec((D_MODEL, IN_COLS), const2)]
        args = [x, mod, g, w]
        out_specs = [nat(2 * ATTN_W), nat(2 * LANE), nat(KV_W), nat(ATTN_W), nat(CV_W), nat(KV_W), nat(KV_W)]
        out_shape = [nat_shape(2 * ATTN_W, BF16), nat_shape(2 * LANE, BF16), nat_shape(KV_W, BF16),
                     nat_shape(ATTN_W, BF16), nat_shape(CV_W, BF16), nat_shape(KV_W, F32), nat_shape(KV_W, F32)]
        scratch = []
        kern = functools.partial(_in_proj_kernel, rope=False, permute=False)
        sem = ("arbitrary",)
    outs = pl.pallas_call(
        kern, grid=grid, in_specs=in_specs, out_specs=out_specs, out_shape=out_shape,
        scratch_shapes=scratch, compiler_params=_params(*sem),
        name="in_proj_lat" if grp.two_stage else "in_proj_ctx",
    )(*args)
    outs = list(outs)
    if grp.two_stage:
        outs[:4] = [o.reshape(B, L, o.shape[-1]) for o in outs[:4]]
    else:
        outs[4] = outs[4].reshape(B, 1, L, CV_W)
    return outs


def _attn_kernel(*refs, lq, local, nblocks):
    if local:
        sink_ref, qq_ref, kk_ref, vv_ref, ck_ref, cv_ref, ga_ref, o_ref = refs
        n = pl.program_id(1)
        starts = [pl.multiple_of(jnp.maximum(n - 1, 0) * BLOCK, BLOCK),
                  pl.multiple_of(n * BLOCK, BLOCK),
                  pl.multiple_of(jnp.minimum(n + 1, nblocks - 1) * BLOCK, BLOCK)]
    else:
        sink_ref, qq_ref, kk_ref, vv_ref, ga_ref, o_ref = refs
    group = N_HEADS // N_KV_HEADS
    rows = group * lq
    nt = (((1,), (1,)), ((), ()))
    lt64 = _lane_lt64((lq, LANE))
    row_head = lax.broadcasted_iota(jnp.int32, (rows, 1), 0) // lq

    for g in range(N_KV_HEADS):
        lhs = jnp.concatenate(
            [qq_ref[:, (group * g + i) * LANE:(group * g + i + 1) * LANE] for i in range(group)], axis=0)
        gl = slice(g * LANE, (g + 1) * LANE)
        if local:
            kwin = jnp.concatenate([kk_ref[pl.ds(s, BLOCK), gl] for s in starts], axis=0)
            vwin = jnp.concatenate([vv_ref[pl.ds(s, BLOCK), :] for s in starts], axis=0)
            s_loc = lax.dot_general(lhs, kwin, nt, preferred_element_type=F32)
            qi = lax.broadcasted_iota(jnp.int32, s_loc.shape, 0) % BLOCK
            kj = lax.broadcasted_iota(jnp.int32, s_loc.shape, 1)
            ok = (jnp.abs(kj - BLOCK - qi) <= WINDOW)
            ok = ok & ((kj >= BLOCK) | (n > 0)) & ((kj < 2 * BLOCK) | (n < nblocks - 1))
            s_loc = jnp.where(ok, s_loc, NEG_INF)
            s_ctx = lax.dot_general(lhs, ck_ref[:, gl], nt, preferred_element_type=F32)
        else:
            kwin = kk_ref[:, gl]
            vwin = vv_ref[...]
            s_loc = lax.dot_general(lhs, kwin, nt, preferred_element_type=F32)
        snk = jnp.zeros((rows, 1), F32)
        for i in range(group):
            snk = jnp.where(row_head == i, sink_ref[group * g + i], snk)
        m = jnp.maximum(jnp.max(s_loc, axis=-1, keepdims=True), snk)
        if local:
            m = jnp.maximum(m, jnp.max(s_ctx, axis=-1, keepdims=True))
        p_loc = jnp.exp(s_loc - m)
        den = jnp.sum(p_loc, axis=-1, keepdims=True) + jnp.exp(snk - m)
        o = _mm(p_loc.astype(BF16), vwin)
        if local:
            p_ctx = jnp.exp(s_ctx - m)
            den = den + jnp.sum(p_ctx, axis=-1, keepdims=True)
            o = o + _mm(p_ctx.astype(BF16), cv_ref[...])
        o = o / den
        for jj in range(group // 2):
            a = o[(2 * jj) * lq:(2 * jj + 1) * lq]
            b = o[(2 * jj + 1) * lq:(2 * jj + 2) * lq]
            if g == 0:
                tile = jnp.where(lt64, a, pltpu.roll(b, HEAD_DIM, axis=1))
            else:
                tile = jnp.where(lt64, pltpu.roll(a, HEAD_DIM, axis=1), b)
            j = (group // 2) * g + jj
            cols = slice(j * LANE, (j + 1) * LANE)
            o_ref[:, cols] = (tile * ga_ref[:, cols].astype(F32)).astype(BF16)


def _attention(grp, sink, qq, kk, vv, ga, ctx=None):
    B, L = grp.B, grp.L
    smem = pl.BlockSpec(memory_space=pltpu.SMEM)
    if ctx is not None:
        ck, cv = ctx
        nb = L // BLOCK
        lc = ck.shape[1]
        grid = (B, nb)
        blk = lambda wd: pl.BlockSpec((None, BLOCK, wd), lambda b, n: (b, n, 0))
        full = lambda rows, wd: pl.BlockSpec((None, rows, wd), lambda b, n: (b, 0, 0))
        in_specs = [smem, blk(2 * ATTN_W), full(L, 2 * LANE), full(L, KV_W), full(lc, 2 * LANE), full(lc, KV_W),
                    blk(ATTN_W)]
        args = [sink, qq, kk, vv, ck, cv, ga]
        out_spec = blk(ATTN_W)
        kern = functools.partial(_attn_kernel, lq=BLOCK, local=True, nblocks=nb)
        sem = ("arbitrary", "arbitrary")
    else:
        grid = (B,)
        full = lambda wd: pl.BlockSpec((None, L, wd), lambda b: (b, 0, 0))
        in_specs = [smem, full(2 * ATTN_W), full(2 * LANE), full(KV_W), full(ATTN_W)]
        args = [sink, qq, kk, vv, ga]
        out_spec = full(ATTN_W)
        kern = functools.partial(_attn_kernel, lq=L, local=False, nblocks=1)
        sem = ("arbitrary",)
    return pl.pallas_call(
        kern, grid=grid, in_specs=in_specs, out_specs=out_spec,
        out_shape=jax.ShapeDtypeStruct((B, L, ATTN_W), BF16),
        compiler_params=_params(*sem), name="attn_lat" if ctx is not None else "attn_ctx",
    )(*args)


def _shift_rows(s, up):
    rows = s.shape[0]
    r = lax.broadcasted_iota(jnp.int32, s.shape, 0)
    if up:
        return jnp.where(r == rows - 1, 0.0, pltpu.roll(s, rows - 1, axis=0))
    return jnp.where(r == 0, 0.0, pltpu.roll(s, 1, axis=0))


def _dwconv3(u, w):
    n2 = u.shape[0]
    first_prev = _shift_rows(u[n2 - 1], up=False)[None]
    last_next = _shift_rows(u[0], up=True)[None]
    if n2 == 1:
        prev, nxt = first_prev, last_next
    else:
        prev = jnp.concatenate([first_prev, u[:-1]], axis=0)
        nxt = jnp.concatenate([u[1:], last_next], axis=0)
    return prev * w[0:1][None] + u * w[1:2][None] + nxt * w[2:3][None]


def _pre_kernel(hv_ref, hx1_ref, hx2_ref, gh_ref, sb_ref, sc_ref, sx_ref, gc_ref, w_ref,
                hvc_ref, hx1c_ref, hx2g_ref, scg_ref):
    f = lambda r, b: r[b].astype(F32)
    for b in range(hv_ref.shape[0]):
        hvc_ref[b] = _dwconv3(f(hv_ref, b), w_ref[:, 0, :]).astype(BF16)
        hx1c_ref[b] = _dwconv3(f(hx1_ref, b), w_ref[:, 1, :]).astype(BF16)
        hx2g_ref[b] = (_dwconv3(f(hx2_ref, b), w_ref[:, 2, :]) * f(gh_ref, b)).astype(BF16)
        inner = _dwconv3(f(sc_ref, b) * f(sx_ref, b), w_ref[:, 3, :])
        scg_ref[b] = (f(sb_ref, b) * inner * f(gc_ref, b)).astype(BF16)


def _pre(grp, cv, wconv):
    B, n2, n1h = grp.B, grp.N2, grp.N1h
    bb = 1 if grp.two_stage else 4
    ncb = HYENA_W // LANE
    col = lambda k: pl.BlockSpec((bb, n2, n1h, LANE), lambda b, c, k=k: (b, 0, 0, k * ncb + c))
    out_spec = pl.BlockSpec((bb, n2, n1h, LANE), lambda b, c: (b, 0, 0, c))
    shape = jax.ShapeDtypeStruct((B, n2, n1h, HYENA_W), BF16)
    return pl.pallas_call(
        _pre_kernel, grid=(B // bb, ncb),
        in_specs=[col(k) for k in range(8)] + [pl.BlockSpec((3, 4, LANE), lambda b, c: (0, 0, c))],
        out_specs=[out_spec] * 4, out_shape=[shape] * 4,
        compiler_params=_params("arbitrary", "arbitrary"),
        name="pre_lat" if grp.two_stage else "pre_ctx",
    )(*([cv] * 8), wconv)


def _filter_kernel(ff_ref, fb_ref, w1_ref, b1_ref, w2_ref, b2_ref, w3_ref, fq_ref, dl_ref,
                   kf_ref, kb_ref, *, chunk):
    rows = ff_ref.shape[0]
    width = 2 * HYENA_W
    fq = fq_ref[...]

    def mlp(feats, w3):
        h = jnp.sin(fq * (_mm_hi(feats, w1_ref[...]) + b1_ref[...]))
        h = jnp.sin(fq * (_mm_hi(h, w2_ref[...]) + b2_ref[...]))
        out = _mm_hi(h, w3)
        dec = jnp.exp(-(feats[:, 0:1] * dl_ref[...]))
        return out * jnp.concatenate([dec, dec], axis=1)

    def body(i, acc):
        r0 = pl.multiple_of(i * chunk, chunk)
        kf = mlp(ff_ref[pl.ds(r0, chunk), :], w3_ref[:, 0:width])
        kb = mlp(fb_ref[pl.ds(r0, chunk), :], w3_ref[:, width:2 * width])
        rid = lax.broadcasted_iota(jnp.int32, kb.shape, 0) + r0
        kb = jnp.where(rid == 0, 0.0, kb)
        kf_ref[pl.ds(r0, chunk), :] = kf
        kb_ref[pl.ds(r0, chunk), :] = kb
        return acc + jnp.sum(jnp.abs(kf) + jnp.abs(kb), axis=0, keepdims=True)

    total = lax.fori_loop(0, rows // chunk, body, jnp.zeros((1, width), F32))

    def scale(i, c):
        r0 = pl.multiple_of(i * chunk, chunk)
        kf_ref[pl.ds(r0, chunk), :] = kf_ref[pl.ds(r0, chunk), :] / total
        kb_ref[pl.ds(r0, chunk), :] = kb_ref[pl.ds(r0, chunk), :] / total
        return c

    lax.fori_loop(0, rows // chunk, scale, 0)


def _filters(grp, feats, w1, b1, w2, b2, w3, fq, deltas):
    L = grp.L
    chunk = min(512, L)
    c2 = lambda l: (0, 0)
    per = lambda a, b: pl.BlockSpec((None, a, b), lambda l: (l, 0, 0))
    shape = jax.ShapeDtypeStruct((DEPTH, L, 2 * HYENA_W), F32)
    return pl.pallas_call(
        functools.partial(_filter_kernel, chunk=chunk), grid=(DEPTH,),
        in_specs=[pl.BlockSpec((L, LANE), c2), pl.BlockSpec((L, LANE), c2),
                  per(LANE, FILTER_HIDDEN), per(1, FILTER_HIDDEN), per(FILTER_HIDDEN, FILTER_HIDDEN),
                  per(1, FILTER_HIDDEN), per(FILTER_HIDDEN, 4 * HYENA_W), per(1, FILTER_HIDDEN),
                  pl.BlockSpec((1, HYENA_W), c2)],
        out_specs=[per(L, 2 * HYENA_W)] * 2, out_shape=[shape] * 2,
        compiler_params=_params("arbitrary"),
        name="filters_lat" if grp.two_stage else "filters_ctx",
    )(feats[0], feats[1], w1, b1, w2, b2, w3, fq, deltas)


def _store_spectrum_rows(s_ref, base, a, n1):
    for comp in range(2):
        for sl in range(2):
            s_ref[comp, sl, pl.ds(base, n1), :] = a[comp * n1:(comp + 1) * n1, sl * LANE:(sl + 1) * LANE]


def _load_column(s_ref, k1, n2, pitch):
    parts = [jnp.concatenate([s_ref[comp, sl, pl.ds(k1, n2, stride=pitch), :] for sl in range(2)], axis=1)
             for comp in range(2)]
    return jnp.concatenate(parts, axis=0).astype(BF16)


def _spectrum2_kernel(kf_ref, kb_ref, faf_ref, g_ref, o_ref, s_ref, *, n1, n2):
    n1h = n1 // 2
    pitch = n1 + PITCH_PAD
    faf = faf_ref[...]

    def stage_a(j, c):
        r0 = pl.multiple_of(j * n1h, n1h)
        rhs = jnp.concatenate([kf_ref[pl.ds(r0, n1h), :], kb_ref[pl.ds(r0, n1h), :]], axis=0).astype(BF16)
        _store_spectrum_rows(s_ref, pl.multiple_of(j * pitch, 8), _mm(faf, rhs), n1)
        return c

    lax.fori_loop(0, n2, stage_a, 0)

    def stage_c(k1, c):
        o_ref[k1] = _mm(g_ref[k1], _load_column(s_ref, k1, n2, pitch)).astype(BF16)
        return c

    lax.fori_loop(0, n1, stage_c, 0)


def _spectrum1_kernel(kf_ref, kb_ref, faf_ref, o_ref):
    rhs = jnp.concatenate([kf_ref[...], kb_ref[...]], axis=0).astype(BF16)
    o_ref[...] = _mm(faf_ref[...], rhs).astype(BF16)


def _spectrum(grp, kf, kb, faf, g):
    n1, n2, L = grp.N1, grp.N2, grp.L
    wd = 2 * LANE
    nblk = 2 * HYENA_W // wd
    kin = pl.BlockSpec((None, L, wd), lambda l, c: (l, 0, c))
    if grp.two_stage:
        pitch = n1 + PITCH_PAD
        return pl.pallas_call(
            functools.partial(_spectrum2_kernel, n1=n1, n2=n2), grid=(DEPTH, nblk),
            in_specs=[kin, kin, pl.BlockSpec((2 * n1, n1), lambda l, c: (0, 0)),
                      pl.BlockSpec((n1, 2 * n2, 2 * n2), lambda l, c: (0, 0, 0))],
            out_specs=pl.BlockSpec((None, n1, 2 * n2, wd), lambda l, c: (l, 0, 0, c)),
            out_shape=jax.ShapeDtypeStruct((DEPTH, n1, 2 * n2, 2 * HYENA_W), BF16),
            scratch_shapes=[pltpu.VMEM((2, 2, n2 * pitch, LANE), F32)],
            compiler_params=_params("arbitrary", "arbitrary"), name="spectrum_lat",
        )(kf, kb, faf, g)
    return pl.pallas_call(
        _spectrum1_kernel, grid=(DEPTH, nblk),
        in_specs=[kin, kin, pl.BlockSpec((2 * n1, n1), lambda l, c: (0, 0))],
        out_specs=pl.BlockSpec((None, 2 * n1, wd), lambda l, c: (l, 0, c)),
        out_shape=jax.ShapeDtypeStruct((DEPTH, 2 * n1, 2 * HYENA_W), BF16),
        compiler_params=_params("arbitrary", "arbitrary"), name="spectrum_ctx",
    )(kf, kb, faf)


def _pair_operand(u_ref, j):
    re = jnp.concatenate([u_ref[0, j], u_ref[2, j]], axis=1)
    im = jnp.concatenate([u_ref[1, j], u_ref[3, j]], axis=1)
    return jnp.concatenate([re, im], axis=0)


def _cmul(x, k, half):
    k = k.astype(F32)
    kr = jnp.concatenate([k[:half], k[:half]], axis=1)
    ki = jnp.concatenate([k[half:], k[half:]], axis=1)
    xr, xi = x[:half], x[half:]
    return jnp.concatenate([xr * kr - xi * ki, xr * ki + xi * kr], axis=0).astype(BF16)


def _conv_epilogue(u_ref, m_ref, o_ref, d, y, j, n1h):
    for b in range(4):
        yb = y[(b % 2) * n1h:(b % 2 + 1) * n1h, (b // 2) * LANE:(b // 2 + 1) * LANE]
        u = u_ref[b, j].astype(F32)
        o_ref[b, j] = (m_ref[b, j].astype(F32) * (yb + u * d)).astype(BF16)


def _conv2_kernel(u_ref, m_ref, fa_ref, fat_ref, g_ref, k_ref, d_ref, o_ref, s_ref, *, n1, n2):
    n1h = n1 // 2
    pitch = n1 + PITCH_PAD
    fa = fa_ref[...]
    fat = fat_ref[...]
    d = d_ref[...]

    def stage_a(j, c):
        _store_spectrum_rows(s_ref, pl.multiple_of(j * pitch, 8), _mm(fa, _pair_operand(u_ref, j)), n1)
        return c

    lax.fori_loop(0, n2, stage_a, 0)

    def stage_c(k1, c):
        g = g_ref[k1]
        y = _cmul(_mm(g, _load_column(s_ref, k1, n2, pitch)), k_ref[k1], n2)
        back = lax.dot_general(g, y, (((0,), (0,)), ((), ())), preferred_element_type=F32)
        for comp in range(2):
            for sl in range(2):
                s_ref[comp, sl, pl.ds(k1, n2, stride=pitch), :] = (
                    back[comp * n2:(comp + 1) * n2, sl * LANE:(sl + 1) * LANE])
        return c

    lax.fori_loop(0, n1, stage_c, 0)

    def stage_inv(j, c):
        base = pl.multiple_of(j * pitch, 8)
        parts = [jnp.concatenate([s_ref[comp, sl, pl.ds(base, n1), :] for sl in range(2)], axis=1)
                 for comp in range(2)]
        y = _mm(fat, jnp.concatenate(parts, axis=0).astype(BF16))
        _conv_epilogue(u_ref, m_ref, o_ref, d, y, j, n1h)
        return c

    lax.fori_loop(0, n2, stage_inv, 0)


def _conv1_kernel(u_ref, m_ref, fa_ref, fat_ref, k_ref, d_ref, o_ref, *, n1):
    x = _mm(fa_ref[...], _pair_operand(u_ref, 0))
    y = _mm(fat_ref[...], _cmul(x, k_ref[...], n1))
    _conv_epilogue(u_ref, m_ref, o_ref, d_ref[...], y, 0, n1 // 2)


def _long_conv(grp, u, m, spec, d, order, tabs):
    B, n1, n2, n1h = grp.B, grp.N1, grp.N2, grp.N1h
    ncb = HYENA_W // LANE
    grid = (ncb, B // 4)
    data = pl.BlockSpec((4, n2, n1h, LANE), lambda c, q: (q, 0, 0, c))
    dspec = pl.BlockSpec((1, LANE), lambda c, q: (0, order * ncb + c))
    fa = pl.BlockSpec((2 * n1, n1), lambda c, q: (0, 0))
    fat = pl.BlockSpec((n1, 2 * n1), lambda c, q: (0, 0))
    out_shape = jax.ShapeDtypeStruct((B, n2, n1h, HYENA_W), BF16)
    if grp.two_stage:
        pitch = n1 + PITCH_PAD
        return pl.pallas_call(
            functools.partial(_conv2_kernel, n1=n1, n2=n2), grid=grid,
            in_specs=[data, data, fa, fat,
                      pl.BlockSpec((n1, 2 * n2, 2 * n2), lambda c, q: (0, 0, 0)),
                      pl.BlockSpec((n1, 2 * n2, LANE), lambda c, q: (0, 0, order * ncb + c)),
                      dspec],
            out_specs=data, out_shape=out_shape,
            scratch_shapes=[pltpu.VMEM((2, 2, n2 * pitch, LANE), F32)],
            compiler_params=_params("arbitrary", "arbitrary"), name="long_conv_lat",
        )(u, m, tabs["fa"], tabs["fat"], tabs["g"], spec, d)
    return pl.pallas_call(
        functools.partial(_conv1_kernel, n1=n1), grid=grid,
        in_specs=[data, data, fa, fat,
                  pl.BlockSpec((2 * n1, LANE), lambda c, q: (0, order * ncb + c)), dspec],
        out_specs=data, out_shape=out_shape,
        compiler_params=_params("arbitrary", "arbitrary"), name="long_conv_ctx",
    )(u, m, tabs["fa"], tabs["fat"], spec, d)


def _out_proj_kernel(*refs, permute, final):
    if permute:
        x_ref, mod_ref, a_ref, z_ref, s_ref, w_ref, fg_ref, o_ref, tmp_ref = refs
    else:
        x_ref, mod_ref, a_ref, z_ref, s_ref, w_ref, fg_ref, o_ref = refs
    a = a_ref[...].reshape(ROWS, ATTN_W)
    nsl = HYENA_W // LANE
    if permute:
        n1h = ROWS // N2_STEP
        for s in range(N2_STEP):
            for sl in range(nsl):
                cols = slice(sl * LANE, (sl + 1) * LANE)
                tmp_ref[sl, pl.ds(s, n1h, stride=N2_STEP), :] = z_ref[s, :, cols].astype(F32)
                tmp_ref[nsl + sl, pl.ds(s, n1h, stride=N2_STEP), :] = s_ref[s, :, cols].astype(F32)
        conv = jnp.concatenate([tmp_ref[i] for i in range(2 * nsl)], axis=1).astype(BF16)
    else:
        conv = jnp.concatenate([z_ref[...].reshape(ROWS, HYENA_W), s_ref[...].reshape(ROWS, CONV_W)], axis=1)
    y = _mm(a, w_ref[0:ATTN_W, :]) + _mm(conv, w_ref[ATTN_W:, :])
    gate = mod_ref[:, 2 * D_MODEL:3 * D_MODEL]
    xn = x_ref[...].reshape(ROWS, D_MODEL) + gate * y
    if final:
        ms = jnp.mean(xn * xn, axis=-1, keepdims=True)
        xn = xn * lax.rsqrt(ms + RMS_EPS) * fg_ref[...]
    o_ref[...] = xn.reshape(o_ref.shape)


def _out_proj(grp, x, mod, attn, zg, scg, w, final_g, final):
    B, L = grp.B, grp.L
    const2 = lambda *_: (0, 0)
    if grp.two_stage:
        n1h, n2 = grp.N1h, grp.N2
        grid = (B, n2 // N2_STEP)
        nat = lambda wd: pl.BlockSpec((None, n1h, N2_STEP, wd), lambda b, j: (b, 0, j, 0))
        cvl = pl.BlockSpec((None, N2_STEP, n1h, HYENA_W), lambda b, j: (b, j, 0, 0))
        in_specs = [nat(D_MODEL), pl.BlockSpec((None, 1, 3 * D_MODEL), lambda b, j: (b, 0, 0)),
                    nat(ATTN_W), cvl, cvl, pl.BlockSpec((D_MODEL, D_MODEL), const2),
                    pl.BlockSpec((1, D_MODEL), const2)]
        args = [x.reshape(B, n1h, n2, D_MODEL), mod, attn.reshape(B, n1h, n2, ATTN_W), zg, scg, w, final_g]
        out_spec = nat(D_MODEL)
        out_shape = jax.ShapeDtypeStruct((B, n1h, n2, D_MODEL), F32)
        scratch = [pltpu.VMEM((4, ROWS, LANE), F32)]
        sem = ("arbitrary", "arbitrary")
    else:
        bb = ROWS // L
        grid = (B // bb,)
        nat = lambda wd: pl.BlockSpec((bb, L, wd), lambda i: (i, 0, 0))
        cvl = pl.BlockSpec((bb, None, L, HYENA_W), lambda i: (i, 0, 0, 0))
        in_specs = [nat(D_MODEL), pl.BlockSpec((None, 1, 3 * D_MODEL), lambda i: (0, 0, 0)),
                    nat(ATTN_W), cvl, cvl, pl.BlockSpec((D_MODEL, D_MODEL), const2),
                    pl.BlockSpec((1, D_MODEL), const2)]
        args = [x, mod, attn, zg, scg, w, final_g]
        out_spec = nat(D_MODEL)
        out_shape = jax.ShapeDtypeStruct((B, L, D_MODEL), F32)
        scratch = []
        sem = ("arbitrary",)
    out = pl.pallas_call(
        functools.partial(_out_proj_kernel, permute=grp.two_stage, final=final),
        grid=grid, in_specs=in_specs, out_specs=out_spec, out_shape=out_shape,
        scratch_shapes=scratch, compiler_params=_params(*sem),
        name="out_proj_lat" if grp.two_stage else "out_proj_ctx",
    )(*args)
    return out.reshape(B, L, D_MODEL)


def _group_tables(grp):
    fa, faf, g = _dft_tables(grp)
    tabs = {"fa": jnp.asarray(fa).astype(BF16), "fat": jnp.asarray(fa.T.copy()).astype(BF16),
            "faf": jnp.asarray(faf).astype(BF16)}
    if g is not None:
        tabs["g"] = jnp.asarray(g).astype(BF16)
    ff, fb = _filter_features(grp)
    tabs["feats"] = (jnp.asarray(ff), jnp.asarray(fb))
    return tabs


def kernel(x_prompt, x_sample, cache_k, cache_v, c, c_ctx, norm_g, mod_w, mod_b, w_in, attn_sink,
           hy_conv_w, hy_filt_w1, hy_filt_b1, hy_filt_w2, hy_filt_b2, hy_filt_w3, hy_filt_freq, hy_d,
           sc_conv_w, w_out, final_g):
    ctx = Group(x_prompt.shape[0], x_prompt.shape[1], 2 * x_prompt.shape[1], 1)
    lat = Group(x_sample.shape[0], x_sample.shape[1], 128, 2 * x_sample.shape[1] // 128)
    nlat = lat.B

    cond = jnp.zeros((16, D_MODEL), F32).at[0].set(c_ctx).at[1:1 + nlat].set(c)
    mods = _modulation(cond, mod_w, mod_b)

    w1p = jnp.pad(hy_filt_w1, ((0, 0), (0, LANE - FILTER_EMB), (0, 0)))
    b1 = hy_filt_b1[:, None, :]
    b2 = hy_filt_b2[:, None, :]
    fq = hy_filt_freq[:, None, :]
    deltas = jnp.asarray(_decay_rates())
    tabs, spec = {}, {}
    for grp in (ctx, lat):
        t = _group_tables(grp)
        kf, kb = _filters(grp, t["feats"], w1p, b1, hy_filt_w2, b2, hy_filt_w3, fq, deltas)
        spec[grp] = _spectrum(grp, kf, kb, t["faf"], t.get("g"))
        tabs[grp] = t
    rope = tuple(jnp.asarray(a) for a in _rope_tables(lat.L))

    w_in_b = w_in.astype(BF16)
    w_out_b = w_out.astype(BF16)
    wconv = jnp.concatenate([hy_conv_w.reshape(DEPTH, 3, 3, HYENA_W), sc_conv_w[:, :, None, :]], axis=2)
    dskip = hy_d.reshape(DEPTH, 1, 2 * HYENA_W)
    fg = final_g[None, :]
    lc = cache_k.shape[2]
    zpad = jnp.zeros((nlat, DEPTH, lc, HEAD_DIM), F32)
    ck_pad = jnp.concatenate([zpad, cache_k[:, :, :, 0], zpad, cache_k[:, :, :, 1]], axis=-1).astype(BF16)
    cv_nat = cache_v.reshape(nlat, DEPTH, lc, KV_W).astype(BF16)

    xp, xs = x_prompt, x_sample
    ks_new, vs_new = [], []
    for l in range(DEPTH):
        g = norm_g[l][None, :]
        last = l == DEPTH - 1
        for grp in (ctx, lat):
            is_lat = grp is lat
            x = xs if is_lat else xp
            mod = mods[l, 1:1 + nlat][:, None, :] if is_lat else mods[l, 0:1][:, None, :]
            outs = _in_proj(grp, x, mod, g, w_in_b[l], rope if is_lat else None)
            qq, kk, vv, ga, cv = outs[:5]
            if is_lat:
                attn = _attention(grp, attn_sink[l], qq, kk, vv, ga, ctx=(ck_pad[:, l], cv_nat[:, l]))
            else:
                ks_new.append(outs[5])
                vs_new.append(outs[6])
                attn = _attention(grp, attn_sink[l], qq, kk, vv, ga)
            hvc, hx1c, hx2g, scg = _pre(grp, cv, wconv[l])
            z1 = _long_conv(grp, hvc, hx1c, spec[grp][l], dskip[l], 0, tabs[grp])
            zg = _long_conv(grp, z1, hx2g, spec[grp][l], dskip[l], 1, tabs[grp])
            xn = _out_proj(grp, x, mod, attn, zg, scg, w_out_b[l], fg, last)
            if is_lat:
                xs = xn
            else:
                xp = xn
    shape = (ctx.B, DEPTH, ctx.L, N_KV_HEADS, HEAD_DIM)
    new_k = jnp.stack(ks_new, axis=1).reshape(shape)
    new_v = jnp.stack(vs_new, axis=1).reshape(shape)
    return (xp, xs, new_k, new_v)
```

```python
import functools
import math

import numpy as np
import jax
import jax.numpy as jnp
from jax import lax
from jax.experimental import pallas as pl
from jax.experimental.pallas import tpu as pltpu

F32 = jnp.float32
BF16 = jnp.bfloat16

D_MODEL = 1024
DEPTH = 4
GRID_W = 64
N_HEADS = 8
N_KV_HEADS = 2
HEAD_DIM = 64
ATTN_W = N_HEADS * HEAD_DIM
KV_W = N_KV_HEADS * HEAD_DIM
HYENA_W = 256
CONV_W = 256
WINDOW = 128
BLOCK = 128
FILTER_EMB = 33
FILTER_HIDDEN = 64
HYENA_TARGET = 1e-2
FAST_DECAY_PCT = 0.3
SLOW_DECAY_PCT = 1.5
ROPE_BASE = 10000.0
RMS_EPS = 1e-6
NEG_INF = -1e30
IN_COLS = 3328
C_Q, C_K, C_V, C_GA, C_CV = 0, 512, 640, 768, 1280
CV_W = 2048

LANE = 128
ROWS = 1024
N2_STEP = 16
TMP_PITCH = 24
VMEM_LIMIT = 56 * 1024 * 1024
PITCH_PAD = 8
UNROLL = 8
ATTN_QBLOCKS = 4


class Group:
    def __init__(self, batch, seq, n1, n2):
        self.B, self.L, self.N1, self.N2 = batch, seq, n1, n2
        self.N1h = n1 // 2
        assert self.N1h * n2 == seq
        self.two_stage = n2 > 1


def _mm(a, b):
    return jnp.dot(a, b, preferred_element_type=F32)


def _mm_hi(a, b):
    return jnp.dot(a, b, preferred_element_type=F32, precision=lax.Precision.HIGHEST)


def _silu(x):
    return x * (1.0 / (1.0 + jnp.exp(-x)))


def _params(*sem):
    return pltpu.CompilerParams(dimension_semantics=sem, vmem_limit_bytes=VMEM_LIMIT)


def _rope_tables(seq):
    t = np.arange(seq)
    n_freq = HEAD_DIM // 4
    inv = ROPE_BASE ** (-np.arange(n_freq, dtype=np.float64) / n_freq)
    row = (t // GRID_W)[:, None] * inv
    col = (t % GRID_W)[:, None] * inv
    cos = np.concatenate([np.cos(row), np.cos(row), np.cos(col), np.cos(col)], axis=1)
    sin = np.concatenate([-np.sin(row), np.sin(row), -np.sin(col), np.sin(col)], axis=1)
    return (np.tile(cos, (1, 2)).astype(np.float32), np.tile(sin, (1, 2)).astype(np.float32))


def _conv_order(grp, table):
    return table.reshape(grp.N1h, grp.N2, -1).transpose(1, 0, 2).reshape(grp.L, -1)


def _filter_features(grp):
    L = grp.L
    t = np.linspace(0.0, 1.0, L)[:, None]
    bands = (FILTER_EMB - 1) // 2
    ang = (2.0 * math.pi / L) * np.arange(L)[:, None]
    fr = np.linspace(1e-4, bands - 1, bands)[None, :]
    feats = np.concatenate([t, np.cos(fr * ang), -np.sin(fr * ang)], axis=-1)
    feats = np.pad(feats, ((0, 0), (0, LANE - FILTER_EMB)))
    fwd = _conv_order(grp, feats)
    lag = (L - np.arange(L)) % L
    bwd = _conv_order(grp, feats[lag])
    return fwd.astype(np.float32), bwd.astype(np.float32)


def _decay_rates():
    max_decay = math.log(HYENA_TARGET) / FAST_DECAY_PCT
    min_decay = math.log(HYENA_TARGET) / SLOW_DECAY_PCT
    return np.abs(np.linspace(min_decay, max_decay, HYENA_W))[None, :].astype(np.float32)


def _dft_tables(grp):
    n1, n2 = grp.N1, grp.N2
    n = n1 * n2
    k = np.arange(n1)
    f = np.exp(-2j * np.pi * ((k[:, None] * k[None, :]) % n1) / n1)
    fh = f[:, : grp.N1h]
    fa = np.block([[fh.real, -fh.imag], [fh.imag, fh.real]])
    faf = np.concatenate([f.real, f.imag], axis=0) / n
    g = None
    if grp.two_stage:
        j = np.arange(n2)
        ph = (k[:, None, None] * j[None, None, :] + n1 * j[None, :, None] * j[None, None, :]) % n
        gc = np.exp(-2j * np.pi * ph / n)
        g = np.concatenate([np.concatenate([gc.real, -gc.imag], axis=2),
                            np.concatenate([gc.imag, gc.real], axis=2)], axis=1)
        g = g.astype(np.float32)
    return fa.astype(np.float32), faf.astype(np.float32), g


def _mod_kernel(c_ref, w_ref, b_ref, o_ref):
    o_ref[...] = _mm_hi(_silu(c_ref[...]), w_ref[...]) + b_ref[...]


def _modulation(cond, mod_w, mod_b):
    nb = 3 * D_MODEL // 1024
    return pl.pallas_call(
        _mod_kernel,
        grid=(DEPTH, nb),
        in_specs=[
            pl.BlockSpec((16, D_MODEL), lambda l, j: (0, 0)),
            pl.BlockSpec((None, D_MODEL, 1024), lambda l, j: (l, 0, j)),
            pl.BlockSpec((None, 1, 1024), lambda l, j: (l, 0, j)),
        ],
        out_specs=pl.BlockSpec((None, 16, 1024), lambda l, j: (l, 0, j)),
        out_shape=jax.ShapeDtypeStruct((DEPTH, 16, 3 * D_MODEL), F32),
        compiler_params=_params("arbitrary", "arbitrary"),
        name="modulation",
    )(cond, mod_w, mod_b.reshape(DEPTH, 1, 3 * D_MODEL))


def _lane_lt64(shape):
    return lax.broadcasted_iota(jnp.int32, shape, 1) < HEAD_DIM


def _rope(x, cos, sin):
    lane = lax.broadcasted_iota(jnp.int32, x.shape, 1)
    first = (lane % 32) < 16
    partner = jnp.where(first, pltpu.roll(x, LANE - 16, axis=1), pltpu.roll(x, 16, axis=1))
    return x * cos + partner * sin


def _in_proj_kernel(*refs, rope, permute):
    if rope:
        x_ref, mod_ref, g_ref, w_ref, cos_ref, sin_ref = refs[:6]
        outs = refs[6:]
    else:
        x_ref, mod_ref, g_ref, w_ref = refs[:4]
        outs = refs[4:]
    if permute:
        qq_ref, kk_ref, vv_ref, ga_ref, cv_ref, tmp_ref = outs
    else:
        qq_ref, kk_ref, vv_ref, ga_ref, cv_ref, kf_ref, vf_ref = outs

    x = x_ref[...].reshape(ROWS, D_MODEL)
    ms = jnp.mean(x * x, axis=-1, keepdims=True)
    y = x * lax.rsqrt(ms + RMS_EPS) * g_ref[...]
    shift = mod_ref[:, 0:D_MODEL]
    scale = mod_ref[:, D_MODEL:2 * D_MODEL]
    h = (y * (1.0 + scale) + shift).astype(BF16)

    if rope:
        cos = cos_ref[...].reshape(ROWS, LANE)
        sin = sin_ref[...].reshape(ROWS, LANE)
    lt64 = _lane_lt64((ROWS, LANE))

    def put(ref, lo, val):
        lead = ref.shape[:-1]
        width = val.shape[-1]
        ref[(slice(None),) * len(lead) + (slice(lo, lo + width),)] = val.reshape(lead + (width,))

    yq = _mm(h, w_ref[:, C_Q:C_Q + ATTN_W]) * (HEAD_DIM ** -0.5)
    for j in range(ATTN_W // LANE):
        xj = yq[:, j * LANE:(j + 1) * LANE]
        rj = _rope(xj, cos, sin) if rope else xj
        even = jnp.where(lt64, rj, pltpu.roll(xj, HEAD_DIM, axis=1))
        odd = jnp.where(lt64, pltpu.roll(rj, HEAD_DIM, axis=1), xj)
        put(qq_ref, (2 * j) * LANE, even.astype(BF16))
        put(qq_ref, (2 * j + 1) * LANE, odd.astype(BF16))

    yk = _mm(h, w_ref[:, C_K:C_K + KV_W])
    rk = _rope(yk, cos, sin) if rope else yk
    put(kk_ref, 0, jnp.where(lt64, rk, 0.0).astype(BF16))
    put(kk_ref, LANE, jnp.where(lt64, pltpu.roll(rk, HEAD_DIM, axis=1), 0.0).astype(BF16))
    yv = _mm(h, w_ref[:, C_V:C_V + KV_W])
    put(vv_ref, 0, yv.astype(BF16))
    if not permute:
        put(kf_ref, 0, yk)
        put(vf_ref, 0, yv)

    put(ga_ref, 0, _silu(_mm(h, w_ref[:, C_GA:C_GA + ATTN_W])).astype(BF16))

    chunk = 512
    for c in range(CV_W // chunk):
        yc = _mm(h, w_ref[:, C_CV + c * chunk:C_CV + (c + 1) * chunk])
        if c % 2 == 1:
            yc = jnp.concatenate([yc[:, :256], _silu(yc[:, 256:])], axis=1)
        if not permute:
            put(cv_ref, c * chunk, yc.astype(BF16))
            continue
        n1h = ROWS // N2_STEP
        nsl = chunk // LANE
        for sl in range(nsl):
            for i in range(n1h):
                tmp_ref[c * nsl + sl, i * TMP_PITCH:i * TMP_PITCH + N2_STEP, :] = (
                    yc[i * N2_STEP:(i + 1) * N2_STEP, sl * LANE:(sl + 1) * LANE])
        for s in range(N2_STEP):
            for sl in range(nsl):
                lo = c * chunk + sl * LANE
                cv_ref[s, :, lo:lo + LANE] = (
                    tmp_ref[c * nsl + sl, pl.ds(s, n1h, stride=TMP_PITCH), :].astype(BF16))


def _in_proj(grp, x, mod, g, w, rope_tabs):
    B, L = grp.B, grp.L
    const2 = lambda *_: (0, 0)
    if grp.two_stage:
        n1h, n2 = grp.N1h, grp.N2
        grid = (B, n2 // N2_STEP)
        xv = x.reshape(B, n1h, n2, D_MODEL)
        nat = lambda wd: pl.BlockSpec((None, n1h, N2_STEP, wd), lambda b, j: (b, 0, j, 0))
        nat_shape = lambda wd, dt: jax.ShapeDtypeStruct((B, n1h, n2, wd), dt)
        in_specs = [nat(D_MODEL),
                    pl.BlockSpec((None, 1, 3 * D_MODEL), lambda b, j: (b, 0, 0)),
                    pl.BlockSpec((1, D_MODEL), const2),
                    pl.BlockSpec((D_MODEL, IN_COLS), const2),
                    pl.BlockSpec((n1h, N2_STEP, LANE), lambda b, j: (0, j, 0)),
                    pl.BlockSpec((n1h, N2_STEP, LANE), lambda b, j: (0, j, 0))]
        args = [xv, mod, g, w, rope_tabs[0].reshape(n1h, n2, LANE), rope_tabs[1].reshape(n1h, n2, LANE)]
        out_specs = [nat(2 * ATTN_W), nat(2 * LANE), nat(KV_W), nat(ATTN_W),
                     pl.BlockSpec((None, N2_STEP, n1h, CV_W), lambda b, j: (b, j, 0, 0))]
        out_shape = [nat_shape(2 * ATTN_W, BF16), nat_shape(2 * LANE, BF16), nat_shape(KV_W, BF16),
                     nat_shape(ATTN_W, BF16), jax.ShapeDtypeStruct((B, n2, n1h, CV_W), BF16)]
        scratch = [pltpu.VMEM((CV_W // LANE, n1h * TMP_PITCH, LANE), F32)]
        kern = functools.partial(_in_proj_kernel, rope=True, permute=True)
        sem = ("arbitrary", "arbitrary")
    else:
        bb = ROWS // L
        grid = (B // bb,)
        nat = lambda wd: pl.BlockSpec((bb, L, wd), lambda i: (i, 0, 0))
        nat_shape = lambda wd, dt: jax.ShapeDtypeStruct((B, L, wd), dt)
        in_specs = [nat(D_MODEL),
                    pl.BlockSpec((None, 1, 3 * D_MODEL), lambda i: (0, 0, 0)),
                    pl.BlockSpec((1, D_MODEL), const2),
                    pl.BlockSpec((D_MODEL, IN_COLS), const2)]
        args = [x, mod, g, w]
        out_specs = [nat(2 * ATTN_W), nat(2 * LANE), nat(KV_W), nat(ATTN_W), nat(CV_W), nat(KV_W), nat(KV_W)]
        out_shape = [nat_shape(2 * ATTN_W, BF16), nat_shape(2 * LANE, BF16), nat_shape(KV_W, BF16),
                     nat_shape(ATTN_W, BF16), nat_shape(CV_W, BF16), nat_shape(KV_W, F32), nat_shape(KV_W, F32)]
        scratch = []
        kern = functools.partial(_in_proj_kernel, rope=False, permute=False)
        sem = ("arbitrary",)
    outs = pl.pallas_call(
        kern, grid=grid, in_specs=in_specs, out_specs=out_specs, out_shape=out_shape,
        scratch_shapes=scratch, compiler_params=_params(*sem),
        name="in_proj_lat" if grp.two_stage else "in_proj_ctx",
    )(*args)
    outs = list(outs)
    if grp.two_stage:
        outs[:4] = [o.reshape(B, L, o.shape[-1]) for o in outs[:4]]
    else:
        outs[4] = outs[4].reshape(B, 1, L, CV_W)
    return outs


def _window_bias():
    qi = np.arange(BLOCK)[:, None]
    kj = np.arange(3 * BLOCK)[None, :]
    return np.where(np.abs(kj - BLOCK - qi) <= WINDOW, 0.0, NEG_INF).astype(np.float32)


def _attn_kernel(*refs, lq, local, nblocks, qblocks):
    if local:
        sink_ref, bias_ref, qq_ref, kk_ref, vv_ref, ck_ref, cv_ref, ga_ref, o_ref = refs
    else:
        sink_ref, qq_ref, kk_ref, vv_ref, ga_ref, o_ref = refs
    group = N_HEADS // N_KV_HEADS
    rows = group * lq
    nt = (((1,), (1,)), ((), ()))
    lt64 = _lane_lt64((lq, LANE))

    for qb in range(qblocks):
        qrows = slice(qb * lq, (qb + 1) * lq)
        if local:
            n = pl.program_id(1) * qblocks + qb
            starts = [pl.multiple_of(jnp.maximum(n - 1, 0) * BLOCK, BLOCK),
                      pl.multiple_of(n * BLOCK, BLOCK),
                      pl.multiple_of(jnp.minimum(n + 1, nblocks - 1) * BLOCK, BLOCK)]
            kj = lax.broadcasted_iota(jnp.int32, (1, 3 * BLOCK), 1)
            edge = (jnp.where((kj < BLOCK) & (n == 0), NEG_INF, 0.0)
                    + jnp.where((kj >= 2 * BLOCK) & (n == nblocks - 1), NEG_INF, 0.0))
            bias = bias_ref[...] + edge
        for g in range(N_KV_HEADS):
            lhs = jnp.concatenate(
                [qq_ref[qrows, (group * g + i) * LANE:(group * g + i + 1) * LANE] for i in range(group)], axis=0)
            gl = slice(g * LANE, (g + 1) * LANE)
            if local:
                kwin = jnp.concatenate([kk_ref[pl.ds(s, BLOCK), gl] for s in starts], axis=0)
                vals = jnp.concatenate([vv_ref[pl.ds(s, BLOCK), :] for s in starts] + [cv_ref[...]], axis=0)
                s_loc = lax.dot_general(lhs, kwin, nt, preferred_element_type=F32)
                s_ctx = lax.dot_general(lhs, ck_ref[:, gl], nt, preferred_element_type=F32)
            else:
                vals = vv_ref[...]
                s_loc = lax.dot_general(lhs, kk_ref[:, gl], nt, preferred_element_type=F32)
            row_head = lax.broadcasted_iota(jnp.int32, (rows, 1), 0) // lq
            snk = jnp.zeros((rows, 1), F32)
            for i in range(group):
                snk = jnp.where(row_head == i, sink_ref[group * g + i], snk)
            if local:
                s_loc = s_loc + jnp.concatenate([bias] * group, axis=0)
            m = jnp.maximum(jnp.max(s_loc, axis=-1, keepdims=True), snk)
            if local:
                m = jnp.maximum(m, jnp.max(s_ctx, axis=-1, keepdims=True))
            p_loc = jnp.exp(s_loc - m)
            den = jnp.sum(p_loc, axis=-1, keepdims=True) + jnp.exp(snk - m)
            o = _mm(p_loc.astype(BF16), vals[:s_loc.shape[1]])
            if local:
                p_ctx = jnp.exp(s_ctx - m)
                den = den + jnp.sum(p_ctx, axis=-1, keepdims=True)
                o = o + _mm(p_ctx.astype(BF16), vals[s_loc.shape[1]:])
            o = o / den
            for jj in range(group // 2):
                a = o[(2 * jj) * lq:(2 * jj + 1) * lq]
                b = o[(2 * jj + 1) * lq:(2 * jj + 2) * lq]
                if g == 0:
                    tile = jnp.where(lt64, a, pltpu.roll(b, HEAD_DIM, axis=1))
                else:
                    tile = jnp.where(lt64, pltpu.roll(a, HEAD_DIM, axis=1), b)
                j = (group // 2) * g + jj
                cols = slice(j * LANE, (j + 1) * LANE)
                o_ref[qrows, cols] = (tile * ga_ref[qrows, cols].astype(F32)).astype(BF16)


def _attention(grp, sink, qq, kk, vv, ga, ctx=None):
    B, L = grp.B, grp.L
    smem = pl.BlockSpec(memory_space=pltpu.SMEM)
    if ctx is not None:
        ck, cv = ctx
        nb = L // BLOCK
        lc = ck.shape[1]
        qrows = ATTN_QBLOCKS * BLOCK
        grid = (B, nb // ATTN_QBLOCKS)
        blk = lambda wd: pl.BlockSpec((None, qrows, wd), lambda b, n: (b, n, 0))
        full = lambda rows, wd: pl.BlockSpec((None, rows, wd), lambda b, n: (b, 0, 0))
        in_specs = [smem, pl.BlockSpec((BLOCK, 3 * BLOCK), lambda b, n: (0, 0)),
                    blk(2 * ATTN_W), full(L, 2 * LANE), full(L, KV_W), full(lc, 2 * LANE), full(lc, KV_W),
                    blk(ATTN_W)]
        args = [sink, jnp.asarray(_window_bias()), qq, kk, vv, ck, cv, ga]
        out_spec = blk(ATTN_W)
        kern = functools.partial(_attn_kernel, lq=BLOCK, local=True, nblocks=nb, qblocks=ATTN_QBLOCKS)
        sem = ("arbitrary", "arbitrary")
    else:
        grid = (B,)
        full = lambda wd: pl.BlockSpec((None, L, wd), lambda b: (b, 0, 0))
        in_specs = [smem, full(2 * ATTN_W), full(2 * LANE), full(KV_W), full(ATTN_W)]
        args = [sink, qq, kk, vv, ga]
        out_spec = full(ATTN_W)
        kern = functools.partial(_attn_kernel, lq=L, local=False, nblocks=1, qblocks=1)
        sem = ("arbitrary",)
    return pl.pallas_call(
        kern, grid=grid, in_specs=in_specs, out_specs=out_spec,
        out_shape=jax.ShapeDtypeStruct((B, L, ATTN_W), BF16),
        compiler_params=_params(*sem), name="attn_lat" if ctx is not None else "attn_ctx",
    )(*args)


def _shift_rows(s, up):
    rows = s.shape[0]
    r = lax.broadcasted_iota(jnp.int32, s.shape, 0)
    if up:
        return jnp.where(r == rows - 1, 0.0, pltpu.roll(s, rows - 1, axis=0))
    return jnp.where(r == 0, 0.0, pltpu.roll(s, 1, axis=0))


def _dwconv3(u, w):
    n2 = u.shape[0]
    first_prev = _shift_rows(u[n2 - 1], up=False)[None]
    last_next = _shift_rows(u[0], up=True)[None]
    if n2 == 1:
        prev, nxt = first_prev, last_next
    else:
        prev = jnp.concatenate([first_prev, u[:-1]], axis=0)
        nxt = jnp.concatenate([u[1:], last_next], axis=0)
    return prev * w[0:1][None] + u * w[1:2][None] + nxt * w[2:3][None]


def _pre_kernel(hv_ref, hx1_ref, hx2_ref, gh_ref, sb_ref, sc_ref, sx_ref, gc_ref, w_ref,
                hvc_ref, hx1c_ref, hx2g_ref, scg_ref):
    f = lambda r, b: r[b].astype(F32)
    for b in range(hv_ref.shape[0]):
        hvc_ref[b] = _dwconv3(f(hv_ref, b), w_ref[:, 0, :]).astype(BF16)
        hx1c_ref[b] = _dwconv3(f(hx1_ref, b), w_ref[:, 1, :]).astype(BF16)
        hx2g_ref[b] = (_dwconv3(f(hx2_ref, b), w_ref[:, 2, :]) * f(gh_ref, b)).astype(BF16)
        inner = _dwconv3(f(sc_ref, b) * f(sx_ref, b), w_ref[:, 3, :])
        scg_ref[b] = (f(sb_ref, b) * inner * f(gc_ref, b)).astype(BF16)


def _pre(grp, cv, wconv):
    B, n2, n1h = grp.B, grp.N2, grp.N1h
    bb = 1 if grp.two_stage else 4
    ncb = HYENA_W // LANE
    col = lambda k: pl.BlockSpec((bb, n2, n1h, LANE), lambda b, c, k=k: (b, 0, 0, k * ncb + c))
    out_spec = pl.BlockSpec((bb, n2, n1h, LANE), lambda b, c: (b, 0, 0, c))
    shape = jax.ShapeDtypeStruct((B, n2, n1h, HYENA_W), BF16)
    return pl.pallas_call(
        _pre_kernel, grid=(B // bb, ncb),
        in_specs=[col(k) for k in range(8)] + [pl.BlockSpec((3, 4, LANE), lambda b, c: (0, 0, c))],
        out_specs=[out_spec] * 4, out_shape=[shape] * 4,
        compiler_params=_params("arbitrary", "arbitrary"),
        name="pre_lat" if grp.two_stage else "pre_ctx",
    )(*([cv] * 8), wconv)


def _filter_kernel(ff_ref, fb_ref, w1_ref, b1_ref, w2_ref, b2_ref, w3_ref, fq_ref, dl_ref,
                   kf_ref, kb_ref, *, chunk):
    rows = ff_ref.shape[0]
    width = 2 * HYENA_W
    fq = fq_ref[...]

    def mlp(feats, w3):
        h = jnp.sin(fq * (_mm_hi(feats, w1_ref[...]) + b1_ref[...]))
        h = jnp.sin(fq * (_mm_hi(h, w2_ref[...]) + b2_ref[...]))
        out = _mm_hi(h, w3)
        dec = jnp.exp(-(feats[:, 0:1] * dl_ref[...]))
        return out * jnp.concatenate([dec, dec], axis=1)

    def body(i, acc):
        r0 = pl.multiple_of(i * chunk, chunk)
        kf = mlp(ff_ref[pl.ds(r0, chunk), :], w3_ref[:, 0:width])
        kb = mlp(fb_ref[pl.ds(r0, chunk), :], w3_ref[:, width:2 * width])
        rid = lax.broadcasted_iota(jnp.int32, kb.shape, 0) + r0
        kb = jnp.where(rid == 0, 0.0, kb)
        kf_ref[pl.ds(r0, chunk), :] = kf
        kb_ref[pl.ds(r0, chunk), :] = kb
        return acc + jnp.sum(jnp.abs(kf) + jnp.abs(kb), axis=0, keepdims=True)

    total = lax.fori_loop(0, rows // chunk, body, jnp.zeros((1, width), F32))

    def scale(i, c):
        r0 = pl.multiple_of(i * chunk, chunk)
        kf_ref[pl.ds(r0, chunk), :] = kf_ref[pl.ds(r0, chunk), :] / total
        kb_ref[pl.ds(r0, chunk), :] = kb_ref[pl.ds(r0, chunk), :] / total
        return c

    lax.fori_loop(0, rows // chunk, scale, 0)


def _filters(grp, feats, w1, b1, w2, b2, w3, fq, deltas):
    L = grp.L
    chunk = min(512, L)
    c2 = lambda l: (0, 0)
    per = lambda a, b: pl.BlockSpec((None, a, b), lambda l: (l, 0, 0))
    shape = jax.ShapeDtypeStruct((DEPTH, L, 2 * HYENA_W), F32)
    return pl.pallas_call(
        functools.partial(_filter_kernel, chunk=chunk), grid=(DEPTH,),
        in_specs=[pl.BlockSpec((L, LANE), c2), pl.BlockSpec((L, LANE), c2),
                  per(LANE, FILTER_HIDDEN), per(1, FILTER_HIDDEN), per(FILTER_HIDDEN, FILTER_HIDDEN),
                  per(1, FILTER_HIDDEN), per(FILTER_HIDDEN, 4 * HYENA_W), per(1, FILTER_HIDDEN),
                  pl.BlockSpec((1, HYENA_W), c2)],
        out_specs=[per(L, 2 * HYENA_W)] * 2, out_shape=[shape] * 2,
        compiler_params=_params("arbitrary"),
        name="filters_lat" if grp.two_stage else "filters_ctx",
    )(feats[0], feats[1], w1, b1, w2, b2, w3, fq, deltas)


def _store_spectrum_rows(s_ref, base, a, n1):
    for comp in range(2):
        for sl in range(2):
            s_ref[comp, sl, pl.ds(base, n1), :] = a[comp * n1:(comp + 1) * n1, sl * LANE:(sl + 1) * LANE]


def _load_column(s_ref, k1, n2, pitch):
    parts = [jnp.concatenate([s_ref[comp, sl, pl.ds(k1, n2, stride=pitch), :] for sl in range(2)], axis=1)
             for comp in range(2)]
    return jnp.concatenate(parts, axis=0).astype(BF16)


def _spectrum2_kernel(kf_ref, kb_ref, faf_ref, g_ref, o_ref, s_ref, *, n1, n2):
    n1h = n1 // 2
    pitch = n1 + PITCH_PAD
    faf = faf_ref[...]

    def stage_a(i, c):
        for t in range(UNROLL):
            j = i * UNROLL + t
            r0 = pl.multiple_of(j * n1h, n1h)
            rhs = jnp.concatenate([kf_ref[pl.ds(r0, n1h), :], kb_ref[pl.ds(r0, n1h), :]], axis=0).astype(BF16)
            _store_spectrum_rows(s_ref, pl.multiple_of(j * pitch, 8), _mm(faf, rhs), n1)
        return c

    lax.fori_loop(0, n2 // UNROLL, stage_a, 0)

    def stage_c(i, c):
        for t in range(UNROLL):
            k1 = i * UNROLL + t
            o_ref[k1] = _mm(g_ref[k1], _load_column(s_ref, k1, n2, pitch)).astype(BF16)
        return c

    lax.fori_loop(0, n1 // UNROLL, stage_c, 0)


def _spectrum1_kernel(kf_ref, kb_ref, faf_ref, o_ref):
    rhs = jnp.concatenate([kf_ref[...], kb_ref[...]], axis=0).astype(BF16)
    o_ref[...] = _mm(faf_ref[...], rhs).astype(BF16)


def _spectrum(grp, kf, kb, faf, g):
    n1, n2, L = grp.N1, grp.N2, grp.L
    wd = 2 * LANE
    nblk = 2 * HYENA_W // wd
    kin = pl.BlockSpec((None, L, wd), lambda l, c: (l, 0, c))
    if grp.two_stage:
        pitch = n1 + PITCH_PAD
        return pl.pallas_call(
            functools.partial(_spectrum2_kernel, n1=n1, n2=n2), grid=(DEPTH, nblk),
            in_specs=[kin, kin, pl.BlockSpec((2 * n1, n1), lambda l, c: (0, 0)),
                      pl.BlockSpec((n1, 2 * n2, 2 * n2), lambda l, c: (0, 0, 0))],
            out_specs=pl.BlockSpec((None, n1, 2 * n2, wd), lambda l, c: (l, 0, 0, c)),
            out_shape=jax.ShapeDtypeStruct((DEPTH, n1, 2 * n2, 2 * HYENA_W), BF16),
            scratch_shapes=[pltpu.VMEM((2, 2, n2 * pitch, LANE), F32)],
            compiler_params=_params("arbitrary", "arbitrary"), name="spectrum_lat",
        )(kf, kb, faf, g)
    return pl.pallas_call(
        _spectrum1_kernel, grid=(DEPTH, nblk),
        in_specs=[kin, kin, pl.BlockSpec((2 * n1, n1), lambda l, c: (0, 0))],
        out_specs=pl.BlockSpec((None, 2 * n1, wd), lambda l, c: (l, 0, c)),
        out_shape=jax.ShapeDtypeStruct((DEPTH, 2 * n1, 2 * HYENA_W), BF16),
        compiler_params=_params("arbitrary", "arbitrary"), name="spectrum_ctx",
    )(kf, kb, faf)


def _pair_operand(u_ref, j):
    re = jnp.concatenate([u_ref[0, j], u_ref[2, j]], axis=1)
    im = jnp.concatenate([u_ref[1, j], u_ref[3, j]], axis=1)
    return jnp.concatenate([re, im], axis=0)


def _cmul(x, k, half):
    k = k.astype(F32)
    kr = jnp.concatenate([k[:half], k[:half]], axis=1)
    ki = jnp.concatenate([k[half:], k[half:]], axis=1)
    xr, xi = x[:half], x[half:]
    return jnp.concatenate([xr * kr - xi * ki, xr * ki + xi * kr], axis=0).astype(BF16)


def _conv_epilogue(u_ref, m_ref, o_ref, d, y, j, n1h):
    for b in range(4):
        yb = y[(b % 2) * n1h:(b % 2 + 1) * n1h, (b // 2) * LANE:(b // 2 + 1) * LANE]
        u = u_ref[b, j].astype(F32)
        o_ref[b, j] = (m_ref[b, j].astype(F32) * (yb + u * d)).astype(BF16)


def _conv2_kernel(u_ref, m_ref, fa_ref, fat_ref, g_ref, k_ref, d_ref, o_ref, s_ref, *, n1, n2):
    n1h = n1 // 2
    pitch = n1 + PITCH_PAD
    fa = fa_ref[...]
    fat = fat_ref[...]
    d = d_ref[...]

    def stage_a(i, c):
        for t in range(UNROLL):
            j = i * UNROLL + t
            _store_spectrum_rows(s_ref, pl.multiple_of(j * pitch, 8), _mm(fa, _pair_operand(u_ref, j)), n1)
        return c

    lax.fori_loop(0, n2 // UNROLL, stage_a, 0)

    def stage_c(i, c):
        ks = [i * UNROLL + t for t in range(UNROLL)]
        cols = [_load_column(s_ref, k1, n2, pitch) for k1 in ks]
        backs = []
        for k1, col in zip(ks, cols):
            g = g_ref[k1]
            y = _cmul(_mm(g, col), k_ref[k1], n2)
            backs.append(lax.dot_general(g, y, (((0,), (0,)), ((), ())), preferred_element_type=F32))
        for k1, back in zip(ks, backs):
            for comp in range(2):
                for sl in range(2):
                    s_ref[comp, sl, pl.ds(k1, n2, stride=pitch), :] = (
                        back[comp * n2:(comp + 1) * n2, sl * LANE:(sl + 1) * LANE])
        return c

    lax.fori_loop(0, n1 // UNROLL, stage_c, 0)

    def stage_inv(i, c):
        for t in range(UNROLL):
            j = i * UNROLL + t
            base = pl.multiple_of(j * pitch, 8)
            parts = [jnp.concatenate([s_ref[comp, sl, pl.ds(base, n1), :] for sl in range(2)], axis=1)
                     for comp in range(2)]
            y = _mm(fat, jnp.concatenate(parts, axis=0).astype(BF16))
            _conv_epilogue(u_ref, m_ref, o_ref, d, y, j, n1h)
        return c

    lax.fori_loop(0, n2 // UNROLL, stage_inv, 0)


def _conv1_kernel(u_ref, m_ref, fa_ref, fat_ref, k_ref, d_ref, o_ref, *, n1):
    x = _mm(fa_ref[...], _pair_operand(u_ref, 0))
    y = _mm(fat_ref[...], _cmul(x, k_ref[...], n1))
    _conv_epilogue(u_ref, m_ref, o_ref, d_ref[...], y, 0, n1 // 2)


def _long_conv(grp, u, m, spec, d, order, tabs):
    B, n1, n2, n1h = grp.B, grp.N1, grp.N2, grp.N1h
    ncb = HYENA_W // LANE
    grid = (ncb, B // 4)
    data = pl.BlockSpec((4, n2, n1h, LANE), lambda c, q: (q, 0, 0, c))
    dspec = pl.BlockSpec((1, LANE), lambda c, q: (0, order * ncb + c))
    fa = pl.BlockSpec((2 * n1, n1), lambda c, q: (0, 0))
    fat = pl.BlockSpec((n1, 2 * n1), lambda c, q: (0, 0))
    out_shape = jax.ShapeDtypeStruct((B, n2, n1h, HYENA_W), BF16)
    if grp.two_stage:
        pitch = n1 + PITCH_PAD
        return pl.pallas_call(
            functools.partial(_conv2_kernel, n1=n1, n2=n2), grid=grid,
            in_specs=[data, data, fa, fat,
                      pl.BlockSpec((n1, 2 * n2, 2 * n2), lambda c, q: (0, 0, 0)),
                      pl.BlockSpec((n1, 2 * n2, LANE), lambda c, q: (0, 0, order * ncb + c)),
                      dspec],
            out_specs=data, out_shape=out_shape,
            scratch_shapes=[pltpu.VMEM((2, 2, n2 * pitch, LANE), F32)],
            compiler_params=_params("arbitrary", "arbitrary"), name="long_conv_lat",
        )(u, m, tabs["fa"], tabs["fat"], tabs["g"], spec, d)
    return pl.pallas_call(
        functools.partial(_conv1_kernel, n1=n1), grid=grid,
        in_specs=[data, data, fa, fat,
                  pl.BlockSpec((2 * n1, LANE), lambda c, q: (0, order * ncb + c)), dspec],
        out_specs=data, out_shape=out_shape,
        compiler_params=_params("arbitrary", "arbitrary"), name="long_conv_ctx",
    )(u, m, tabs["fa"], tabs["fat"], spec, d)


def _out_proj_kernel(*refs, permute, final):
    if permute:
        x_ref, mod_ref, a_ref, z_ref, s_ref, w_ref, fg_ref, o_ref, tmp_ref = refs
    else:
        x_ref, mod_ref, a_ref, z_ref, s_ref, w_ref, fg_ref, o_ref = refs
    a = a_ref[...].reshape(ROWS, ATTN_W)
    nsl = HYENA_W // LANE
    if permute:
        n1h = ROWS // N2_STEP
        for s in range(N2_STEP):
            for sl in range(nsl):
                cols = slice(sl * LANE, (sl + 1) * LANE)
                tmp_ref[sl, pl.ds(s, n1h, stride=N2_STEP), :] = z_ref[s, :, cols].astype(F32)
                tmp_ref[nsl + sl, pl.ds(s, n1h, stride=N2_STEP), :] = s_ref[s, :, cols].astype(F32)
        conv = jnp.concatenate([tmp_ref[i] for i in range(2 * nsl)], axis=1).astype(BF16)
    else:
        conv = jnp.concatenate([z_ref[...].reshape(ROWS, HYENA_W), s_ref[...].reshape(ROWS, CONV_W)], axis=1)
    y = _mm(a, w_ref[0:ATTN_W, :]) + _mm(conv, w_ref[ATTN_W:, :])
    gate = mod_ref[:, 2 * D_MODEL:3 * D_MODEL]
    xn = x_ref[...].reshape(ROWS, D_MODEL) + gate * y
    if final:
        ms = jnp.mean(xn * xn, axis=-1, keepdims=True)
        xn = xn * lax.rsqrt(ms + RMS_EPS) * fg_ref[...]
    o_ref[...] = xn.reshape(o_ref.shape)


def _out_proj(grp, x, mod, attn, zg, scg, w, final_g, final):
    B, L = grp.B, grp.L
    const2 = lambda *_: (0, 0)
    if grp.two_stage:
        n1h, n2 = grp.N1h, grp.N2
        grid = (B, n2 // N2_STEP)
        nat = lambda wd: pl.BlockSpec((None, n1h, N2_STEP, wd), lambda b, j: (b, 0, j, 0))
        cvl = pl.BlockSpec((None, N2_STEP, n1h, HYENA_W), lambda b, j: (b, j, 0, 0))
        in_specs = [nat(D_MODEL), pl.BlockSpec((None, 1, 3 * D_MODEL), lambda b, j: (b, 0, 0)),
                    nat(ATTN_W), cvl, cvl, pl.BlockSpec((D_MODEL, D_MODEL), const2),
                    pl.BlockSpec((1, D_MODEL), const2)]
        args = [x.reshape(B, n1h, n2, D_MODEL), mod, attn.reshape(B, n1h, n2, ATTN_W), zg, scg, w, final_g]
        out_spec = nat(D_MODEL)
        out_shape = jax.ShapeDtypeStruct((B, n1h, n2, D_MODEL), F32)
        scratch = [pltpu.VMEM((4, ROWS, LANE), F32)]
        sem = ("arbitrary", "arbitrary")
    else:
        bb = ROWS // L
        grid = (B // bb,)
        nat = lambda wd: pl.BlockSpec((bb, L, wd), lambda i: (i, 0, 0))
        cvl = pl.BlockSpec((bb, None, L, HYENA_W), lambda i: (i, 0, 0, 0))
        in_specs = [nat(D_MODEL), pl.BlockSpec((None, 1, 3 * D_MODEL), lambda i: (0, 0, 0)),
                    nat(ATTN_W), cvl, cvl, pl.BlockSpec((D_MODEL, D_MODEL), const2),
                    pl.BlockSpec((1, D_MODEL), const2)]
        args = [x, mod, attn, zg, scg, w, final_g]
        out_spec = nat(D_MODEL)
        out_shape = jax.ShapeDtypeStruct((B, L, D_MODEL), F32)
        scratch = []
        sem = ("arbitrary",)
    out = pl.pallas_call(
        functools.partial(_out_proj_kernel, permute=grp.two_stage, final=final),
        grid=grid, in_specs=in_specs, out_specs=out_spec, out_shape=out_shape,
        scratch_shapes=scratch, compiler_params=_params(*sem),
        name="out_proj_lat" if grp.two_stage else "out_proj_ctx",
    )(*args)
    return out.reshape(B, L, D_MODEL)


def _group_tables(grp):
    fa, faf, g = _dft_tables(grp)
    tabs = {"fa": jnp.asarray(fa).astype(BF16), "fat": jnp.asarray(fa.T.copy()).astype(BF16),
            "faf": jnp.asarray(faf).astype(BF16)}
    if g is not None:
        tabs["g"] = jnp.asarray(g).astype(BF16)
    ff, fb = _filter_features(grp)
    tabs["feats"] = (jnp.asarray(ff), jnp.asarray(fb))
    return tabs


def kernel(x_prompt, x_sample, cache_k, cache_v, c, c_ctx, norm_g, mod_w, mod_b, w_in, attn_sink,
           hy_conv_w, hy_filt_w1, hy_filt_b1, hy_filt_w2, hy_filt_b2, hy_filt_w3, hy_filt_freq, hy_d,
           sc_conv_w, w_out, final_g):
    ctx = Group(x_prompt.shape[0], x_prompt.shape[1], 2 * x_prompt.shape[1], 1)
    lat = Group(x_sample.shape[0], x_sample.shape[1], 128, 2 * x_sample.shape[1] // 128)
    nlat = lat.B

    cond = jnp.zeros((16, D_MODEL), F32).at[0].set(c_ctx).at[1:1 + nlat].set(c)
    mods = _modulation(cond, mod_w, mod_b)

    w1p = jnp.pad(hy_filt_w1, ((0, 0), (0, LANE - FILTER_EMB), (0, 0)))
    b1 = hy_filt_b1[:, None, :]
    b2 = hy_filt_b2[:, None, :]
    fq = hy_filt_freq[:, None, :]
    deltas = jnp.asarray(_decay_rates())
    tabs, spec = {}, {}
    for grp in (ctx, lat):
        t = _group_tables(grp)
        kf, kb = _filters(grp, t["feats"], w1p, b1, hy_filt_w2, b2, hy_filt_w3, fq, deltas)
        spec[grp] = _spectrum(grp, kf, kb, t["faf"], t.get("g"))
        tabs[grp] = t
    rope = tuple(jnp.asarray(a) for a in _rope_tables(lat.L))

    w_in_b = w_in.astype(BF16)
    w_out_b = w_out.astype(BF16)
    wconv = jnp.concatenate([hy_conv_w.reshape(DEPTH, 3, 3, HYENA_W), sc_conv_w[:, :, None, :]], axis=2)
    dskip = hy_d.reshape(DEPTH, 1, 2 * HYENA_W)
    fg = final_g[None, :]
    lc = cache_k.shape[2]
    zpad = jnp.zeros((nlat, DEPTH, lc, HEAD_DIM), F32)
    ck_pad = jnp.concatenate([zpad, cache_k[:, :, :, 0], zpad, cache_k[:, :, :, 1]], axis=-1).astype(BF16)
    cv_nat = cache_v.reshape(nlat, DEPTH, lc, KV_W).astype(BF16)

    xp, xs = x_prompt, x_sample
    ks_new, vs_new = [], []
    for l in range(DEPTH):
        g = norm_g[l][None, :]
        last = l == DEPTH - 1
        for grp in (ctx, lat):
            is_lat = grp is lat
            x = xs if is_lat else xp
            mod = mods[l, 1:1 + nlat][:, None, :] if is_lat else mods[l, 0:1][:, None, :]
            outs = _in_proj(grp, x, mod, g, w_in_b[l], rope if is_lat else None)
            qq, kk, vv, ga, cv = outs[:5]
            if is_lat:
                attn = _attention(grp, attn_sink[l], qq, kk, vv, ga, ctx=(ck_pad[:, l], cv_nat[:, l]))
            else:
                ks_new.append(outs[5])
                vs_new.append(outs[6])
                attn = _attention(grp, attn_sink[l], qq, kk, vv, ga)
            hvc, hx1c, hx2g, scg = _pre(grp, cv, wconv[l])
            z1 = _long_conv(grp, hvc, hx1c, spec[grp][l], dskip[l], 0, tabs[grp])
            zg = _long_conv(grp, z1, hx2g, spec[grp][l], dskip[l], 1, tabs[grp])
            xn = _out_proj(grp, x, mod, attn, zg, scg, w_out_b[l], fg, last)
            if is_lat:
                xs = xn
            else:
                xp = xn
    shape = (ctx.B, DEPTH, ctx.L, N_KV_HEADS, HEAD_DIM)
    new_k = jnp.stack(ks_new, axis=1).reshape(shape)
    new_v = jnp.stack(vs_new, axis=1).reshape(shape)
    return (xp, xs, new_k, new_v)
```

```python
import functools
import math

import numpy as np
import jax
import jax.numpy as jnp
from jax import lax
from jax.experimental import pallas as pl
from jax.experimental.pallas import tpu as pltpu

F32 = jnp.float32
BF16 = jnp.bfloat16

D_MODEL = 1024
DEPTH = 4
GRID_W = 64
N_HEADS = 8
N_KV_HEADS = 2
HEAD_DIM = 64
ATTN_W = N_HEADS * HEAD_DIM
KV_W = N_KV_HEADS * HEAD_DIM
HYENA_W = 256
CONV_W = 256
WINDOW = 128
BLOCK = 128
FILTER_EMB = 33
FILTER_HIDDEN = 64
HYENA_TARGET = 1e-2
FAST_DECAY_PCT = 0.3
SLOW_DECAY_PCT = 1.5
ROPE_BASE = 10000.0
RMS_EPS = 1e-6
NEG_INF = -1e30
IN_COLS = 3328
C_Q, C_K, C_V, C_GA, C_CV = 0, 512, 640, 768, 1280
CV_W = 2048

LANE = 128
ROWS = 1024
N2_STEP = 16
TMP_PITCH = 24
VMEM_LIMIT = 56 * 1024 * 1024
PITCH_PAD = 8
UNROLL = 8
ATTN_QBLOCKS = 4
CTX_SEQS = 2


class Group:
    def __init__(self, batch, seq, n1, n2):
        self.B, self.L, self.N1, self.N2 = batch, seq, n1, n2
        self.N1h = n1 // 2
        assert self.N1h * n2 == seq
        self.two_stage = n2 > 1


def _mm(a, b):
    return jnp.dot(a, b, preferred_element_type=F32)


def _mm_hi(a, b):
    return jnp.dot(a, b, preferred_element_type=F32, precision=lax.Precision.HIGHEST)


def _silu(x):
    return x * (1.0 / (1.0 + jnp.exp(-x)))


def _params(*sem):
    return pltpu.CompilerParams(dimension_semantics=sem, vmem_limit_bytes=VMEM_LIMIT)


def _rope_tables(seq):
    t = np.arange(seq)
    n_freq = HEAD_DIM // 4
    inv = ROPE_BASE ** (-np.arange(n_freq, dtype=np.float64) / n_freq)
    row = (t // GRID_W)[:, None] * inv
    col = (t % GRID_W)[:, None] * inv
    cos = np.concatenate([np.cos(row), np.cos(row), np.cos(col), np.cos(col)], axis=1)
    sin = np.concatenate([-np.sin(row), np.sin(row), -np.sin(col), np.sin(col)], axis=1)
    return (np.tile(cos, (1, 2)).astype(np.float32), np.tile(sin, (1, 2)).astype(np.float32))


def _conv_order(grp, table):
    return table.reshape(grp.N1h, grp.N2, -1).transpose(1, 0, 2).reshape(grp.L, -1)


def _filter_features(grp):
    L = grp.L
    t = np.linspace(0.0, 1.0, L)[:, None]
    bands = (FILTER_EMB - 1) // 2
    ang = (2.0 * math.pi / L) * np.arange(L)[:, None]
    fr = np.linspace(1e-4, bands - 1, bands)[None, :]
    feats = np.concatenate([t, np.cos(fr * ang), -np.sin(fr * ang)], axis=-1)
    feats = np.pad(feats, ((0, 0), (0, LANE - FILTER_EMB)))
    fwd = _conv_order(grp, feats).astype(np.float32)
    lag = (L - np.arange(L)) % L
    bwd = _conv_order(grp, feats[lag]).astype(np.float32)
    return fwd[:, 0:1], bwd[:, 0:1], np.ascontiguousarray(fwd.T), np.ascontiguousarray(bwd.T)


def _decay_rates():
    max_decay = math.log(HYENA_TARGET) / FAST_DECAY_PCT
    min_decay = math.log(HYENA_TARGET) / SLOW_DECAY_PCT
    return np.abs(np.linspace(min_decay, max_decay, HYENA_W))[None, :].astype(np.float32)


def _dft_tables(grp):
    n1, n2 = grp.N1, grp.N2
    n = n1 * n2
    k = np.arange(n1)
    f = np.exp(-2j * np.pi * ((k[:, None] * k[None, :]) % n1) / n1)
    fh = f[:, : grp.N1h]
    fa = np.block([[fh.real, -fh.imag], [fh.imag, fh.real]])
    faf = np.concatenate([f.real, f.imag], axis=0) / n
    g = None
    if grp.two_stage:
        j = np.arange(n2)
        ph = (k[:, None, None] * j[None, None, :] + n1 * j[None, :, None] * j[None, None, :]) % n
        gc = np.exp(-2j * np.pi * ph / n)
        g = np.concatenate([np.concatenate([gc.real, -gc.imag], axis=2),
                            np.concatenate([gc.imag, gc.real], axis=2)], axis=1)
        g = g.astype(np.float32)
    return fa.astype(np.float32), faf.astype(np.float32), g


def _mod_kernel(c_ref, w_ref, b_ref, o_ref):
    o_ref[...] = _mm_hi(_silu(c_ref[...]), w_ref[...]) + b_ref[...]


def _modulation(cond, mod_w, mod_b):
    nb = 3 * D_MODEL // 1024
    return pl.pallas_call(
        _mod_kernel,
        grid=(DEPTH, nb),
        in_specs=[
            pl.BlockSpec((16, D_MODEL), lambda l, j: (0, 0)),
            pl.BlockSpec((None, D_MODEL, 1024), lambda l, j: (l, 0, j)),
            pl.BlockSpec((None, 1, 1024), lambda l, j: (l, 0, j)),
        ],
        out_specs=pl.BlockSpec((None, 16, 1024), lambda l, j: (l, 0, j)),
        out_shape=jax.ShapeDtypeStruct((DEPTH, 16, 3 * D_MODEL), F32),
        compiler_params=_params("arbitrary", "arbitrary"),
        name="modulation",
    )(cond, mod_w, mod_b.reshape(DEPTH, 1, 3 * D_MODEL))


def _lane_lt64(shape):
    return lax.broadcasted_iota(jnp.int32, shape, 1) < HEAD_DIM


def _rope(x, cos, sin):
    lane = lax.broadcasted_iota(jnp.int32, x.shape, 1)
    first = (lane % 32) < 16
    partner = jnp.where(first, pltpu.roll(x, LANE - 16, axis=1), pltpu.roll(x, 16, axis=1))
    return x * cos + partner * sin


def _in_proj_kernel(*refs, rope, permute):
    if rope:
        x_ref, mod_ref, g_ref, w_ref, cos_ref, sin_ref = refs[:6]
        outs = refs[6:]
    else:
        x_ref, mod_ref, g_ref, w_ref = refs[:4]
        outs = refs[6:]
    if permute:
        qq_ref, kk_ref, vv_ref, ga_ref, cv_ref, tmp_ref = outs
    else:
        qq_ref, kk_ref, vv_ref, ga_ref, cv_ref, kf_ref, vf_ref = outs

    x = x_ref[...].reshape(ROWS, D_MODEL)
    ms = jnp.mean(x * x, axis=-1, keepdims=True)
    y = x * lax.rsqrt(ms + RMS_EPS) * g_ref[...]
    shift = mod_ref[:, 0:D_MODEL]
    scale = mod_ref[:, D_MODEL:2 * D_MODEL]
    h = (y * (1.0 + scale) + shift).astype(BF16)

    if rope:
        cos = cos_ref[...].reshape(ROWS, LANE)
        sin = sin_ref[...].reshape(ROWS, LANE)
    lt64 = _lane_lt64((ROWS, LANE))

    def put(ref, lo, val):
        lead = ref.shape[:-1]
        width = val.shape[-1]
        ref[(slice(None),) * len(lead) + (slice(lo, lo + width),)] = val.reshape(lead + (width,))

    yq = _mm(h, w_ref[:, C_Q:C_Q + ATTN_W]) * (HEAD_DIM ** -0.5)
    for j in range(ATTN_W // LANE):
        xj = yq[:, j * LANE:(j + 1) * LANE]
        rj = _rope(xj, cos, sin) if rope else xj
        even = jnp.where(lt64, rj, pltpu.roll(xj, HEAD_DIM, axis=1))
        odd = jnp.where(lt64, pltpu.roll(rj, HEAD_DIM, axis=1), xj)
        put(qq_ref, (2 * j) * LANE, even.astype(BF16))
        put(qq_ref, (2 * j + 1) * LANE, odd.astype(BF16))

    yk = _mm(h, w_ref[:, C_K:C_K + KV_W])
    rk = _rope(yk, cos, sin) if rope else yk
    put(kk_ref, 0, jnp.where(lt64, rk, 0.0).astype(BF16))
    put(kk_ref, LANE, jnp.where(lt64, pltpu.roll(rk, HEAD_DIM, axis=1), 0.0).astype(BF16))
    yv = _mm(h, w_ref[:, C_V:C_V + KV_W])
    put(vv_ref, 0, yv.astype(BF16))
    if not permute:
        put(kf_ref, 0, yk)
        put(vf_ref, 0, yv)

    put(ga_ref, 0, _silu(_mm(h, w_ref[:, C_GA:C_GA + ATTN_W])).astype(BF16))

    chunk = 512
    for c in range(CV_W // chunk):
        yc = _mm(h, w_ref[:, C_CV + c * chunk:C_CV + (c + 1) * chunk])
        if c % 2 == 1:
            yc = jnp.concatenate([yc[:, :256], _silu(yc[:, 256:])], axis=1)
        if not permute:
            put(cv_ref, c * chunk, yc.astype(BF16))
            continue
        n1h = ROWS // N2_STEP
        nsl = chunk // LANE
        for sl in range(nsl):
            for i in range(n1h):
                tmp_ref[c * nsl + sl, i * TMP_PITCH:i * TMP_PITCH + N2_STEP, :] = (
                    yc[i * N2_STEP:(i + 1) * N2_STEP, sl * LANE:(sl + 1) * LANE])
        for s in range(N2_STEP):
            for sl in range(nsl):
                lo = c * chunk + sl * LANE
                cv_ref[s, :, lo:lo + LANE] = (
                    tmp_ref[c * nsl + sl, pl.ds(s, n1h, stride=TMP_PITCH), :].astype(BF16))


def _in_proj(grp, x, mod, g, w, layer, rope_tabs=None, caches=None):
    B, L = grp.B, grp.L
    const2 = lambda *_: (0, 0)
    wspec = pl.BlockSpec((None, D_MODEL, IN_COLS), lambda *_: (layer, 0, 0))
    aliases = {}
    if grp.two_stage:
        n1h, n2 = grp.N1h, grp.N2
        grid = (B, n2 // N2_STEP)
        xv = x.reshape(B, n1h, n2, D_MODEL)
        nat = lambda wd: pl.BlockSpec((None, n1h, N2_STEP, wd), lambda b, j: (b, 0, j, 0))
        nat_shape = lambda wd, dt: jax.ShapeDtypeStruct((B, n1h, n2, wd), dt)
        in_specs = [nat(D_MODEL),
                    pl.BlockSpec((None, 1, 3 * D_MODEL), lambda b, j: (b, 0, 0)),
                    pl.BlockSpec((1, D_MODEL), const2),
                    wspec,
                    pl.BlockSpec((n1h, N2_STEP, LANE), lambda b, j: (0, j, 0)),
                    pl.BlockSpec((n1h, N2_STEP, LANE), lambda b, j: (0, j, 0))]
        args = [xv, mod, g, w, rope_tabs[0].reshape(n1h, n2, LANE), rope_tabs[1].reshape(n1h, n2, LANE)]
        out_specs = [nat(2 * ATTN_W), nat(2 * LANE), nat(KV_W), nat(ATTN_W),
                     pl.BlockSpec((None, N2_STEP, n1h, CV_W), lambda b, j: (b, j, 0, 0))]
        out_shape = [nat_shape(2 * ATTN_W, BF16), nat_shape(2 * LANE, BF16), nat_shape(KV_W, BF16),
                     nat_shape(ATTN_W, BF16), jax.ShapeDtypeStruct((B, n2, n1h, CV_W), BF16)]
        scratch = [pltpu.VMEM((CV_W // LANE, n1h * TMP_PITCH, LANE), F32)]
        kern = functools.partial(_in_proj_kernel, rope=True, permute=True)
        sem = ("arbitrary", "arbitrary")
    else:
        bb = ROWS // L
        grid = (B // bb,)
        nat = lambda wd: pl.BlockSpec((bb, L, wd), lambda i: (i, 0, 0))
        nat_shape = lambda wd, dt: jax.ShapeDtypeStruct((B, L, wd), dt)
        in_specs = [nat(D_MODEL),
                    pl.BlockSpec((None, 1, 3 * D_MODEL), lambda i: (0, 0, 0)),
                    pl.BlockSpec((1, D_MODEL), const2),
                    wspec,
                    pl.BlockSpec(memory_space=pl.ANY),
                    pl.BlockSpec(memory_space=pl.ANY)]
        args = [x, mod, g, w, caches[0], caches[1]]
        aliases = {4: 5, 5: 6}
        cache_spec = pl.BlockSpec((bb, None, L, KV_W), lambda i: (i, layer, 0, 0))
        cache_shape = jax.ShapeDtypeStruct((B, DEPTH, L, KV_W), F32)
        out_specs = [nat(2 * ATTN_W), nat(2 * LANE), nat(KV_W), nat(ATTN_W), nat(CV_W), cache_spec, cache_spec]
        out_shape = [nat_shape(2 * ATTN_W, BF16), nat_shape(2 * LANE, BF16), nat_shape(KV_W, BF16),
                     nat_shape(ATTN_W, BF16), nat_shape(CV_W, BF16), cache_shape, cache_shape]
        scratch = []
        kern = functools.partial(_in_proj_kernel, rope=False, permute=False)
        sem = ("arbitrary",)
    outs = pl.pallas_call(
        kern, grid=grid, in_specs=in_specs, out_specs=out_specs, out_shape=out_shape,
        scratch_shapes=scratch, input_output_aliases=aliases, compiler_params=_params(*sem),
        name="in_proj_lat" if grp.two_stage else "in_proj_ctx",
    )(*args)
    outs = list(outs)
    if grp.two_stage:
        outs[:4] = [o.reshape(B, L, o.shape[-1]) for o in outs[:4]]
    else:
        outs[4] = outs[4].reshape(B, 1, L, CV_W)
    return outs


def _window_bias():
    qi = np.arange(BLOCK)[:, None]
    kj = np.arange(3 * BLOCK)[None, :]
    return np.where(np.abs(kj - BLOCK - qi) <= WINDOW, 0.0, NEG_INF).astype(np.float32)


def _attn_kernel(*refs, lq, local, nblocks, qblocks):
    if local:
        sink_ref, bias_ref, qq_ref, kk_ref, vv_ref, ck_ref, cv_ref, ga_ref, o_ref = refs
    else:
        sink_ref, qq_ref, kk_ref, vv_ref, ga_ref, o_ref = refs
    group = N_HEADS // N_KV_HEADS
    rows = group * lq
    nt = (((1,), (1,)), ((), ()))
    lt64 = _lane_lt64((lq, LANE))

    for qb in range(qblocks):
        qrows = slice(qb * lq, (qb + 1) * lq)
        if local:
            n = pl.program_id(1) * qblocks + qb
            starts = [pl.multiple_of(jnp.maximum(n - 1, 0) * BLOCK, BLOCK),
                      pl.multiple_of(n * BLOCK, BLOCK),
                      pl.multiple_of(jnp.minimum(n + 1, nblocks - 1) * BLOCK, BLOCK)]
            kj = lax.broadcasted_iota(jnp.int32, (1, 3 * BLOCK), 1)
            edge = (jnp.where((kj < BLOCK) & (n == 0), NEG_INF, 0.0)
                    + jnp.where((kj >= 2 * BLOCK) & (n == nblocks - 1), NEG_INF, 0.0))
            bias = bias_ref[...] + edge
        for g in range(N_KV_HEADS):
            lhs = jnp.concatenate(
                [qq_ref[qrows, (group * g + i) * LANE:(group * g + i + 1) * LANE] for i in range(group)], axis=0)
            gl = slice(g * LANE, (g + 1) * LANE)
            if local:
                kwin = jnp.concatenate([kk_ref[pl.ds(s, BLOCK), gl] for s in starts], axis=0)
                vals = jnp.concatenate([vv_ref[pl.ds(s, BLOCK), :] for s in starts] + [cv_ref[...]], axis=0)
                s_loc = lax.dot_general(lhs, kwin, nt, preferred_element_type=F32)
                s_ctx = lax.dot_general(lhs, ck_ref[:, gl], nt, preferred_element_type=F32)
            else:
                vals = vv_ref[qrows, :]
                s_loc = lax.dot_general(lhs, kk_ref[qrows, gl], nt, preferred_element_type=F32)
            row_head = lax.broadcasted_iota(jnp.int32, (rows, 1), 0) // lq
            snk = jnp.zeros((rows, 1), F32)
            for i in range(group):
                snk = jnp.where(row_head == i, sink_ref[group * g + i], snk)
            if local:
                s_loc = s_loc + jnp.concatenate([bias] * group, axis=0)
            m = jnp.maximum(jnp.max(s_loc, axis=-1, keepdims=True), snk)
            if local:
                m = jnp.maximum(m, jnp.max(s_ctx, axis=-1, keepdims=True))
            ones = jnp.ones((vals.shape[0], LANE), BF16)
            vext = jnp.concatenate([vals, ones], axis=1)
            nk = s_loc.shape[1]
            o = _mm(jnp.exp(s_loc - m).astype(BF16), vext[:nk])
            if local:
                o = o + _mm(jnp.exp(s_ctx - m).astype(BF16), vext[nk:])
            o = o[:, :LANE] / (o[:, LANE:] + jnp.exp(snk - m))
            for jj in range(group // 2):
                a = o[(2 * jj) * lq:(2 * jj + 1) * lq]
                b = o[(2 * jj + 1) * lq:(2 * jj + 2) * lq]
                if g == 0:
                    tile = jnp.where(lt64, a, pltpu.roll(b, HEAD_DIM, axis=1))
                else:
                    tile = jnp.where(lt64, pltpu.roll(a, HEAD_DIM, axis=1), b)
                j = (group // 2) * g + jj
                cols = slice(j * LANE, (j + 1) * LANE)
                o_ref[qrows, cols] = (tile * ga_ref[qrows, cols].astype(F32)).astype(BF16)


def _attention(grp, sink, qq, kk, vv, ga, layer=0, ctx=None):
    B, L = grp.B, grp.L
    smem = pl.BlockSpec(memory_space=pltpu.SMEM)
    if ctx is not None:
        ck, cv = ctx
        nb = L // BLOCK
        lc = ck.shape[2]
        qrows = ATTN_QBLOCKS * BLOCK
        grid = (B, nb // ATTN_QBLOCKS)
        blk = lambda wd: pl.BlockSpec((None, qrows, wd), lambda b, n: (b, n, 0))
        full = lambda rows, wd: pl.BlockSpec((None, rows, wd), lambda b, n: (b, 0, 0))
        cache = lambda wd: pl.BlockSpec((None, None, lc, wd), lambda b, n: (b, layer, 0, 0))
        in_specs = [smem, pl.BlockSpec((BLOCK, 3 * BLOCK), lambda b, n: (0, 0)),
                    blk(2 * ATTN_W), full(L, 2 * LANE), full(L, KV_W), cache(2 * LANE), cache(KV_W),
                    blk(ATTN_W)]
        args = [sink, jnp.asarray(_window_bias()), qq, kk, vv, ck, cv, ga]
        out_spec = blk(ATTN_W)
        out_shape = jax.ShapeDtypeStruct((B, L, ATTN_W), BF16)
        kern = functools.partial(_attn_kernel, lq=BLOCK, local=True, nblocks=nb, qblocks=ATTN_QBLOCKS)
        sem = ("arbitrary", "arbitrary")
    else:
        grid = (B // CTX_SEQS,)
        flat = lambda a: a.reshape(B * L, a.shape[-1])
        blk = lambda wd: pl.BlockSpec((CTX_SEQS * L, wd), lambda i: (i, 0))
        in_specs = [smem, blk(2 * ATTN_W), blk(2 * LANE), blk(KV_W), blk(ATTN_W)]
        args = [sink, flat(qq), flat(kk), flat(vv), flat(ga)]
        out_spec = blk(ATTN_W)
        out_shape = jax.ShapeDtypeStruct((B * L, ATTN_W), BF16)
        kern = functools.partial(_attn_kernel, lq=L, local=False, nblocks=1, qblocks=CTX_SEQS)
        sem = ("arbitrary",)
    out = pl.pallas_call(
        kern, grid=grid, in_specs=in_specs, out_specs=out_spec, out_shape=out_shape,
        compiler_params=_params(*sem), name="attn_lat" if ctx is not None else "attn_ctx",
    )(*args)
    return out.reshape(B, L, ATTN_W)


def _shift_rows(s, up):
    rows = s.shape[0]
    r = lax.broadcasted_iota(jnp.int32, s.shape, 0)
    if up:
        return jnp.where(r == rows - 1, 0.0, pltpu.roll(s, rows - 1, axis=0))
    return jnp.where(r == 0, 0.0, pltpu.roll(s, 1, axis=0))


def _dwconv3(u, w):
    n2 = u.shape[0]
    first_prev = _shift_rows(u[n2 - 1], up=False)[None]
    last_next = _shift_rows(u[0], up=True)[None]
    if n2 == 1:
        prev, nxt = first_prev, last_next
    else:
        prev = jnp.concatenate([first_prev, u[:-1]], axis=0)
        nxt = jnp.concatenate([u[1:], last_next], axis=0)
    return prev * w[0:1][None] + u * w[1:2][None] + nxt * w[2:3][None]


def _pre_kernel(hv_ref, hx1_ref, hx2_ref, gh_ref, sb_ref, sc_ref, sx_ref, gc_ref, w_ref,
                hvc_ref, hx1c_ref, hx2g_ref, scg_ref):
    f = lambda r, b: r[b].astype(F32)
    for b in range(hv_ref.shape[0]):
        hvc_ref[b] = _dwconv3(f(hv_ref, b), w_ref[:, 0, :]).astype(BF16)
        hx1c_ref[b] = _dwconv3(f(hx1_ref, b), w_ref[:, 1, :]).astype(BF16)
        hx2g_ref[b] = (_dwconv3(f(hx2_ref, b), w_ref[:, 2, :]) * f(gh_ref, b)).astype(BF16)
        inner = _dwconv3(f(sc_ref, b) * f(sx_ref, b), w_ref[:, 3, :])
        scg_ref[b] = (f(sb_ref, b) * inner * f(gc_ref, b)).astype(BF16)


def _pre(grp, cv, wconv):
    B, n2, n1h = grp.B, grp.N2, grp.N1h
    bb = 1 if grp.two_stage else 4
    ncb = HYENA_W // LANE
    col = lambda k: pl.BlockSpec((bb, n2, n1h, LANE), lambda b, c, k=k: (b, 0, 0, k * ncb + c))
    out_spec = pl.BlockSpec((bb, n2, n1h, LANE), lambda b, c: (b, 0, 0, c))
    shape = jax.ShapeDtypeStruct((B, n2, n1h, HYENA_W), BF16)
    return pl.pallas_call(
        _pre_kernel, grid=(B // bb, ncb),
        in_specs=[col(k) for k in range(8)] + [pl.BlockSpec((3, 4, LANE), lambda b, c: (0, 0, c))],
        out_specs=[out_spec] * 4, out_shape=[shape] * 4,
        compiler_params=_params("arbitrary", "arbitrary"),
        name="pre_lat" if grp.two_stage else "pre_ctx",
    )(*([cv] * 8), wconv)


def _filter_kernel(tf_ref, tb_ref, ff_ref, fb_ref, w1_ref, b1_ref, w2_ref, b2_ref, w3_ref, fq_ref, dl_ref,
                   kf_ref, kb_ref, *, chunk):
    rows = tf_ref.shape[0]
    width = 2 * HYENA_W
    fq = fq_ref[...]

    def mlp(time, feats_t, w3):
        h = jnp.sin(fq * (_mm_hi(w1_ref[...], feats_t) + b1_ref[...]))
        h = jnp.sin(fq * (_mm_hi(w2_ref[...], h) + b2_ref[...]))
        out = lax.dot_general(h, w3, (((0,), (0,)), ((), ())), preferred_element_type=F32,
                              precision=lax.Precision.HIGHEST)
        dec = jnp.exp(-(time * dl_ref[...]))
        return out * jnp.concatenate([dec, dec], axis=1)

    def body(i, acc):
        r0 = pl.multiple_of(i * chunk, chunk)
        kf = mlp(tf_ref[pl.ds(r0, chunk), :], ff_ref[:, pl.ds(r0, chunk)], w3_ref[:, 0:width])
        kb = mlp(tb_ref[pl.ds(r0, chunk), :], fb_ref[:, pl.ds(r0, chunk)], w3_ref[:, width:2 * width])
        rid = lax.broadcasted_iota(jnp.int32, kb.shape, 0) + r0
        kb = jnp.where(rid == 0, 0.0, kb)
        kf_ref[pl.ds(r0, chunk), :] = kf
        kb_ref[pl.ds(r0, chunk), :] = kb
        return acc + jnp.sum(jnp.abs(kf) + jnp.abs(kb), axis=0, keepdims=True)

    total = lax.fori_loop(0, rows // chunk, body, jnp.zeros((1, width), F32))

    def scale(i, c):
        r0 = pl.multiple_of(i * chunk, chunk)
        kf_ref[pl.ds(r0, chunk), :] = kf_ref[pl.ds(r0, chunk), :] / total
        kb_ref[pl.ds(r0, chunk), :] = kb_ref[pl.ds(r0, chunk), :] / total
        return c

    lax.fori_loop(0, rows // chunk, scale, 0)


def _filters(grp, feats, w1, b1, w2, b2, w3, fq, deltas):
    L = grp.L
    chunk = min(512, L)
    c2 = lambda l: (0, 0)
    per = lambda a, b: pl.BlockSpec((None, a, b), lambda l: (l, 0, 0))
    shape = jax.ShapeDtypeStruct((DEPTH, L, 2 * HYENA_W), F32)
    return pl.pallas_call(
        functools.partial(_filter_kernel, chunk=chunk), grid=(DEPTH,),
        in_specs=[pl.BlockSpec((L, 1), c2), pl.BlockSpec((L, 1), c2),
                  pl.BlockSpec((LANE, L), c2), pl.BlockSpec((LANE, L), c2),
                  per(FILTER_HIDDEN, LANE), per(FILTER_HIDDEN, 1), per(FILTER_HIDDEN, FILTER_HIDDEN),
                  per(FILTER_HIDDEN, 1), per(FILTER_HIDDEN, 4 * HYENA_W), per(FILTER_HIDDEN, 1),
                  pl.BlockSpec((1, HYENA_W), c2)],
        out_specs=[per(L, 2 * HYENA_W)] * 2, out_shape=[shape] * 2,
        compiler_params=_params("arbitrary"),
        name="filters_lat" if grp.two_stage else "filters_ctx",
    )(*feats, w1, b1, w2, b2, w3, fq, deltas)


def _store_spectrum_rows(s_ref, base, a, n1):
    for comp in range(2):
        for sl in range(2):
            s_ref[comp, sl, pl.ds(base, n1), :] = a[comp * n1:(comp + 1) * n1, sl * LANE:(sl + 1) * LANE]


def _load_column(s_ref, k1, n2, pitch):
    parts = [jnp.concatenate([s_ref[comp, sl, pl.ds(k1, n2, stride=pitch), :] for sl in range(2)], axis=1)
             for comp in range(2)]
    return jnp.concatenate(parts, axis=0).astype(BF16)


def _spectrum2_kernel(kf_ref, kb_ref, faf_ref, g_ref, o_ref, s_ref, *, n1, n2):
    n1h = n1 // 2
    pitch = n1 + PITCH_PAD
    faf = faf_ref[...]

    def stage_a(i, c):
        for t in range(UNROLL):
            j = i * UNROLL + t
            r0 = pl.multiple_of(j * n1h, n1h)
            rhs = jnp.concatenate([kf_ref[pl.ds(r0, n1h), :], kb_ref[pl.ds(r0, n1h), :]], axis=0).astype(BF16)
            _store_spectrum_rows(s_ref, pl.multiple_of(j * pitch, 8), _mm(faf, rhs), n1)
        return c

    lax.fori_loop(0, n2 // UNROLL, stage_a, 0)

    def stage_c(i, c):
        for t in range(UNROLL):
            k1 = i * UNROLL + t
            o_ref[k1] = _mm(g_ref[k1], _load_column(s_ref, k1, n2, pitch)).astype(BF16)
        return c

    lax.fori_loop(0, n1 // UNROLL, stage_c, 0)


def _spectrum1_kernel(kf_ref, kb_ref, faf_ref, o_ref):
    rhs = jnp.concatenate([kf_ref[...], kb_ref[...]], axis=0).astype(BF16)
    o_ref[...] = _mm(faf_ref[...], rhs).astype(BF16)


def _spectrum(grp, kf, kb, faf, g):
    n1, n2, L = grp.N1, grp.N2, grp.L
    wd = 2 * LANE
    nblk = 2 * HYENA_W // wd
    kin = pl.BlockSpec((None, L, wd), lambda l, c: (l, 0, c))
    if grp.two_stage:
        pitch = n1 + PITCH_PAD
        return pl.pallas_call(
            functools.partial(_spectrum2_kernel, n1=n1, n2=n2), grid=(DEPTH, nblk),
            in_specs=[kin, kin, pl.BlockSpec((2 * n1, n1), lambda l, c: (0, 0)),
                      pl.BlockSpec((n1, 2 * n2, 2 * n2), lambda l, c: (0, 0, 0))],
            out_specs=pl.BlockSpec((None, n1, 2 * n2, wd), lambda l, c: (l, 0, 0, c)),
            out_shape=jax.ShapeDtypeStruct((DEPTH, n1, 2 * n2, 2 * HYENA_W), BF16),
            scratch_shapes=[pltpu.VMEM((2, 2, n2 * pitch, LANE), F32)],
            compiler_params=_params("arbitrary", "arbitrary"), name="spectrum_lat",
        )(kf, kb, faf, g)
    return pl.pallas_call(
        _spectrum1_kernel, grid=(DEPTH, nblk),
        in_specs=[kin, kin, pl.BlockSpec((2 * n1, n1), lambda l, c: (0, 0))],
        out_specs=pl.BlockSpec((None, 2 * n1, wd), lambda l, c: (l, 0, c)),
        out_shape=jax.ShapeDtypeStruct((DEPTH, 2 * n1, 2 * HYENA_W), BF16),
        compiler_params=_params("arbitrary", "arbitrary"), name="spectrum_ctx",
    )(kf, kb, faf)


def _pair_operand(u_ref, j):
    nb = u_ref.shape[0]
    re = jnp.concatenate([u_ref[b, j] for b in range(0, nb, 2)], axis=1)
    im = jnp.concatenate([u_ref[b, j] for b in range(1, nb, 2)], axis=1)
    return jnp.concatenate([re, im], axis=0)


def _cmul(x, k, half):
    k = k.astype(F32)
    npair = x.shape[1] // LANE
    kr = jnp.concatenate([k[:half]] * npair, axis=1)
    ki = jnp.concatenate([k[half:]] * npair, axis=1)
    xr, xi = x[:half], x[half:]
    return jnp.concatenate([xr * kr - xi * ki, xr * ki + xi * kr], axis=0).astype(BF16)


def _conv_epilogue(u_ref, m_ref, o_ref, d, y, j, n1h):
    for b in range(u_ref.shape[0]):
        yb = y[(b % 2) * n1h:(b % 2 + 1) * n1h, (b // 2) * LANE:(b // 2 + 1) * LANE]
        u = u_ref[b, j].astype(F32)
        o_ref[b, j] = (m_ref[b, j].astype(F32) * (yb + u * d)).astype(BF16)


def _conv2_kernel(u_ref, m_ref, fa_ref, fat_ref, g_ref, k_ref, d_ref, o_ref, s_ref, *, n1, n2):
    n1h = n1 // 2
    pitch = n1 + PITCH_PAD
    fa = fa_ref[...]
    fat = fat_ref[...]
    d = d_ref[...]

    def stage_a(i, c):
        for t in range(UNROLL):
            j = i * UNROLL + t
            _store_spectrum_rows(s_ref, pl.multiple_of(j * pitch, 8), _mm(fa, _pair_operand(u_ref, j)), n1)
        return c

    lax.fori_loop(0, n2 // UNROLL, stage_a, 0)

    def stage_c(i, c):
        ks = [i * UNROLL + t for t in range(UNROLL)]
        cols = [_load_column(s_ref, k1, n2, pitch) for k1 in ks]
        backs = []
        for k1, col in zip(ks, cols):
            g = g_ref[k1]
            y = _cmul(_mm(g, col), k_ref[k1], n2)
            backs.append(lax.dot_general(g, y, (((0,), (0,)), ((), ())), preferred_element_type=F32))
        for k1, back in zip(ks, backs):
            for comp in range(2):
                for sl in range(2):
                    s_ref[comp, sl, pl.ds(k1, n2, stride=pitch), :] = (
                        back[comp * n2:(comp + 1) * n2, sl * LANE:(sl + 1) * LANE])
        return c

    lax.fori_loop(0, n1 // UNROLL, stage_c, 0)

    def stage_inv(i, c):
        for t in range(UNROLL):
            j = i * UNROLL + t
            base = pl.multiple_of(j * pitch, 8)
            parts = [jnp.concatenate([s_ref[comp, sl, pl.ds(base, n1), :] for sl in range(2)], axis=1)
                     for comp in range(2)]
            y = _mm(fat, jnp.concatenate(parts, axis=0).astype(BF16))
            _conv_epilogue(u_ref, m_ref, o_ref, d, y, j, n1h)
        return c

    lax.fori_loop(0, n2 // UNROLL, stage_inv, 0)


def _conv1_kernel(u_ref, m_ref, fa_ref, fat_ref, k_ref, d_ref, o_ref, *, n1):
    x = _mm(fa_ref[...], _pair_operand(u_ref, 0))
    y = _mm(fat_ref[...], _cmul(x, k_ref[...], n1))
    _conv_epilogue(u_ref, m_ref, o_ref, d_ref[...], y, 0, n1 // 2)


def _long_conv(grp, u, m, spec, d, layer, order, tabs):
    B, n1, n2, n1h = grp.B, grp.N1, grp.N2, grp.N1h
    ncb = HYENA_W // LANE
    nbatch = 4 if grp.two_stage else 8
    grid = (ncb, B // nbatch)
    data = pl.BlockSpec((nbatch, n2, n1h, LANE), lambda c, q: (q, 0, 0, c))
    dspec = pl.BlockSpec((None, 1, LANE), lambda c, q: (layer, 0, order * ncb + c))
    fa = pl.BlockSpec((2 * n1, n1), lambda c, q: (0, 0))
    fat = pl.BlockSpec((n1, 2 * n1), lambda c, q: (0, 0))
    out_shape = jax.ShapeDtypeStruct((B, n2, n1h, HYENA_W), BF16)
    if grp.two_stage:
        pitch = n1 + PITCH_PAD
        return pl.pallas_call(
            functools.partial(_conv2_kernel, n1=n1, n2=n2), grid=grid,
            in_specs=[data, data, fa, fat,
                      pl.BlockSpec((n1, 2 * n2, 2 * n2), lambda c, q: (0, 0, 0)),
                      pl.BlockSpec((None, n1, 2 * n2, LANE), lambda c, q: (layer, 0, 0, order * ncb + c)),
                      dspec],
            out_specs=data, out_shape=out_shape,
            scratch_shapes=[pltpu.VMEM((2, 2, n2 * pitch, LANE), F32)],
            compiler_params=_params("arbitrary", "arbitrary"), name="long_conv_lat",
        )(u, m, tabs["fa"], tabs["fat"], tabs["g"], spec, d)
    return pl.pallas_call(
        functools.partial(_conv1_kernel, n1=n1), grid=grid,
        in_specs=[data, data, fa, fat,
                  pl.BlockSpec((None, 2 * n1, LANE), lambda c, q: (layer, 0, order * ncb + c)), dspec],
        out_specs=data, out_shape=out_shape,
        compiler_params=_params("arbitrary", "arbitrary"), name="long_conv_ctx",
    )(u, m, tabs["fa"], tabs["fat"], spec, d)


def _out_proj_kernel(*refs, permute, final):
    if permute:
        x_ref, mod_ref, a_ref, z_ref, s_ref, w_ref, fg_ref, o_ref, tmp_ref = refs
    else:
        x_ref, mod_ref, a_ref, z_ref, s_ref, w_ref, fg_ref, o_ref = refs
    a = a_ref[...].reshape(ROWS, ATTN_W)
    nsl = HYENA_W // LANE
    if permute:
        n1h = ROWS // N2_STEP
        for s in range(N2_STEP):
            for sl in range(nsl):
                cols = slice(sl * LANE, (sl + 1) * LANE)
                tmp_ref[sl, pl.ds(s, n1h, stride=N2_STEP), :] = z_ref[s, :, cols].astype(F32)
                tmp_ref[nsl + sl, pl.ds(s, n1h, stride=N2_STEP), :] = s_ref[s, :, cols].astype(F32)
        conv = jnp.concatenate([tmp_ref[i] for i in range(2 * nsl)], axis=1).astype(BF16)
    else:
        conv = jnp.concatenate([z_ref[...].reshape(ROWS, HYENA_W), s_ref[...].reshape(ROWS, CONV_W)], axis=1)
    y = _mm(a, w_ref[0:ATTN_W, :]) + _mm(conv, w_ref[ATTN_W:, :])
    gate = mod_ref[:, 2 * D_MODEL:3 * D_MODEL]
    xn = x_ref[...].reshape(ROWS, D_MODEL) + gate * y
    if final:
        ms = jnp.mean(xn * xn, axis=-1, keepdims=True)
        xn = xn * lax.rsqrt(ms + RMS_EPS) * fg_ref[...]
    o_ref[...] = xn.reshape(o_ref.shape)


def _out_proj(grp, x, mod, attn, zg, scg, w, layer, final_g, final):
    B, L = grp.B, grp.L
    const2 = lambda *_: (0, 0)
    if grp.two_stage:
        n1h, n2 = grp.N1h, grp.N2
        grid = (B, n2 // N2_STEP)
        nat = lambda wd: pl.BlockSpec((None, n1h, N2_STEP, wd), lambda b, j: (b, 0, j, 0))
        cvl = pl.BlockSpec((None, N2_STEP, n1h, HYENA_W), lambda b, j: (b, j, 0, 0))
        in_specs = [nat(D_MODEL), pl.BlockSpec((None, 1, 3 * D_MODEL), lambda b, j: (b, 0, 0)),
                    nat(ATTN_W), cvl, cvl, pl.BlockSpec((None, D_MODEL, D_MODEL), lambda *_: (layer, 0, 0)),
                    pl.BlockSpec((1, D_MODEL), const2)]
        args = [x.reshape(B, n1h, n2, D_MODEL), mod, attn.reshape(B, n1h, n2, ATTN_W), zg, scg, w, final_g]
        out_spec = nat(D_MODEL)
        out_shape = jax.ShapeDtypeStruct((B, n1h, n2, D_MODEL), F32)
        scratch = [pltpu.VMEM((4, ROWS, LANE), F32)]
        sem = ("arbitrary", "arbitrary")
    else:
        bb = ROWS // L
        grid = (B // bb,)
        nat = lambda wd: pl.BlockSpec((bb, L, wd), lambda i: (i, 0, 0))
        cvl = pl.BlockSpec((bb, None, L, HYENA_W), lambda i: (i, 0, 0, 0))
        in_specs = [nat(D_MODEL), pl.BlockSpec((None, 1, 3 * D_MODEL), lambda i: (0, 0, 0)),
                    nat(ATTN_W), cvl, cvl, pl.BlockSpec((None, D_MODEL, D_MODEL), lambda *_: (layer, 0, 0)),
                    pl.BlockSpec((1, D_MODEL), const2)]
        args = [x, mod, attn, zg, scg, w, final_g]
        out_spec = nat(D_MODEL)
        out_shape = jax.ShapeDtypeStruct((B, L, D_MODEL), F32)
        scratch = []
        sem = ("arbitrary",)
    out = pl.pallas_call(
        functools.partial(_out_proj_kernel, permute=grp.two_stage, final=final),
        grid=grid, in_specs=in_specs, out_specs=out_spec, out_shape=out_shape,
        scratch_shapes=scratch, compiler_params=_params(*sem),
        name="out_proj_lat" if grp.two_stage else "out_proj_ctx",
    )(*args)
    return out.reshape(B, L, D_MODEL)


def _group_tables(grp):
    fa, faf, g = _dft_tables(grp)
    tabs = {"fa": jnp.asarray(fa).astype(BF16), "fat": jnp.asarray(fa.T.copy()).astype(BF16),
            "faf": jnp.asarray(faf).astype(BF16)}
    if g is not None:
        tabs["g"] = jnp.asarray(g).astype(BF16)
    tabs["feats"] = tuple(jnp.asarray(a) for a in _filter_features(grp))
    return tabs


def kernel(x_prompt, x_sample, cache_k, cache_v, c, c_ctx, norm_g, mod_w, mod_b, w_in, attn_sink,
           hy_conv_w, hy_filt_w1, hy_filt_b1, hy_filt_w2, hy_filt_b2, hy_filt_w3, hy_filt_freq, hy_d,
           sc_conv_w, w_out, final_g):
    ctx = Group(x_prompt.shape[0], x_prompt.shape[1], 2 * x_prompt.shape[1], 1)
    lat = Group(x_sample.shape[0], x_sample.shape[1], 128, 2 * x_sample.shape[1] // 128)
    nlat = lat.B

    cond = jnp.zeros((16, D_MODEL), F32).at[0].set(c_ctx).at[1:1 + nlat].set(c)
    mods = _modulation(cond, mod_w, mod_b)

    w1t = jnp.pad(hy_filt_w1, ((0, 0), (0, LANE - FILTER_EMB), (0, 0))).transpose(0, 2, 1)
    w2t = hy_filt_w2.transpose(0, 2, 1)
    b1 = hy_filt_b1[:, :, None]
    b2 = hy_filt_b2[:, :, None]
    fq = hy_filt_freq[:, :, None]
    deltas = jnp.asarray(_decay_rates())
    tabs, spec = {}, {}
    for grp in (ctx, lat):
        t = _group_tables(grp)
        kf, kb = _filters(grp, t["feats"], w1t, b1, w2t, b2, hy_filt_w3, fq, deltas)
        spec[grp] = _spectrum(grp, kf, kb, t["faf"], t.get("g"))
        tabs[grp] = t
    rope = tuple(jnp.asarray(a) for a in _rope_tables(lat.L))

    w_in_b = w_in.astype(BF16)
    w_out_b = w_out.astype(BF16)
    wconv = jnp.concatenate([hy_conv_w.reshape(DEPTH, 3, 3, HYENA_W), sc_conv_w[:, :, None, :]], axis=2)
    dskip = hy_d.reshape(DEPTH, 1, 2 * HYENA_W)
    fg = final_g[None, :]
    lc = cache_k.shape[2]
    zpad = jnp.zeros((nlat, DEPTH, lc, HEAD_DIM), F32)
    ck_pad = jnp.concatenate([zpad, cache_k[:, :, :, 0], zpad, cache_k[:, :, :, 1]], axis=-1).astype(BF16)
    cv_nat = cache_v.reshape(nlat, DEPTH, lc, KV_W).astype(BF16)

    xp, xs = x_prompt, x_sample
    caches = (jnp.zeros((ctx.B, DEPTH, ctx.L, KV_W), F32), jnp.zeros((ctx.B, DEPTH, ctx.L, KV_W), F32))
    for l in range(DEPTH):
        g = norm_g[l][None, :]
        last = l == DEPTH - 1
        for grp in (ctx, lat):
            is_lat = grp is lat
            x = xs if is_lat else xp
            mod = mods[l, 1:1 + nlat][:, None, :] if is_lat else mods[l, 0:1][:, None, :]
            if is_lat:
                qq, kk, vv, ga, cv = _in_proj(grp, x, mod, g, w_in_b, l, rope_tabs=rope)
                attn = _attention(grp, attn_sink[l], qq, kk, vv, ga, l, ctx=(ck_pad, cv_nat))
            else:
                qq, kk, vv, ga, cv, *caches = _in_proj(grp, x, mod, g, w_in_b, l, caches=caches)
                attn = _attention(grp, attn_sink[l], qq, kk, vv, ga)
            hvc, hx1c, hx2g, scg = _pre(grp, cv, wconv[l])
            z1 = _long_conv(grp, hvc, hx1c, spec[grp], dskip, l, 0, tabs[grp])
            zg = _long_conv(grp, z1, hx2g, spec[grp], dskip, l, 1, tabs[grp])
            xn = _out_proj(grp, x, mod, attn, zg, scg, w_out_b, l, fg, last)
            if is_lat:
                xs = xn
            else:
                xp = xn
    shape = (ctx.B, DEPTH, ctx.L, N_KV_HEADS, HEAD_DIM)
    return (xp, xs, caches[0].reshape(shape), caches[1].reshape(shape))
```

```python
import functools
import math

import numpy as np
import jax
import jax.numpy as jnp
from jax import lax
from jax.experimental import pallas as pl
from jax.experimental.pallas import tpu as pltpu

F32 = jnp.float32
BF16 = jnp.bfloat16

D_MODEL = 1024
DEPTH = 4
GRID_W = 64
N_HEADS = 8
N_KV_HEADS = 2
HEAD_DIM = 64
ATTN_W = N_HEADS * HEAD_DIM
KV_W = N_KV_HEADS * HEAD_DIM
HYENA_W = 256
CONV_W = 256
WINDOW = 128
BLOCK = 128
FILTER_EMB = 33
FILTER_HIDDEN = 64
HYENA_TARGET = 1e-2
FAST_DECAY_PCT = 0.3
SLOW_DECAY_PCT = 1.5
ROPE_BASE = 10000.0
RMS_EPS = 1e-6
NEG_INF = -1e30
IN_COLS = 3328
C_Q, C_K, C_V, C_GA, C_CV = 0, 512, 640, 768, 1280
CV_W = 2048

LANE = 128
ROWS = 1024
N1_STEP = 16
ROW_PITCH = 72
VMEM_LIMIT = 56 * 1024 * 1024
PITCH_PAD = 8
UNROLL = 8
UNROLL_WIDE = 16
ATTN_QBLOCKS = 4
CTX_SEQS = 2


class Group:
    def __init__(self, batch, seq, n1, n2):
        self.B, self.L, self.N1, self.N2 = batch, seq, n1, n2
        self.N1h = n1 // 2
        assert self.N1h * n2 == seq
        self.two_stage = n2 > 1


def _mm(a, b):
    return jnp.dot(a, b, preferred_element_type=F32)


def _mm_hi(a, b):
    return jnp.dot(a, b, preferred_element_type=F32, precision=lax.Precision.HIGHEST)


def _silu(x):
    return x * (1.0 / (1.0 + jnp.exp(-x)))


def _params(*sem):
    return pltpu.CompilerParams(dimension_semantics=sem, vmem_limit_bytes=VMEM_LIMIT)


def _rope_tables(seq):
    t = np.arange(seq)
    n_freq = HEAD_DIM // 4
    inv = ROPE_BASE ** (-np.arange(n_freq, dtype=np.float64) / n_freq)
    row = (t // GRID_W)[:, None] * inv
    col = (t % GRID_W)[:, None] * inv
    cos = np.concatenate([np.cos(row), np.cos(row), np.cos(col), np.cos(col)], axis=1)
    sin = np.concatenate([-np.sin(row), np.sin(row), -np.sin(col), np.sin(col)], axis=1)
    return (np.tile(cos, (1, 2)).astype(np.float32), np.tile(sin, (1, 2)).astype(np.float32))


def _conv_order(grp, table):
    return table.reshape(grp.N1h, grp.N2, -1).transpose(1, 0, 2).reshape(grp.L, -1)


def _filter_features(grp):
    L = grp.L
    t = np.linspace(0.0, 1.0, L)[:, None]
    bands = (FILTER_EMB - 1) // 2
    ang = (2.0 * math.pi / L) * np.arange(L)[:, None]
    fr = np.linspace(1e-4, bands - 1, bands)[None, :]
    feats = np.concatenate([t, np.cos(fr * ang), -np.sin(fr * ang)], axis=-1)
    feats = np.pad(feats, ((0, 0), (0, LANE - FILTER_EMB)))
    fwd = _conv_order(grp, feats).astype(np.float32)
    lag = (L - np.arange(L)) % L
    bwd = _conv_order(grp, feats[lag]).astype(np.float32)
    return fwd[:, 0:1], bwd[:, 0:1], np.ascontiguousarray(fwd.T), np.ascontiguousarray(bwd.T)


def _decay_rates():
    max_decay = math.log(HYENA_TARGET) / FAST_DECAY_PCT
    min_decay = math.log(HYENA_TARGET) / SLOW_DECAY_PCT
    return np.abs(np.linspace(min_decay, max_decay, HYENA_W))[None, :].astype(np.float32)


def _dft_tables(grp):
    n1, n2 = grp.N1, grp.N2
    n = n1 * n2
    k = np.arange(n1)
    f = np.exp(-2j * np.pi * ((k[:, None] * k[None, :]) % n1) / n1)
    fh = f[:, : grp.N1h]
    fa = np.block([[fh.real, -fh.imag], [fh.imag, fh.real]])
    faf = np.concatenate([f.real, f.imag], axis=0) / n
    g = None
    if grp.two_stage:
        j = np.arange(n2)
        ph = (k[:, None, None] * j[None, None, :] + n1 * j[None, :, None] * j[None, None, :]) % n
        gc = np.exp(-2j * np.pi * ph / n)
        g = np.concatenate([np.concatenate([gc.real, -gc.imag], axis=2),
                            np.concatenate([gc.imag, gc.real], axis=2)], axis=1)
        g = g.astype(np.float32)
    return fa.astype(np.float32), faf.astype(np.float32), g


def _mod_kernel(c_ref, w_ref, b_ref, o_ref):
    o_ref[...] = _mm_hi(_silu(c_ref[...]), w_ref[...]) + b_ref[...]


def _modulation(cond, mod_w, mod_b):
    nb = 3 * D_MODEL // 1024
    return pl.pallas_call(
        _mod_kernel,
        grid=(DEPTH, nb),
        in_specs=[
            pl.BlockSpec((16, D_MODEL), lambda l, j: (0, 0)),
            pl.BlockSpec((None, D_MODEL, 1024), lambda l, j: (l, 0, j)),
            pl.BlockSpec((None, 1, 1024), lambda l, j: (l, 0, j)),
        ],
        out_specs=pl.BlockSpec((None, 16, 1024), lambda l, j: (l, 0, j)),
        out_shape=jax.ShapeDtypeStruct((DEPTH, 16, 3 * D_MODEL), F32),
        compiler_params=_params("arbitrary", "arbitrary"),
        name="modulation",
    )(cond, mod_w, mod_b.reshape(DEPTH, 1, 3 * D_MODEL))


def _lane_lt64(shape):
    return lax.broadcasted_iota(jnp.int32, shape, 1) < HEAD_DIM


def _rope(x, cos, sin):
    lane = lax.broadcasted_iota(jnp.int32, x.shape, 1)
    first = (lane % 32) < 16
    partner = jnp.where(first, pltpu.roll(x, LANE - 16, axis=1), pltpu.roll(x, 16, axis=1))
    return x * cos + partner * sin


def _in_proj_kernel(*refs, latent, seq):
    if latent:
        x_ref, xp_ref, xn_ref, mod_ref, g_ref, w_ref, wc_ref, cos_ref, sin_ref = refs[:9]
        qq_ref, kk_ref, vv_ref, ga_ref, cv_ref, tmp_ref = refs[9:]
        first_row = pl.program_id(1) * ROWS
    else:
        x_ref, mod_ref, g_ref, w_ref, wc_ref = refs[:5]
        qq_ref, kk_ref, vv_ref, ga_ref, cv_ref, kf_ref, vf_ref = refs[7:]
        first_row = 0
    rope = latent

    shift = mod_ref[:, 0:D_MODEL]
    scale = mod_ref[:, D_MODEL:2 * D_MODEL]

    def modulated_norm(x):
        ms = jnp.mean(x * x, axis=-1, keepdims=True)
        y = x * lax.rsqrt(ms + RMS_EPS) * g_ref[...]
        return (y * (1.0 + scale) + shift).astype(BF16)

    h = modulated_norm(x_ref[...].reshape(ROWS, D_MODEL))
    h_halo = modulated_norm(jnp.concatenate([xp_ref[...], xn_ref[...]], axis=0)) if latent else None

    if rope:
        cos = cos_ref[...]
        sin = sin_ref[...]
    lt64 = _lane_lt64((ROWS, LANE))

    def put(ref, lo, val):
        lead = ref.shape[:-1]
        width = val.shape[-1]
        ref[(slice(None),) * len(lead) + (slice(lo, lo + width),)] = val.reshape(lead + (width,))

    yq = _mm(h, w_ref[:, C_Q:C_Q + ATTN_W]) * (HEAD_DIM ** -0.5)
    for j in range(ATTN_W // LANE):
        xj = yq[:, j * LANE:(j + 1) * LANE]
        rj = _rope(xj, cos, sin) if rope else xj
        even = jnp.where(lt64, rj, pltpu.roll(xj, HEAD_DIM, axis=1))
        odd = jnp.where(lt64, pltpu.roll(rj, HEAD_DIM, axis=1), xj)
        put(qq_ref, (2 * j) * LANE, even.astype(BF16))
        put(qq_ref, (2 * j + 1) * LANE, odd.astype(BF16))

    yk = _mm(h, w_ref[:, C_K:C_K + KV_W])
    rk = _rope(yk, cos, sin) if rope else yk
    put(kk_ref, 0, jnp.where(lt64, rk, 0.0).astype(BF16))
    put(kk_ref, LANE, jnp.where(lt64, pltpu.roll(rk, HEAD_DIM, axis=1), 0.0).astype(BF16))
    yv = _mm(h, w_ref[:, C_V:C_V + KV_W])
    put(vv_ref, 0, yv.astype(BF16))
    if not latent:
        put(kf_ref, 0, yk)
        put(vf_ref, 0, yv)

    put(ga_ref, 0, _silu(_mm(h, w_ref[:, C_GA:C_GA + ATTN_W])).astype(BF16))

    wd = HYENA_W
    col = lambda k, n=1: slice(C_CV + k * wd, C_CV + (k + n) * wd)
    sub = lax.broadcasted_iota(jnp.int32, (8, 1), 0)

    def set_row(arr, r, value):
        g0 = r - r % 8
        fixed = jnp.where(sub == r % 8, value, arr[g0:g0 + 8])
        parts = ([arr[:g0]] if g0 else []) + [fixed] + ([arr[g0 + 8:]] if g0 + 8 < arr.shape[0] else [])
        return jnp.concatenate(parts, axis=0)

    def dwconv3(u, u_halo, k):
        prev = pltpu.roll(u, 1, axis=0)
        nxt = pltpu.roll(u, ROWS - 1, axis=0)
        if latent:
            prev = set_row(prev, 0, jnp.where(first_row == 0, 0.0, u_halo[7:8]))
            nxt = set_row(nxt, ROWS - 1, jnp.where(first_row == seq - ROWS, 0.0, u_halo[8:9]))
        else:
            for s in range(ROWS // seq):
                prev = set_row(prev, s * seq, 0.0)
                nxt = set_row(nxt, (s + 1) * seq - 1, 0.0)
        return prev * wc_ref[0:1, k, :] + u * wc_ref[1:2, k, :] + nxt * wc_ref[2:3, k, :]

    def halo(cols):
        return _mm(h_halo, w_ref[:, cols]) if latent else None

    def emit(k, val):
        if not latent:
            put(cv_ref, k * wd, val.astype(BF16))
            return
        n2 = cv_ref.shape[0]
        nsl = wd // LANE
        for sl in range(nsl):
            for i in range(N1_STEP):
                tmp_ref[k * nsl + sl, i * ROW_PITCH:i * ROW_PITCH + n2, :] = (
                    val[i * n2:(i + 1) * n2, sl * LANE:(sl + 1) * LANE])
        for j in range(n2):
            for sl in range(nsl):
                lo = k * wd + sl * LANE
                cv_ref[j, :, lo:lo + LANE] = (
                    tmp_ref[k * nsl + sl, pl.ds(j, N1_STEP, stride=ROW_PITCH), :].astype(BF16))

    ya = _mm(h, w_ref[:, col(0, 2)])
    ya_halo = halo(col(0, 2))
    emit(0, dwconv3(ya[:, :wd], None if ya_halo is None else ya_halo[:, :wd], 0))
    emit(1, dwconv3(ya[:, wd:], None if ya_halo is None else ya_halo[:, wd:], 1))
    yb = _mm(h, w_ref[:, col(2, 2)])
    emit(2, dwconv3(yb[:, :wd], halo(col(2)), 2) * _silu(yb[:, wd:]))
    yc = _mm(h, w_ref[:, col(4, 2)])
    yd = _mm(h, w_ref[:, col(6, 2)])
    inner_halo = halo(col(5)) * halo(col(6)) if latent else None
    inner = dwconv3(yc[:, wd:] * yd[:, :wd], inner_halo, 3)
    emit(3, yc[:, :wd] * inner * _silu(yd[:, wd:]))


def _in_proj(grp, x, mod, g, w, wconv, layer, rope_tabs=None, caches=None):
    B, L = grp.B, grp.L
    cvw = 4 * HYENA_W
    const2 = lambda *_: (0, 0)
    wspec = pl.BlockSpec((None, D_MODEL, IN_COLS), lambda *_: (layer, 0, 0))
    wcspec = pl.BlockSpec((3, 4, HYENA_W), lambda *_: (0, 0, 0))
    aliases = {}
    if grp.two_stage:
        n1h, n2 = grp.N1h, grp.N2
        assert ROWS == N1_STEP * n2
        nblk = L // ROWS
        grid = (B, nblk)
        nat = lambda wd: pl.BlockSpec((None, ROWS, wd), lambda b, j: (b, j, 0))
        nat_shape = lambda wd, dt: jax.ShapeDtypeStruct((B, L, wd), dt)
        x8 = x.reshape(B, L // 8, 8, D_MODEL)
        per8 = ROWS // 8
        in_specs = [nat(D_MODEL),
                    pl.BlockSpec((None, None, 8, D_MODEL), lambda b, j: (b, jnp.maximum(j * per8 - 1, 0), 0, 0)),
                    pl.BlockSpec((None, None, 8, D_MODEL),
                                 lambda b, j: (b, jnp.minimum((j + 1) * per8, L // 8 - 1), 0, 0)),
                    pl.BlockSpec((None, 1, 3 * D_MODEL), lambda b, j: (b, 0, 0)),
                    pl.BlockSpec((1, D_MODEL), const2),
                    wspec, wcspec,
                    pl.BlockSpec((ROWS, LANE), lambda b, j: (j, 0)),
                    pl.BlockSpec((ROWS, LANE), lambda b, j: (j, 0))]
        args = [x, x8, x8, mod, g, w, wconv, rope_tabs[0], rope_tabs[1]]
        out_specs = [nat(2 * ATTN_W), nat(2 * LANE), nat(KV_W), nat(ATTN_W),
                     pl.BlockSpec((None, n2, N1_STEP, cvw), lambda b, j: (b, 0, j, 0))]
        out_shape = [nat_shape(2 * ATTN_W, BF16), nat_shape(2 * LANE, BF16), nat_shape(KV_W, BF16),
                     nat_shape(ATTN_W, BF16), jax.ShapeDtypeStruct((B, n2, n1h, cvw), BF16)]
        scratch = [pltpu.VMEM((cvw // LANE, N1_STEP * ROW_PITCH, LANE), F32)]
        sem = ("arbitrary", "arbitrary")
    else:
        bb = ROWS // L
        grid = (B // bb,)
        nat = lambda wd: pl.BlockSpec((bb, L, wd), lambda i: (i, 0, 0))
        nat_shape = lambda wd, dt: jax.ShapeDtypeStruct((B, L, wd), dt)
        in_specs = [nat(D_MODEL),
                    pl.BlockSpec((None, 1, 3 * D_MODEL), lambda i: (0, 0, 0)),
                    pl.BlockSpec((1, D_MODEL), const2),
                    wspec, wcspec,
                    pl.BlockSpec(memory_space=pl.ANY),
                    pl.BlockSpec(memory_space=pl.ANY)]
        args = [x, mod, g, w, wconv, caches[0], caches[1]]
        aliases = {5: 5, 6: 6}
        cache_spec = pl.BlockSpec((bb, None, L, KV_W), lambda i: (i, layer, 0, 0))
        cache_shape = jax.ShapeDtypeStruct((B, DEPTH, L, KV_W), F32)
        out_specs = [nat(2 * ATTN_W), nat(2 * LANE), nat(KV_W), nat(ATTN_W), nat(cvw), cache_spec, cache_spec]
        out_shape = [nat_shape(2 * ATTN_W, BF16), nat_shape(2 * LANE, BF16), nat_shape(KV_W, BF16),
                     nat_shape(ATTN_W, BF16), nat_shape(cvw, BF16), cache_shape, cache_shape]
        scratch = []
        sem = ("arbitrary",)
    outs = pl.pallas_call(
        functools.partial(_in_proj_kernel, latent=grp.two_stage, seq=L),
        grid=grid, in_specs=in_specs, out_specs=out_specs, out_shape=out_shape,
        scratch_shapes=scratch, input_output_aliases=aliases, compiler_params=_params(*sem),
        name="in_proj_lat" if grp.two_stage else "in_proj_ctx",
    )(*args)
    outs = list(outs)
    if not grp.two_stage:
        outs[4] = outs[4].reshape(B, 1, L, cvw)
    return outs


def _window_bias():
    qi = np.arange(BLOCK)[:, None]
    kj = np.arange(3 * BLOCK)[None, :]
    return np.where(np.abs(kj - BLOCK - qi) <= WINDOW, 0.0, NEG_INF).astype(np.float32)


def _attn_kernel(*refs, lq, local, nblocks, qblocks):
    if local:
        sink_ref, bias_ref, qq_ref, kk_ref, vv_ref, ck_ref, cv_ref, ga_ref, o_ref = refs
    else:
        sink_ref, qq_ref, kk_ref, vv_ref, ga_ref, o_ref = refs
    group = N_HEADS // N_KV_HEADS
    rows = group * lq
    nt = (((1,), (1,)), ((), ()))
    lt64 = _lane_lt64((lq, LANE))

    for qb in range(qblocks):
        qrows = slice(qb * lq, (qb + 1) * lq)
        if local:
            n = pl.program_id(1) * qblocks + qb
            starts = [pl.multiple_of(jnp.maximum(n - 1, 0) * BLOCK, BLOCK),
                      pl.multiple_of(n * BLOCK, BLOCK),
                      pl.multiple_of(jnp.minimum(n + 1, nblocks - 1) * BLOCK, BLOCK)]
            kj = lax.broadcasted_iota(jnp.int32, (1, 3 * BLOCK), 1)
            edge = (jnp.where((kj < BLOCK) & (n == 0), NEG_INF, 0.0)
                    + jnp.where((kj >= 2 * BLOCK) & (n == nblocks - 1), NEG_INF, 0.0))
            bias = bias_ref[...] + edge
        for g in range(N_KV_HEADS):
            lhs = jnp.concatenate(
                [qq_ref[qrows, (group * g + i) * LANE:(group * g + i + 1) * LANE] for i in range(group)], axis=0)
            gl = slice(g * LANE, (g + 1) * LANE)
            if local:
                kwin = jnp.concatenate([kk_ref[pl.ds(s, BLOCK), gl] for s in starts], axis=0)
                vals = jnp.concatenate([vv_ref[pl.ds(s, BLOCK), :] for s in starts] + [cv_ref[...]], axis=0)
                s_loc = lax.dot_general(lhs, kwin, nt, preferred_element_type=F32)
                s_ctx = lax.dot_general(lhs, ck_ref[:, gl], nt, preferred_element_type=F32)
            else:
                vals = vv_ref[qrows, :]
                s_loc = lax.dot_general(lhs, kk_ref[qrows, gl], nt, preferred_element_type=F32)
            row_head = lax.broadcasted_iota(jnp.int32, (rows, 1), 0) // lq
            snk = jnp.zeros((rows, 1), F32)
            for i in range(group):
                snk = jnp.where(row_head == i, sink_ref[group * g + i], snk)
            if local:
                s_loc = s_loc + jnp.concatenate([bias] * group, axis=0)
            m = jnp.maximum(jnp.max(s_loc, axis=-1, keepdims=True), snk)
            if local:
                m = jnp.maximum(m, jnp.max(s_ctx, axis=-1, keepdims=True))
            ones = jnp.ones((vals.shape[0], LANE), BF16)
            vext = jnp.concatenate([vals, ones], axis=1)
            nk = s_loc.shape[1]
            o = _mm(jnp.exp(s_loc - m).astype(BF16), vext[:nk])
            if local:
                o = o + _mm(jnp.exp(s_ctx - m).astype(BF16), vext[nk:])
            o = o[:, :LANE] / (o[:, LANE:] + jnp.exp(snk - m))
            for jj in range(group // 2):
                a = o[(2 * jj) * lq:(2 * jj + 1) * lq]
                b = o[(2 * jj + 1) * lq:(2 * jj + 2) * lq]
                if g == 0:
                    tile = jnp.where(lt64, a, pltpu.roll(b, HEAD_DIM, axis=1))
                else:
                    tile = jnp.where(lt64, pltpu.roll(a, HEAD_DIM, axis=1), b)
                j = (group // 2) * g + jj
                cols = slice(j * LANE, (j + 1) * LANE)
                o_ref[qrows, cols] = (tile * ga_ref[qrows, cols].astype(F32)).astype(BF16)


def _attention(grp, sink, qq, kk, vv, ga, layer=0, ctx=None):
    B, L = grp.B, grp.L
    smem = pl.BlockSpec(memory_space=pltpu.SMEM)
    if ctx is not None:
        ck, cv = ctx
        nb = L // BLOCK
        lc = ck.shape[2]
        qrows = ATTN_QBLOCKS * BLOCK
        grid = (B, nb // ATTN_QBLOCKS)
        blk = lambda wd: pl.BlockSpec((None, qrows, wd), lambda b, n: (b, n, 0))
        full = lambda rows, wd: pl.BlockSpec((None, rows, wd), lambda b, n: (b, 0, 0))
        cache = lambda wd: pl.BlockSpec((None, None, lc, wd), lambda b, n: (b, layer, 0, 0))
        in_specs = [smem, pl.BlockSpec((BLOCK, 3 * BLOCK), lambda b, n: (0, 0)),
                    blk(2 * ATTN_W), full(L, 2 * LANE), full(L, KV_W), cache(2 * LANE), cache(KV_W),
                    blk(ATTN_W)]
        args = [sink, jnp.asarray(_window_bias()), qq, kk, vv, ck, cv, ga]
        out_spec = blk(ATTN_W)
        out_shape = jax.ShapeDtypeStruct((B, L, ATTN_W), BF16)
        kern = functools.partial(_attn_kernel, lq=BLOCK, local=True, nblocks=nb, qblocks=ATTN_QBLOCKS)
        sem = ("arbitrary", "arbitrary")
    else:
        grid = (B // CTX_SEQS,)
        flat = lambda a: a.reshape(B * L, a.shape[-1])
        blk = lambda wd: pl.BlockSpec((CTX_SEQS * L, wd), lambda i: (i, 0))
        in_specs = [smem, blk(2 * ATTN_W), blk(2 * LANE), blk(KV_W), blk(ATTN_W)]
        args = [sink, flat(qq), flat(kk), flat(vv), flat(ga)]
        out_spec = blk(ATTN_W)
        out_shape = jax.ShapeDtypeStruct((B * L, ATTN_W), BF16)
        kern = functools.partial(_attn_kernel, lq=L, local=False, nblocks=1, qblocks=CTX_SEQS)
        sem = ("arbitrary",)
    out = pl.pallas_call(
        kern, grid=grid, in_specs=in_specs, out_specs=out_spec, out_shape=out_shape,
        compiler_params=_params(*sem), name="attn_lat" if ctx is not None else "attn_ctx",
    )(*args)
    return out.reshape(B, L, ATTN_W)


def _filter_kernel(tf_ref, tb_ref, ff_ref, fb_ref, w1_ref, b1_ref, w2_ref, b2_ref, w3_ref, fq_ref, dl_ref,
                   kf_ref, kb_ref, *, chunk):
    rows = tf_ref.shape[0]
    width = 2 * HYENA_W
    fq = fq_ref[...]

    def mlp(time, feats_t, w3):
        h = jnp.sin(fq * (_mm_hi(w1_ref[...], feats_t) + b1_ref[...]))
        h = jnp.sin(fq * (_mm_hi(w2_ref[...], h) + b2_ref[...]))
        out = lax.dot_general(h, w3, (((0,), (0,)), ((), ())), preferred_element_type=F32,
                              precision=lax.Precision.HIGHEST)
        dec = jnp.exp(-(time * dl_ref[...]))
        return out * jnp.concatenate([dec, dec], axis=1)

    def body(i, acc):
        r0 = pl.multiple_of(i * chunk, chunk)
        kf = mlp(tf_ref[pl.ds(r0, chunk), :], ff_ref[:, pl.ds(r0, chunk)], w3_ref[:, 0:width])
        kb = mlp(tb_ref[pl.ds(r0, chunk), :], fb_ref[:, pl.ds(r0, chunk)], w3_ref[:, width:2 * width])
        rid = lax.broadcasted_iota(jnp.int32, kb.shape, 0) + r0
        kb = jnp.where(rid == 0, 0.0, kb)
        kf_ref[pl.ds(r0, chunk), :] = kf
        kb_ref[pl.ds(r0, chunk), :] = kb
        return acc + jnp.sum(jnp.abs(kf) + jnp.abs(kb), axis=0, keepdims=True)

    total = lax.fori_loop(0, rows // chunk, body, jnp.zeros((1, width), F32))

    def scale(i, c):
        r0 = pl.multiple_of(i * chunk, chunk)
        kf_ref[pl.ds(r0, chunk), :] = kf_ref[pl.ds(r0, chunk), :] / total
        kb_ref[pl.ds(r0, chunk), :] = kb_ref[pl.ds(r0, chunk), :] / total
        return c

    lax.fori_loop(0, rows // chunk, scale, 0)


def _filters(grp, feats, w1, b1, w2, b2, w3, fq, deltas):
    L = grp.L
    chunk = min(512, L)
    c2 = lambda l: (0, 0)
    per = lambda a, b: pl.BlockSpec((None, a, b), lambda l: (l, 0, 0))
    shape = jax.ShapeDtypeStruct((DEPTH, L, 2 * HYENA_W), F32)
    return pl.pallas_call(
        functools.partial(_filter_kernel, chunk=chunk), grid=(DEPTH,),
        in_specs=[pl.BlockSpec((L, 1), c2), pl.BlockSpec((L, 1), c2),
                  pl.BlockSpec((LANE, L), c2), pl.BlockSpec((LANE, L), c2),
                  per(FILTER_HIDDEN, LANE), per(FILTER_HIDDEN, 1), per(FILTER_HIDDEN, FILTER_HIDDEN),
                  per(FILTER_HIDDEN, 1), per(FILTER_HIDDEN, 4 * HYENA_W), per(FILTER_HIDDEN, 1),
                  pl.BlockSpec((1, HYENA_W), c2)],
        out_specs=[per(L, 2 * HYENA_W)] * 2, out_shape=[shape] * 2,
        compiler_params=_params("arbitrary"),
        name="filters_lat" if grp.two_stage else "filters_ctx",
    )(*feats, w1, b1, w2, b2, w3, fq, deltas)


def _store_spectrum_rows(s_ref, base, a, n1):
    for comp in range(2):
        for sl in range(2):
            s_ref[comp, sl, pl.ds(base, n1), :] = a[comp * n1:(comp + 1) * n1, sl * LANE:(sl + 1) * LANE]


def _load_column(s_ref, k1, n2, pitch):
    parts = [jnp.concatenate([s_ref[comp, sl, pl.ds(k1, n2, stride=pitch), :] for sl in range(2)], axis=1)
             for comp in range(2)]
    return jnp.concatenate(parts, axis=0).astype(BF16)


def _spectrum2_kernel(kf_ref, kb_ref, faf_ref, g_ref, o_ref, s_ref, *, n1, n2):
    n1h = n1 // 2
    pitch = n1 + PITCH_PAD
    faf = faf_ref[...]

    def stage_a(i, c):
        for t in range(UNROLL):
            j = i * UNROLL + t
            r0 = pl.multiple_of(j * n1h, n1h)
            rhs = jnp.concatenate([kf_ref[pl.ds(r0, n1h), :], kb_ref[pl.ds(r0, n1h), :]], axis=0).astype(BF16)
            _store_spectrum_rows(s_ref, pl.multiple_of(j * pitch, 8), _mm(faf, rhs), n1)
        return c

    lax.fori_loop(0, n2 // UNROLL, stage_a, 0)

    def stage_c(i, c):
        for t in range(UNROLL):
            k1 = i * UNROLL + t
            o_ref[k1] = _mm(g_ref[k1], _load_column(s_ref, k1, n2, pitch)).astype(BF16)
        return c

    lax.fori_loop(0, n1 // UNROLL, stage_c, 0)


def _spectrum1_kernel(kf_ref, kb_ref, faf_ref, o_ref):
    rhs = jnp.concatenate([kf_ref[...], kb_ref[...]], axis=0).astype(BF16)
    o_ref[...] = _mm(faf_ref[...], rhs).astype(BF16)


def _spectrum(grp, kf, kb, faf, g):
    n1, n2, L = grp.N1, grp.N2, grp.L
    wd = 2 * LANE
    nblk = 2 * HYENA_W // wd
    kin = pl.BlockSpec((None, L, wd), lambda l, c: (l, 0, c))
    if grp.two_stage:
        pitch = n1 + PITCH_PAD
        return pl.pallas_call(
            functools.partial(_spectrum2_kernel, n1=n1, n2=n2), grid=(DEPTH, nblk),
            in_specs=[kin, kin, pl.BlockSpec((2 * n1, n1), lambda l, c: (0, 0)),
                      pl.BlockSpec((n1, 2 * n2, 2 * n2), lambda l, c: (0, 0, 0))],
            out_specs=pl.BlockSpec((None, n1, 2 * n2, wd), lambda l, c: (l, 0, 0, c)),
            out_shape=jax.ShapeDtypeStruct((DEPTH, n1, 2 * n2, 2 * HYENA_W), BF16),
            scratch_shapes=[pltpu.VMEM((2, 2, n2 * pitch, LANE), F32)],
            compiler_params=_params("arbitrary", "arbitrary"), name="spectrum_lat",
        )(kf, kb, faf, g)
    return pl.pallas_call(
        _spectrum1_kernel, grid=(DEPTH, nblk),
        in_specs=[kin, kin, pl.BlockSpec((2 * n1, n1), lambda l, c: (0, 0))],
        out_specs=pl.BlockSpec((None, 2 * n1, wd), lambda l, c: (l, 0, c)),
        out_shape=jax.ShapeDtypeStruct((DEPTH, 2 * n1, 2 * HYENA_W), BF16),
        compiler_params=_params("arbitrary", "arbitrary"), name="spectrum_ctx",
    )(kf, kb, faf)


def _pair_operand(u_ref, j):
    nb = u_ref.shape[0]
    re = jnp.concatenate([u_ref[b, j] for b in range(0, nb, 2)], axis=1)
    im = jnp.concatenate([u_ref[b, j] for b in range(1, nb, 2)], axis=1)
    return jnp.concatenate([re, im], axis=0)


def _cmul(x, k, half):
    k = k.astype(F32)
    npair = x.shape[1] // LANE
    kr = jnp.concatenate([k[:half]] * npair, axis=1)
    ki = jnp.concatenate([k[half:]] * npair, axis=1)
    xr, xi = x[:half], x[half:]
    return jnp.concatenate([xr * kr - xi * ki, xr * ki + xi * kr], axis=0).astype(BF16)


def _conv_epilogue(u_ref, m_ref, o_ref, d, y, j, n1h):
    for b in range(u_ref.shape[0]):
        yb = y[(b % 2) * n1h:(b % 2 + 1) * n1h, (b // 2) * LANE:(b // 2 + 1) * LANE]
        u = u_ref[b, j].astype(F32)
        o_ref[b, j] = (m_ref[b, j].astype(F32) * (yb + u * d)).astype(BF16)


def _conv2_kernel(u_ref, m_ref, fa_ref, fat_ref, g_ref, k_ref, d_ref, o_ref, s_ref, *, n1, n2):
    n1h = n1 // 2
    pitch = n1 + PITCH_PAD
    fa = fa_ref[...]
    fat = fat_ref[...]
    d = d_ref[...]

    def stage_a(i, c):
        for t in range(UNROLL_WIDE):
            j = i * UNROLL_WIDE + t
            _store_spectrum_rows(s_ref, pl.multiple_of(j * pitch, 8), _mm(fa, _pair_operand(u_ref, j)), n1)
        return c

    lax.fori_loop(0, n2 // UNROLL_WIDE, stage_a, 0)

    def stage_c(i, c):
        ks = [i * UNROLL_WIDE + t for t in range(UNROLL_WIDE)]
        cols = [_load_column(s_ref, k1, n2, pitch) for k1 in ks]
        backs = []
        for k1, col in zip(ks, cols):
            g = g_ref[k1]
            y = _cmul(_mm(g, col), k_ref[k1], n2)
            backs.append(lax.dot_general(g, y, (((0,), (0,)), ((), ())), preferred_element_type=F32))
        for k1, back in zip(ks, backs):
            for comp in range(2):
                for sl in range(2):
                    s_ref[comp, sl, pl.ds(k1, n2, stride=pitch), :] = (
                        back[comp * n2:(comp + 1) * n2, sl * LANE:(sl + 1) * LANE])
        return c

    lax.fori_loop(0, n1 // UNROLL_WIDE, stage_c, 0)

    def stage_inv(i, c):
        for t in range(UNROLL):
            j = i * UNROLL + t
            base = pl.multiple_of(j * pitch, 8)
            parts = [jnp.concatenate([s_ref[comp, sl, pl.ds(base, n1), :] for sl in range(2)], axis=1)
                     for comp in range(2)]
            y = _mm(fat, jnp.concatenate(parts, axis=0).astype(BF16))
            _conv_epilogue(u_ref, m_ref, o_ref, d, y, j, n1h)
        return c

    lax.fori_loop(0, n2 // UNROLL, stage_inv, 0)


def _conv1_kernel(u_ref, m_ref, fa_ref, fat_ref, k_ref, d_ref, o_ref, *, n1):
    x = _mm(fa_ref[...], _pair_operand(u_ref, 0))
    y = _mm(fat_ref[...], _cmul(x, k_ref[...], n1))
    _conv_epilogue(u_ref, m_ref, o_ref, d_ref[...], y, 0, n1 // 2)


def _long_conv(grp, u, m, spec, d, layer, order, tabs):
    B, n1, n2, n1h = grp.B, grp.N1, grp.N2, grp.N1h
    ncb = HYENA_W // LANE
    nbatch = 4 if grp.two_stage else 8
    grid = (ncb, B // nbatch)
    group = lambda k: pl.BlockSpec((nbatch, n2, n1h, LANE), lambda c, q: (q, 0, 0, k * ncb + c))
    (u, ku), (m, km) = u, m
    uspec, mspec = group(ku), group(km)
    data = group(0)
    dspec = pl.BlockSpec((None, 1, LANE), lambda c, q: (layer, 0, order * ncb + c))
    fa = pl.BlockSpec((2 * n1, n1), lambda c, q: (0, 0))
    fat = pl.BlockSpec((n1, 2 * n1), lambda c, q: (0, 0))
    out_shape = jax.ShapeDtypeStruct((B, n2, n1h, HYENA_W), BF16)
    if grp.two_stage:
        pitch = n1 + PITCH_PAD
        return pl.pallas_call(
            functools.partial(_conv2_kernel, n1=n1, n2=n2), grid=grid,
            in_specs=[uspec, mspec, fa, fat,
                      pl.BlockSpec((n1, 2 * n2, 2 * n2), lambda c, q: (0, 0, 0)),
                      pl.BlockSpec((None, n1, 2 * n2, LANE), lambda c, q: (layer, 0, 0, order * ncb + c)),
                      dspec],
            out_specs=data, out_shape=out_shape,
            scratch_shapes=[pltpu.VMEM((2, 2, n2 * pitch, LANE), F32)],
            compiler_params=_params("arbitrary", "arbitrary"), name="long_conv_lat",
        )(u, m, tabs["fa"], tabs["fat"], tabs["g"], spec, d)
    return pl.pallas_call(
        functools.partial(_conv1_kernel, n1=n1), grid=grid,
        in_specs=[uspec, mspec, fa, fat,
                  pl.BlockSpec((None, 2 * n1, LANE), lambda c, q: (layer, 0, order * ncb + c)), dspec],
        out_specs=data, out_shape=out_shape,
        compiler_params=_params("arbitrary", "arbitrary"), name="long_conv_ctx",
    )(u, m, tabs["fa"], tabs["fat"], spec, d)


def _out_proj_kernel(*refs, permute, final):
    if permute:
        x_ref, mod_ref, a_ref, z_ref, s_ref, w_ref, fg_ref, o_ref, tmp_ref = refs
    else:
        x_ref, mod_ref, a_ref, z_ref, s_ref, w_ref, fg_ref, o_ref = refs
    a = a_ref[...].reshape(ROWS, ATTN_W)
    nsl = HYENA_W // LANE
    if permute:
        n2 = z_ref.shape[0]
        for j in range(n2):
            for sl in range(nsl):
                cols = slice(sl * LANE, (sl + 1) * LANE)
                tmp_ref[sl, pl.ds(j, N1_STEP, stride=ROW_PITCH), :] = z_ref[j, :, cols].astype(F32)
                tmp_ref[nsl + sl, pl.ds(j, N1_STEP, stride=ROW_PITCH), :] = s_ref[j, :, cols].astype(F32)
        conv = jnp.concatenate(
            [jnp.concatenate([tmp_ref[i, k * ROW_PITCH:k * ROW_PITCH + n2, :] for k in range(N1_STEP)], axis=0)
             for i in range(2 * nsl)], axis=1).astype(BF16)
    else:
        conv = jnp.concatenate([z_ref[...].reshape(ROWS, HYENA_W), s_ref[...].reshape(ROWS, CONV_W)], axis=1)
    y = _mm(a, w_ref[0:ATTN_W, :]) + _mm(conv, w_ref[ATTN_W:, :])
    gate = mod_ref[:, 2 * D_MODEL:3 * D_MODEL]
    xn = x_ref[...].reshape(ROWS, D_MODEL) + gate * y
    if final:
        ms = jnp.mean(xn * xn, axis=-1, keepdims=True)
        xn = xn * lax.rsqrt(ms + RMS_EPS) * fg_ref[...]
    o_ref[...] = xn.reshape(o_ref.shape)


def _out_proj(grp, x, mod, attn, zg, scg, w, layer, final_g, final):
    B, L = grp.B, grp.L
    const2 = lambda *_: (0, 0)
    wspec = pl.BlockSpec((None, D_MODEL, D_MODEL), lambda *_: (layer, 0, 0))
    (zg, kz), (scg, ks) = zg, scg
    if grp.two_stage:
        n2 = grp.N2
        grid = (B, L // ROWS)
        nat = lambda wd: pl.BlockSpec((None, ROWS, wd), lambda b, j: (b, j, 0))
        cvl = lambda k: pl.BlockSpec((None, n2, N1_STEP, HYENA_W), lambda b, j: (b, 0, j, k))
        in_specs = [nat(D_MODEL), pl.BlockSpec((None, 1, 3 * D_MODEL), lambda b, j: (b, 0, 0)),
                    nat(ATTN_W), cvl(kz), cvl(ks), wspec, pl.BlockSpec((1, D_MODEL), const2)]
        scratch = [pltpu.VMEM((2 * HYENA_W // LANE, N1_STEP * ROW_PITCH, LANE), F32)]
        sem = ("arbitrary", "arbitrary")
    else:
        bb = ROWS // L
        grid = (B // bb,)
        nat = lambda wd: pl.BlockSpec((bb, L, wd), lambda i: (i, 0, 0))
        cvl = lambda k: pl.BlockSpec((bb, None, L, HYENA_W), lambda i: (i, 0, 0, k))
        in_specs = [nat(D_MODEL), pl.BlockSpec((None, 1, 3 * D_MODEL), lambda i: (0, 0, 0)),
                    nat(ATTN_W), cvl(kz), cvl(ks), wspec, pl.BlockSpec((1, D_MODEL), const2)]
        scratch = []
        sem = ("arbitrary",)
    return pl.pallas_call(
        functools.partial(_out_proj_kernel, permute=grp.two_stage, final=final),
        grid=grid, in_specs=in_specs, out_specs=nat(D_MODEL),
        out_shape=jax.ShapeDtypeStruct((B, L, D_MODEL), F32),
        scratch_shapes=scratch, compiler_params=_params(*sem),
        name="out_proj_lat" if grp.two_stage else "out_proj_ctx",
    )(x, mod, attn, zg, scg, w, final_g)


def _group_tables(grp):
    fa, faf, g = _dft_tables(grp)
    tabs = {"fa": jnp.asarray(fa).astype(BF16), "fat": jnp.asarray(fa.T.copy()).astype(BF16),
            "faf": jnp.asarray(faf).astype(BF16)}
    if g is not None:
        tabs["g"] = jnp.asarray(g).astype(BF16)
    tabs["feats"] = tuple(jnp.asarray(a) for a in _filter_features(grp))
    return tabs


def kernel(x_prompt, x_sample, cache_k, cache_v, c, c_ctx, norm_g, mod_w, mod_b, w_in, attn_sink,
           hy_conv_w, hy_filt_w1, hy_filt_b1, hy_filt_w2, hy_filt_b2, hy_filt_w3, hy_filt_freq, hy_d,
           sc_conv_w, w_out, final_g):
    ctx = Group(x_prompt.shape[0], x_prompt.shape[1], 2 * x_prompt.shape[1], 1)
    lat = Group(x_sample.shape[0], x_sample.shape[1], 128, 2 * x_sample.shape[1] // 128)
    nlat = lat.B

    cond = jnp.zeros((16, D_MODEL), F32).at[0].set(c_ctx).at[1:1 + nlat].set(c)
    mods = _modulation(cond, mod_w, mod_b)

    w1t = jnp.pad(hy_filt_w1, ((0, 0), (0, LANE - FILTER_EMB), (0, 0))).transpose(0, 2, 1)
    w2t = hy_filt_w2.transpose(0, 2, 1)
    b1 = hy_filt_b1[:, :, None]
    b2 = hy_filt_b2[:, :, None]
    fq = hy_filt_freq[:, :, None]
    deltas = jnp.asarray(_decay_rates())
    tabs, spec = {}, {}
    for grp in (ctx, lat):
        t = _group_tables(grp)
        kf, kb = _filters(grp, t["feats"], w1t, b1, w2t, b2, hy_filt_w3, fq, deltas)
        spec[grp] = _spectrum(grp, kf, kb, t["faf"], t.get("g"))
        tabs[grp] = t
    rope = tuple(jnp.asarray(a) for a in _rope_tables(lat.L))

    w_in_b = w_in.astype(BF16)
    w_out_b = w_out.astype(BF16)
    wconv = jnp.concatenate([hy_conv_w.reshape(DEPTH, 3, 3, HYENA_W), sc_conv_w[:, :, None, :]], axis=2)
    dskip = hy_d.reshape(DEPTH, 1, 2 * HYENA_W)
    fg = final_g[None, :]
    lc = cache_k.shape[2]
    zpad = jnp.zeros((nlat, DEPTH, lc, HEAD_DIM), F32)
    ck_pad = jnp.concatenate([zpad, cache_k[:, :, :, 0], zpad, cache_k[:, :, :, 1]], axis=-1).astype(BF16)
    cv_nat = cache_v.reshape(nlat, DEPTH, lc, KV_W).astype(BF16)

    xp, xs = x_prompt, x_sample
    caches = (jnp.zeros((ctx.B, DEPTH, ctx.L, KV_W), F32), jnp.zeros((ctx.B, DEPTH, ctx.L, KV_W), F32))
    for l in range(DEPTH):
        g = norm_g[l][None, :]
        last = l == DEPTH - 1
        for grp in (ctx, lat):
            is_lat = grp is lat
            x = xs if is_lat else xp
            mod = mods[l, 1:1 + nlat][:, None, :] if is_lat else mods[l, 0:1][:, None, :]
            if is_lat:
                qq, kk, vv, ga, cv = _in_proj(grp, x, mod, g, w_in_b, wconv[l], l, rope_tabs=rope)
                attn = _attention(grp, attn_sink[l], qq, kk, vv, ga, l, ctx=(ck_pad, cv_nat))
            else:
                qq, kk, vv, ga, cv, *caches = _in_proj(grp, x, mod, g, w_in_b, wconv[l], l, caches=caches)
                attn = _attention(grp, attn_sink[l], qq, kk, vv, ga)
            z1 = _long_conv(grp, (cv, 0), (cv, 1), spec[grp], dskip, l, 0, tabs[grp])
            zg = _long_conv(grp, (z1, 0), (cv, 2), spec[grp], dskip, l, 1, tabs[grp])
            xn = _out_proj(grp, x, mod, attn, (zg, 0), (cv, 3), w_out_b, l, fg, last)
            if is_lat:
                xs = xn
            else:
                xp = xn
    shape = (ctx.B, DEPTH, ctx.L, N_KV_HEADS, HEAD_DIM)
    return (xp, xs, caches[0].reshape(shape), caches[1].reshape(shape))
```

```python
import functools
import math

import numpy as np
import jax
import jax.numpy as jnp
from jax import lax
from jax.experimental import pallas as pl
from jax.experimental.pallas import tpu as pltpu

F32 = jnp.float32
BF16 = jnp.bfloat16

D_MODEL = 1024
DEPTH = 4
GRID_W = 64
N_HEADS = 8
N_KV_HEADS = 2
HEAD_DIM = 64
ATTN_W = N_HEADS * HEAD_DIM
KV_W = N_KV_HEADS * HEAD_DIM
HYENA_W = 256
CONV_W = 256
WINDOW = 128
BLOCK = 128
FILTER_EMB = 33
FILTER_HIDDEN = 64
HYENA_TARGET = 1e-2
FAST_DECAY_PCT = 0.3
SLOW_DECAY_PCT = 1.5
ROPE_BASE = 10000.0
RMS_EPS = 1e-6
NEG_INF = -1e30
IN_COLS = 3328
C_Q, C_K, C_V, C_GA, C_CV = 0, 512, 640, 768, 1280
CV_W = 2048

LANE = 128
ROWS = 1024
N1_STEP = 16
ROW_PITCH = 72
VMEM_LIMIT = 56 * 1024 * 1024
PITCH_PAD = 8
UNROLL = 8
UNROLL_WIDE = 16
ATTN_QBLOCKS = 4
CTX_SEQS = 2


class Group:
    def __init__(self, batch, seq, n1, n2):
        self.B, self.L, self.N1, self.N2 = batch, seq, n1, n2
        self.N1h = n1 // 2
        assert self.N1h * n2 == seq
        self.two_stage = n2 > 1


def _mm(a, b):
    return jnp.dot(a, b, preferred_element_type=F32)


def _mm_hi(a, b):
    return jnp.dot(a, b, preferred_element_type=F32, precision=lax.Precision.HIGHEST)


def _silu(x):
    return x * (1.0 / (1.0 + jnp.exp(-x)))


def _params(*sem):
    return pltpu.CompilerParams(dimension_semantics=sem, vmem_limit_bytes=VMEM_LIMIT)


def _rope_tables(seq):
    t = np.arange(seq)
    n_freq = HEAD_DIM // 4
    inv = ROPE_BASE ** (-np.arange(n_freq, dtype=np.float64) / n_freq)
    row = (t // GRID_W)[:, None] * inv
    col = (t % GRID_W)[:, None] * inv
    cos = np.concatenate([np.cos(row), np.cos(row), np.cos(col), np.cos(col)], axis=1)
    sin = np.concatenate([-np.sin(row), np.sin(row), -np.sin(col), np.sin(col)], axis=1)
    return (np.tile(cos, (1, 2)).astype(np.float32), np.tile(sin, (1, 2)).astype(np.float32))


def _conv_order(grp, table):
    return table.reshape(grp.N1h, grp.N2, -1).transpose(1, 0, 2).reshape(grp.L, -1)


def _filter_features(grp):
    L = grp.L
    t = np.linspace(0.0, 1.0, L)[:, None]
    bands = (FILTER_EMB - 1) // 2
    ang = (2.0 * math.pi / L) * np.arange(L)[:, None]
    fr = np.linspace(1e-4, bands - 1, bands)[None, :]
    feats = np.concatenate([t, np.cos(fr * ang), -np.sin(fr * ang)], axis=-1)
    feats = np.pad(feats, ((0, 0), (0, LANE - FILTER_EMB)))
    feats = _conv_order(grp, feats).astype(np.float32)
    return feats[:, 0:1], np.ascontiguousarray(feats.T)


def _decay_rates():
    max_decay = math.log(HYENA_TARGET) / FAST_DECAY_PCT
    min_decay = math.log(HYENA_TARGET) / SLOW_DECAY_PCT
    return np.abs(np.linspace(min_decay, max_decay, HYENA_W))[None, :].astype(np.float32)


def _dft_tables(grp):
    n1, n2 = grp.N1, grp.N2
    n = n1 * n2
    k = np.arange(n1)
    f = np.exp(-2j * np.pi * ((k[:, None] * k[None, :]) % n1) / n1)
    fh = f[:, : grp.N1h]
    fa = np.block([[fh.real, -fh.imag], [fh.imag, fh.real]])
    frev = np.concatenate([f[:, : grp.N1h], f[:, grp.N1h:][:, ::-1]], axis=1)
    faf = np.concatenate([frev.real, frev.imag], axis=0) / n
    g = None
    if grp.two_stage:
        j = np.arange(n2)
        ph = (k[:, None, None] * j[None, None, :] + n1 * j[None, :, None] * j[None, None, :]) % n
        gc = np.exp(-2j * np.pi * ph / n)
        g = np.concatenate([np.concatenate([gc.real, -gc.imag], axis=2),
                            np.concatenate([gc.imag, gc.real], axis=2)], axis=1)
        g = g.astype(np.float32)
    return fa.astype(np.float32), faf.astype(np.float32), g


def _mod_kernel(c_ref, w_ref, b_ref, o_ref):
    o_ref[...] = _mm_hi(_silu(c_ref[...]), w_ref[...]) + b_ref[...]


def _modulation(cond, mod_w, mod_b):
    nb = 3 * D_MODEL // 1024
    return pl.pallas_call(
        _mod_kernel,
        grid=(DEPTH, nb),
        in_specs=[
            pl.BlockSpec((16, D_MODEL), lambda l, j: (0, 0)),
            pl.BlockSpec((None, D_MODEL, 1024), lambda l, j: (l, 0, j)),
            pl.BlockSpec((None, 1, 1024), lambda l, j: (l, 0, j)),
        ],
        out_specs=pl.BlockSpec((None, 16, 1024), lambda l, j: (l, 0, j)),
        out_shape=jax.ShapeDtypeStruct((DEPTH, 16, 3 * D_MODEL), F32),
        compiler_params=_params("arbitrary", "arbitrary"),
        name="modulation",
    )(cond, mod_w, mod_b.reshape(DEPTH, 1, 3 * D_MODEL))


def _lane_lt64(shape):
    return lax.broadcasted_iota(jnp.int32, shape, 1) < HEAD_DIM


def _rope(x, cos, sin):
    lane = lax.broadcasted_iota(jnp.int32, x.shape, 1)
    first = (lane % 32) < 16
    partner = jnp.where(first, pltpu.roll(x, LANE - 16, axis=1), pltpu.roll(x, 16, axis=1))
    return x * cos + partner * sin


def _in_proj_kernel(*refs, latent, seq):
    if latent:
        x_ref, xp_ref, xn_ref, mod_ref, g_ref, w_ref, wc_ref, cos_ref, sin_ref = refs[:9]
        qq_ref, kk_ref, vv_ref, ga_ref, cv_ref, tmp_ref = refs[9:]
        first_row = pl.program_id(1) * ROWS
    else:
        x_ref, mod_ref, g_ref, w_ref, wc_ref = refs[:5]
        qq_ref, kk_ref, vv_ref, ga_ref, cv_ref, kf_ref, vf_ref = refs[7:]
        first_row = 0
    rope = latent

    shift = mod_ref[:, 0:D_MODEL]
    scale = mod_ref[:, D_MODEL:2 * D_MODEL]

    def modulated_norm(x):
        ms = jnp.mean(x * x, axis=-1, keepdims=True)
        y = x * lax.rsqrt(ms + RMS_EPS) * g_ref[...]
        return (y * (1.0 + scale) + shift).astype(BF16)

    h = modulated_norm(x_ref[...].reshape(ROWS, D_MODEL))
    h_ext = (jnp.concatenate([h, modulated_norm(jnp.concatenate([xp_ref[...], xn_ref[...]], axis=0))], axis=0)
             if latent else h)

    if rope:
        cos = cos_ref[...]
        sin = sin_ref[...]
    lt64 = _lane_lt64((ROWS, LANE))

    def put(ref, lo, val):
        lead = ref.shape[:-1]
        width = val.shape[-1]
        ref[(slice(None),) * len(lead) + (slice(lo, lo + width),)] = val.reshape(lead + (width,))

    yq = _mm(h, w_ref[:, C_Q:C_Q + ATTN_W]) * (HEAD_DIM ** -0.5)
    for j in range(ATTN_W // LANE):
        xj = yq[:, j * LANE:(j + 1) * LANE]
        rj = _rope(xj, cos, sin) if rope else xj
        even = jnp.where(lt64, rj, pltpu.roll(xj, HEAD_DIM, axis=1))
        odd = jnp.where(lt64, pltpu.roll(rj, HEAD_DIM, axis=1), xj)
        put(qq_ref, (2 * j) * LANE, even.astype(BF16))
        put(qq_ref, (2 * j + 1) * LANE, odd.astype(BF16))

    ykv = _mm(h, w_ref[:, C_K:C_V + KV_W])
    yk, yv = ykv[:, :KV_W], ykv[:, KV_W:]
    rk = _rope(yk, cos, sin) if rope else yk
    put(kk_ref, 0, jnp.where(lt64, rk, 0.0).astype(BF16))
    put(kk_ref, LANE, jnp.where(lt64, pltpu.roll(rk, HEAD_DIM, axis=1), 0.0).astype(BF16))
    put(vv_ref, 0, yv.astype(BF16))
    if not latent:
        put(kf_ref, 0, yk)
        put(vf_ref, 0, yv)

    put(ga_ref, 0, _silu(_mm(h, w_ref[:, C_GA:C_GA + ATTN_W])).astype(BF16))

    wd = HYENA_W
    col = lambda k, n=1: slice(C_CV + k * wd, C_CV + (k + n) * wd)
    sub = lax.broadcasted_iota(jnp.int32, (8, 1), 0)

    def set_row(arr, r, value):
        g0 = r - r % 8
        fixed = jnp.where(sub == r % 8, value, arr[g0:g0 + 8])
        parts = ([arr[:g0]] if g0 else []) + [fixed] + ([arr[g0 + 8:]] if g0 + 8 < arr.shape[0] else [])
        return jnp.concatenate(parts, axis=0)

    def dwconv3(u, u_halo, k):
        prev = pltpu.roll(u, 1, axis=0)
        nxt = pltpu.roll(u, ROWS - 1, axis=0)
        if latent:
            prev = set_row(prev, 0, jnp.where(first_row == 0, 0.0, u_halo[7:8]))
            nxt = set_row(nxt, ROWS - 1, jnp.where(first_row == seq - ROWS, 0.0, u_halo[8:9]))
        else:
            for s in range(ROWS // seq):
                prev = set_row(prev, s * seq, 0.0)
                nxt = set_row(nxt, (s + 1) * seq - 1, 0.0)
        return prev * wc_ref[0:1, k, :] + u * wc_ref[1:2, k, :] + nxt * wc_ref[2:3, k, :]

    def emit(k, val):
        if not latent:
            put(cv_ref, k * wd, val.astype(BF16))
            return
        n2 = cv_ref.shape[0]
        nsl = wd // LANE
        for sl in range(nsl):
            for i in range(N1_STEP):
                tmp_ref[k * nsl + sl, i * ROW_PITCH:i * ROW_PITCH + n2, :] = (
                    val[i * n2:(i + 1) * n2, sl * LANE:(sl + 1) * LANE])
        for j in range(n2):
            for sl in range(nsl):
                lo = k * wd + sl * LANE
                cv_ref[j, :, lo:lo + LANE] = (
                    tmp_ref[k * nsl + sl, pl.ds(j, N1_STEP, stride=ROW_PITCH), :].astype(BF16))

    def project(cols):
        y = _mm(h_ext, w_ref[:, cols])
        return (y[:ROWS], y[ROWS:]) if latent else (y, None)

    def part(y_halo, lo):
        return None if y_halo is None else y_halo[:, lo:lo + wd]

    ya, ya_halo = project(col(0, 2))
    emit(0, dwconv3(ya[:, :wd], part(ya_halo, 0), 0))
    emit(1, dwconv3(ya[:, wd:], part(ya_halo, wd), 1))
    yb, yb_halo = project(col(2, 2))
    emit(2, dwconv3(yb[:, :wd], part(yb_halo, 0), 2) * _silu(yb[:, wd:]))
    yc, yc_halo = project(col(4, 2))
    yd, yd_halo = project(col(6, 2))
    inner_halo = part(yc_halo, wd) * part(yd_halo, 0) if latent else None
    inner = dwconv3(yc[:, wd:] * yd[:, :wd], inner_halo, 3)
    emit(3, yc[:, :wd] * inner * _silu(yd[:, wd:]))


def _in_proj(grp, x, mod, g, w, wconv, layer, rope_tabs=None, caches=None):
    B, L = grp.B, grp.L
    cvw = 4 * HYENA_W
    const2 = lambda *_: (0, 0)
    wspec = pl.BlockSpec((None, D_MODEL, IN_COLS), lambda *_: (layer, 0, 0))
    wcspec = pl.BlockSpec((3, 4, HYENA_W), lambda *_: (0, 0, 0))
    aliases = {}
    if grp.two_stage:
        n1h, n2 = grp.N1h, grp.N2
        assert ROWS == N1_STEP * n2
        nblk = L // ROWS
        grid = (B, nblk)
        nat = lambda wd: pl.BlockSpec((None, ROWS, wd), lambda b, j: (b, j, 0))
        nat_shape = lambda wd, dt: jax.ShapeDtypeStruct((B, L, wd), dt)
        x8 = x.reshape(B, L // 8, 8, D_MODEL)
        per8 = ROWS // 8
        in_specs = [nat(D_MODEL),
                    pl.BlockSpec((None, None, 8, D_MODEL), lambda b, j: (b, jnp.maximum(j * per8 - 1, 0), 0, 0)),
                    pl.BlockSpec((None, None, 8, D_MODEL),
                                 lambda b, j: (b, jnp.minimum((j + 1) * per8, L // 8 - 1), 0, 0)),
                    pl.BlockSpec((None, 1, 3 * D_MODEL), lambda b, j: (b, 0, 0)),
                    pl.BlockSpec((1, D_MODEL), const2),
                    wspec, wcspec,
                    pl.BlockSpec((ROWS, LANE), lambda b, j: (j, 0)),
                    pl.BlockSpec((ROWS, LANE), lambda b, j: (j, 0))]
        args = [x, x8, x8, mod, g, w, wconv, rope_tabs[0], rope_tabs[1]]
        out_specs = [nat(2 * ATTN_W), nat(2 * LANE), nat(KV_W), nat(ATTN_W),
                     pl.BlockSpec((None, n2, N1_STEP, cvw), lambda b, j: (b, 0, j, 0))]
        out_shape = [nat_shape(2 * ATTN_W, BF16), nat_shape(2 * LANE, BF16), nat_shape(KV_W, BF16),
                     nat_shape(ATTN_W, BF16), jax.ShapeDtypeStruct((B, n2, n1h, cvw), BF16)]
        scratch = [pltpu.VMEM((cvw // LANE, N1_STEP * ROW_PITCH, LANE), F32)]
        sem = ("arbitrary", "arbitrary")
    else:
        bb = ROWS // L
        grid = (B // bb,)
        nat = lambda wd: pl.BlockSpec((bb, L, wd), lambda i: (i, 0, 0))
        nat_shape = lambda wd, dt: jax.ShapeDtypeStruct((B, L, wd), dt)
        in_specs = [nat(D_MODEL),
                    pl.BlockSpec((None, 1, 3 * D_MODEL), lambda i: (0, 0, 0)),
                    pl.BlockSpec((1, D_MODEL), const2),
                    wspec, wcspec,
                    pl.BlockSpec(memory_space=pl.ANY),
                    pl.BlockSpec(memory_space=pl.ANY)]
        args = [x, mod, g, w, wconv, caches[0], caches[1]]
        aliases = {5: 5, 6: 6}
        cache_spec = pl.BlockSpec((bb, None, L, KV_W), lambda i: (i, layer, 0, 0))
        cache_shape = jax.ShapeDtypeStruct((B, DEPTH, L, KV_W), F32)
        out_specs = [nat(2 * ATTN_W), nat(2 * LANE), nat(KV_W), nat(ATTN_W), nat(cvw), cache_spec, cache_spec]
        out_shape = [nat_shape(2 * ATTN_W, BF16), nat_shape(2 * LANE, BF16), nat_shape(KV_W, BF16),
                     nat_shape(ATTN_W, BF16), nat_shape(cvw, BF16), cache_shape, cache_shape]
        scratch = []
        sem = ("arbitrary",)
    outs = pl.pallas_call(
        functools.partial(_in_proj_kernel, latent=grp.two_stage, seq=L),
        grid=grid, in_specs=in_specs, out_specs=out_specs, out_shape=out_shape,
        scratch_shapes=scratch, input_output_aliases=aliases, compiler_params=_params(*sem),
        name="in_proj_lat" if grp.two_stage else "in_proj_ctx",
    )(*args)
    outs = list(outs)
    if not grp.two_stage:
        outs[4] = outs[4].reshape(B, 1, L, cvw)
    return outs


def _window_bias():
    qi = np.arange(BLOCK)[:, None]
    kj = np.arange(3 * BLOCK)[None, :]
    return np.where(np.abs(kj - BLOCK - qi) <= WINDOW, 0.0, NEG_INF).astype(np.float32)


def _attn_kernel(*refs, lq, local, nblocks, qblocks):
    if local:
        sink_ref, bias_ref, qq_ref, kk_ref, vv_ref, ck_ref, cv_ref, ga_ref, o_ref = refs
    else:
        sink_ref, qq_ref, kk_ref, vv_ref, ga_ref, o_ref = refs
    group = N_HEADS // N_KV_HEADS
    rows = group * lq
    nt = (((1,), (1,)), ((), ()))
    lt64 = _lane_lt64((lq, LANE))

    for qb in range(qblocks):
        qrows = slice(qb * lq, (qb + 1) * lq)
        if local:
            n = pl.program_id(1) * qblocks + qb
            starts = [pl.multiple_of(jnp.maximum(n - 1, 0) * BLOCK, BLOCK),
                      pl.multiple_of(n * BLOCK, BLOCK),
                      pl.multiple_of(jnp.minimum(n + 1, nblocks - 1) * BLOCK, BLOCK)]
            kj = lax.broadcasted_iota(jnp.int32, (1, 3 * BLOCK), 1)
            edge = (jnp.where((kj < BLOCK) & (n == 0), NEG_INF, 0.0)
                    + jnp.where((kj >= 2 * BLOCK) & (n == nblocks - 1), NEG_INF, 0.0))
            bias = bias_ref[...] + edge
        for g in range(N_KV_HEADS):
            lhs = jnp.concatenate(
                [qq_ref[qrows, (group * g + i) * LANE:(group * g + i + 1) * LANE] for i in range(group)], axis=0)
            gl = slice(g * LANE, (g + 1) * LANE)
            if local:
                kwin = jnp.concatenate([kk_ref[pl.ds(s, BLOCK), gl] for s in starts], axis=0)
                vals = jnp.concatenate([vv_ref[pl.ds(s, BLOCK), :] for s in starts] + [cv_ref[...]], axis=0)
                s_loc = lax.dot_general(lhs, kwin, nt, preferred_element_type=F32)
                s_ctx = lax.dot_general(lhs, ck_ref[:, gl], nt, preferred_element_type=F32)
            else:
                vals = vv_ref[qrows, :]
                s_loc = lax.dot_general(lhs, kk_ref[qrows, gl], nt, preferred_element_type=F32)
            row_head = lax.broadcasted_iota(jnp.int32, (rows, 1), 0) // lq
            snk = jnp.zeros((rows, 1), F32)
            for i in range(group):
                snk = jnp.where(row_head == i, sink_ref[group * g + i], snk)
            if local:
                s_loc = s_loc + jnp.concatenate([bias] * group, axis=0)
            m = jnp.maximum(jnp.max(s_loc, axis=-1, keepdims=True), snk)
            if local:
                m = jnp.maximum(m, jnp.max(s_ctx, axis=-1, keepdims=True))
            ones = jnp.ones((vals.shape[0], LANE), BF16)
            vext = jnp.concatenate([vals, ones], axis=1)
            nk = s_loc.shape[1]
            o = _mm(jnp.exp(s_loc - m).astype(BF16), vext[:nk])
            if local:
                o = o + _mm(jnp.exp(s_ctx - m).astype(BF16), vext[nk:])
            o = o[:, :LANE] / (o[:, LANE:] + jnp.exp(snk - m))
            for jj in range(group // 2):
                a = o[(2 * jj) * lq:(2 * jj + 1) * lq]
                b = o[(2 * jj + 1) * lq:(2 * jj + 2) * lq]
                if g == 0:
                    tile = jnp.where(lt64, a, pltpu.roll(b, HEAD_DIM, axis=1))
                else:
                    tile = jnp.where(lt64, pltpu.roll(a, HEAD_DIM, axis=1), b)
                j = (group // 2) * g + jj
                cols = slice(j * LANE, (j + 1) * LANE)
                o_ref[qrows, cols] = (tile * ga_ref[qrows, cols].astype(F32)).astype(BF16)


def _attention(grp, sink, qq, kk, vv, ga, layer=0, ctx=None):
    B, L = grp.B, grp.L
    smem = pl.BlockSpec(memory_space=pltpu.SMEM)
    if ctx is not None:
        ck, cv = ctx
        nb = L // BLOCK
        lc = ck.shape[2]
        qrows = ATTN_QBLOCKS * BLOCK
        grid = (B, nb // ATTN_QBLOCKS)
        blk = lambda wd: pl.BlockSpec((None, qrows, wd), lambda b, n: (b, n, 0))
        full = lambda rows, wd: pl.BlockSpec((None, rows, wd), lambda b, n: (b, 0, 0))
        cache = lambda wd: pl.BlockSpec((None, None, lc, wd), lambda b, n: (b, layer, 0, 0))
        in_specs = [smem, pl.BlockSpec((BLOCK, 3 * BLOCK), lambda b, n: (0, 0)),
                    blk(2 * ATTN_W), full(L, 2 * LANE), full(L, KV_W), cache(2 * LANE), cache(KV_W),
                    blk(ATTN_W)]
        args = [sink, jnp.asarray(_window_bias()), qq, kk, vv, ck, cv, ga]
        out_spec = blk(ATTN_W)
        out_shape = jax.ShapeDtypeStruct((B, L, ATTN_W), BF16)
        kern = functools.partial(_attn_kernel, lq=BLOCK, local=True, nblocks=nb, qblocks=ATTN_QBLOCKS)
        sem = ("arbitrary", "arbitrary")
    else:
        grid = (B // CTX_SEQS,)
        flat = lambda a: a.reshape(B * L, a.shape[-1])
        blk = lambda wd: pl.BlockSpec((CTX_SEQS * L, wd), lambda i: (i, 0))
        in_specs = [smem, blk(2 * ATTN_W), blk(2 * LANE), blk(KV_W), blk(ATTN_W)]
        args = [sink, flat(qq), flat(kk), flat(vv), flat(ga)]
        out_spec = blk(ATTN_W)
        out_shape = jax.ShapeDtypeStruct((B * L, ATTN_W), BF16)
        kern = functools.partial(_attn_kernel, lq=L, local=False, nblocks=1, qblocks=CTX_SEQS)
        sem = ("arbitrary",)
    out = pl.pallas_call(
        kern, grid=grid, in_specs=in_specs, out_specs=out_spec, out_shape=out_shape,
        compiler_params=_params(*sem), name="attn_lat" if ctx is not None else "attn_ctx",
    )(*args)
    return out.reshape(B, L, ATTN_W)


def _filter_kernel(t_ref, f_ref, w1_ref, b1_ref, w2_ref, b2_ref, w3_ref, fq_ref, dl_ref,
                   kf_ref, kb_ref, *, chunk):
    rows = t_ref.shape[0]
    width = 2 * HYENA_W
    fq = fq_ref[...]
    nn = (((0,), (0,)), ((), ()))

    def body(i, acc):
        r0 = pl.multiple_of(i * chunk, chunk)
        h = jnp.sin(fq * (_mm_hi(w1_ref[...], f_ref[:, pl.ds(r0, chunk)]) + b1_ref[...]))
        h = jnp.sin(fq * (_mm_hi(w2_ref[...], h) + b2_ref[...]))
        dec = jnp.exp(-(t_ref[pl.ds(r0, chunk), :] * dl_ref[...]))
        dec = jnp.concatenate([dec, dec], axis=1)
        hi = lax.Precision.HIGHEST
        kf = lax.dot_general(h, w3_ref[:, 0:width], nn, preferred_element_type=F32, precision=hi) * dec
        kb = lax.dot_general(h, w3_ref[:, width:2 * width], nn, preferred_element_type=F32, precision=hi) * dec
        rid = lax.broadcasted_iota(jnp.int32, kb.shape, 0) + r0
        kb = jnp.where(rid == 0, 0.0, kb)
        kf_ref[pl.ds(r0, chunk), :] = kf
        kb_ref[pl.ds(r0, chunk), :] = kb
        return acc + jnp.sum(jnp.abs(kf) + jnp.abs(kb), axis=0, keepdims=True)

    total = lax.fori_loop(0, rows // chunk, body, jnp.zeros((1, width), F32))

    def scale(i, c):
        r0 = pl.multiple_of(i * chunk, chunk)
        kf_ref[pl.ds(r0, chunk), :] = kf_ref[pl.ds(r0, chunk), :] / total
        kb_ref[pl.ds(r0, chunk), :] = kb_ref[pl.ds(r0, chunk), :] / total
        return c

    lax.fori_loop(0, rows // chunk, scale, 0)


def _filters(grp, feats, w1, b1, w2, b2, w3, fq, deltas):
    L = grp.L
    chunk = min(512, L)
    c2 = lambda l: (0, 0)
    per = lambda a, b: pl.BlockSpec((None, a, b), lambda l: (l, 0, 0))
    shape = jax.ShapeDtypeStruct((DEPTH, L, 2 * HYENA_W), F32)
    return pl.pallas_call(
        functools.partial(_filter_kernel, chunk=chunk), grid=(DEPTH,),
        in_specs=[pl.BlockSpec((L, 1), c2), pl.BlockSpec((LANE, L), c2),
                  per(FILTER_HIDDEN, LANE), per(FILTER_HIDDEN, 1), per(FILTER_HIDDEN, FILTER_HIDDEN),
                  per(FILTER_HIDDEN, 1), per(FILTER_HIDDEN, 4 * HYENA_W), per(FILTER_HIDDEN, 1),
                  pl.BlockSpec((1, HYENA_W), c2)],
        out_specs=[per(L, 2 * HYENA_W)] * 2, out_shape=[shape] * 2,
        compiler_params=_params("arbitrary"),
        name="filters_lat" if grp.two_stage else "filters_ctx",
    )(*feats, w1, b1, w2, b2, w3, fq, deltas)


def _store_spectrum_rows(s_ref, base, a, n1):
    for comp in range(2):
        for sl in range(2):
            s_ref[comp, sl, pl.ds(base, n1), :] = a[comp * n1:(comp + 1) * n1, sl * LANE:(sl + 1) * LANE]


def _load_column(s_ref, k1, n2, pitch):
    parts = [jnp.concatenate([s_ref[comp, sl, pl.ds(k1, n2, stride=pitch), :] for sl in range(2)], axis=1)
             for comp in range(2)]
    return jnp.concatenate(parts, axis=0).astype(BF16)


def _spectrum2_kernel(kf_ref, kb_ref, faf_ref, g_ref, o_ref, s_ref, *, n1, n2):
    n1h = n1 // 2
    pitch = n1 + PITCH_PAD
    faf = faf_ref[...]

    def stage_a(i, c):
        for t in range(UNROLL):
            j = i * UNROLL + t
            r0 = pl.multiple_of(j * n1h, n1h)
            rb = pl.multiple_of(((n2 - j) % n2) * n1h, n1h)
            back = kb_ref[pl.ds(rb, n1h), :]
            back = jnp.where(j == 0, pltpu.roll(back, n1h - 1, axis=0), back)
            rhs = jnp.concatenate([kf_ref[pl.ds(r0, n1h), :], back], axis=0).astype(BF16)
            _store_spectrum_rows(s_ref, pl.multiple_of(j * pitch, 8), _mm(faf, rhs), n1)
        return c

    lax.fori_loop(0, n2 // UNROLL, stage_a, 0)

    def stage_c(i, c):
        for t in range(UNROLL):
            k1 = i * UNROLL + t
            o_ref[k1] = _mm(g_ref[k1], _load_column(s_ref, k1, n2, pitch)).astype(BF16)
        return c

    lax.fori_loop(0, n1 // UNROLL, stage_c, 0)


def _spectrum1_kernel(kf_ref, kb_ref, faf_ref, o_ref):
    back = kb_ref[...]
    rhs = jnp.concatenate([kf_ref[...], pltpu.roll(back, back.shape[0] - 1, axis=0)], axis=0).astype(BF16)
    o_ref[...] = _mm(faf_ref[...], rhs).astype(BF16)


def _spectrum(grp, kf, kb, faf, g):
    n1, n2, L = grp.N1, grp.N2, grp.L
    wd = 2 * LANE
    nblk = 2 * HYENA_W // wd
    kin = pl.BlockSpec((None, L, wd), lambda l, c: (l, 0, c))
    if grp.two_stage:
        pitch = n1 + PITCH_PAD
        return pl.pallas_call(
            functools.partial(_spectrum2_kernel, n1=n1, n2=n2), grid=(DEPTH, nblk),
            in_specs=[kin, kin, pl.BlockSpec((2 * n1, n1), lambda l, c: (0, 0)),
                      pl.BlockSpec((n1, 2 * n2, 2 * n2), lambda l, c: (0, 0, 0))],
            out_specs=pl.BlockSpec((None, n1, 2 * n2, wd), lambda l, c: (l, 0, 0, c)),
            out_shape=jax.ShapeDtypeStruct((DEPTH, n1, 2 * n2, 2 * HYENA_W), BF16),
            scratch_shapes=[pltpu.VMEM((2, 2, n2 * pitch, LANE), F32)],
            compiler_params=_params("arbitrary", "arbitrary"), name="spectrum_lat",
        )(kf, kb, faf, g)
    return pl.pallas_call(
        _spectrum1_kernel, grid=(DEPTH, nblk),
        in_specs=[kin, kin, pl.BlockSpec((2 * n1, n1), lambda l, c: (0, 0))],
        out_specs=pl.BlockSpec((None, 2 * n1, wd), lambda l, c: (l, 0, c)),
        out_shape=jax.ShapeDtypeStruct((DEPTH, 2 * n1, 2 * HYENA_W), BF16),
        compiler_params=_params("arbitrary", "arbitrary"), name="spectrum_ctx",
    )(kf, kb, faf)


def _pair_operand(u_ref, j):
    nb = u_ref.shape[0]
    re = jnp.concatenate([u_ref[b, j] for b in range(0, nb, 2)], axis=1)
    im = jnp.concatenate([u_ref[b, j] for b in range(1, nb, 2)], axis=1)
    return jnp.concatenate([re, im], axis=0)


def _cmul(x, k, half):
    k = k.astype(F32)
    npair = x.shape[1] // LANE
    kr = jnp.concatenate([k[:half]] * npair, axis=1)
    ki = jnp.concatenate([k[half:]] * npair, axis=1)
    xr, xi = x[:half], x[half:]
    return jnp.concatenate([xr * kr - xi * ki, xr * ki + xi * kr], axis=0).astype(BF16)


def _conv_epilogue(u_ref, m_ref, o_ref, d, y, j, n1h):
    for b in range(u_ref.shape[0]):
        yb = y[(b % 2) * n1h:(b % 2 + 1) * n1h, (b // 2) * LANE:(b // 2 + 1) * LANE]
        u = u_ref[b, j].astype(F32)
        o_ref[b, j] = (m_ref[b, j].astype(F32) * (yb + u * d)).astype(BF16)


def _conv2_kernel(u_ref, m_ref, fa_ref, fat_ref, g_ref, k_ref, d_ref, o_ref, s_ref, *, n1, n2):
    n1h = n1 // 2
    pitch = n1 + PITCH_PAD
    fa = fa_ref[...]
    fat = fat_ref[...]
    d = d_ref[...]

    def stage_a(i, c):
        for t in range(UNROLL_WIDE):
            j = i * UNROLL_WIDE + t
            _store_spectrum_rows(s_ref, pl.multiple_of(j * pitch, 8), _mm(fa, _pair_operand(u_ref, j)), n1)
        return c

    lax.fori_loop(0, n2 // UNROLL_WIDE, stage_a, 0)

    def stage_c(i, c):
        ks = [i * UNROLL_WIDE + t for t in range(UNROLL_WIDE)]
        cols = [_load_column(s_ref, k1, n2, pitch) for k1 in ks]
        backs = []
        for k1, col in zip(ks, cols):
            g = g_ref[k1]
            y = _cmul(_mm(g, col), k_ref[k1], n2)
            backs.append(lax.dot_general(g, y, (((0,), (0,)), ((), ())), preferred_element_type=F32))
        for k1, back in zip(ks, backs):
            for comp in range(2):
                for sl in range(2):
                    s_ref[comp, sl, pl.ds(k1, n2, stride=pitch), :] = (
                        back[comp * n2:(comp + 1) * n2, sl * LANE:(sl + 1) * LANE])
        return c

    lax.fori_loop(0, n1 // UNROLL_WIDE, stage_c, 0)

    def stage_inv(i, c):
        for t in range(UNROLL):
            j = i * UNROLL + t
            base = pl.multiple_of(j * pitch, 8)
            parts = [jnp.concatenate([s_ref[comp, sl, pl.ds(base, n1), :] for sl in range(2)], axis=1)
                     for comp in range(2)]
            y = _mm(fat, jnp.concatenate(parts, axis=0).astype(BF16))
            _conv_epilogue(u_ref, m_ref, o_ref, d, y, j, n1h)
        return c

    lax.fori_loop(0, n2 // UNROLL, stage_inv, 0)


def _conv1_kernel(u_ref, m_ref, fa_ref, fat_ref, k_ref, d_ref, o_ref, *, n1):
    x = _mm(fa_ref[...], _pair_operand(u_ref, 0))
    y = _mm(fat_ref[...], _cmul(x, k_ref[...], n1))
    _conv_epilogue(u_ref, m_ref, o_ref, d_ref[...], y, 0, n1 // 2)


def _long_conv(grp, u, m, spec, d, layer, order, tabs):
    B, n1, n2, n1h = grp.B, grp.N1, grp.N2, grp.N1h
    ncb = HYENA_W // LANE
    nbatch = 4 if grp.two_stage else 8
    grid = (ncb, B // nbatch)
    group = lambda k: pl.BlockSpec((nbatch, n2, n1h, LANE), lambda c, q: (q, 0, 0, k * ncb + c))
    (u, ku), (m, km) = u, m
    uspec, mspec = group(ku), group(km)
    data = group(0)
    dspec = pl.BlockSpec((None, 1, LANE), lambda c, q: (layer, 0, order * ncb + c))
    fa = pl.BlockSpec((2 * n1, n1), lambda c, q: (0, 0))
    fat = pl.BlockSpec((n1, 2 * n1), lambda c, q: (0, 0))
    out_shape = jax.ShapeDtypeStruct((B, n2, n1h, HYENA_W), BF16)
    if grp.two_stage:
        pitch = n1 + PITCH_PAD
        return pl.pallas_call(
            functools.partial(_conv2_kernel, n1=n1, n2=n2), grid=grid,
            in_specs=[uspec, mspec, fa, fat,
                      pl.BlockSpec((n1, 2 * n2, 2 * n2), lambda c, q: (0, 0, 0)),
                      pl.BlockSpec((None, n1, 2 * n2, LANE), lambda c, q: (layer, 0, 0, order * ncb + c)),
                      dspec],
            out_specs=data, out_shape=out_shape,
            scratch_shapes=[pltpu.VMEM((2, 2, n2 * pitch, LANE), F32)],
            compiler_params=_params("arbitrary", "arbitrary"), name="long_conv_lat",
        )(u, m, tabs["fa"], tabs["fat"], tabs["g"], spec, d)
    return pl.pallas_call(
        functools.partial(_conv1_kernel, n1=n1), grid=grid,
        in_specs=[uspec, mspec, fa, fat,
                  pl.BlockSpec((None, 2 * n1, LANE), lambda c, q: (layer, 0, order * ncb + c)), dspec],
        out_specs=data, out_shape=out_shape,
        compiler_params=_params("arbitrary", "arbitrary"), name="long_conv_ctx",
    )(u, m, tabs["fa"], tabs["fat"], spec, d)


def _out_proj_kernel(*refs, permute, final):
    if permute:
        x_ref, mod_ref, a_ref, z_ref, s_ref, w_ref, fg_ref, o_ref, tmp_ref = refs
    else:
        x_ref, mod_ref, a_ref, z_ref, s_ref, w_ref, fg_ref, o_ref = refs
    a = a_ref[...].reshape(ROWS, ATTN_W)
    nsl = HYENA_W // LANE
    if permute:
        n2 = z_ref.shape[0]
        for j in range(n2):
            for sl in range(nsl):
                cols = slice(sl * LANE, (sl + 1) * LANE)
                tmp_ref[sl, pl.ds(j, N1_STEP, stride=ROW_PITCH), :] = z_ref[j, :, cols].astype(F32)
                tmp_ref[nsl + sl, pl.ds(j, N1_STEP, stride=ROW_PITCH), :] = s_ref[j, :, cols].astype(F32)
        conv = jnp.concatenate(
            [jnp.concatenate([tmp_ref[i, k * ROW_PITCH:k * ROW_PITCH + n2, :] for k in range(N1_STEP)], axis=0)
             for i in range(2 * nsl)], axis=1).astype(BF16)
    else:
        conv = jnp.concatenate([z_ref[...].reshape(ROWS, HYENA_W), s_ref[...].reshape(ROWS, CONV_W)], axis=1)
    y = _mm(a, w_ref[0:ATTN_W, :]) + _mm(conv, w_ref[ATTN_W:, :])
    gate = mod_ref[:, 2 * D_MODEL:3 * D_MODEL]
    xn = x_ref[...].reshape(ROWS, D_MODEL) + gate * y
    if final:
        ms = jnp.mean(xn * xn, axis=-1, keepdims=True)
        xn = xn * lax.rsqrt(ms + RMS_EPS) * fg_ref[...]
    o_ref[...] = xn.reshape(o_ref.shape)


def _out_proj(grp, x, mod, attn, zg, scg, w, layer, final_g, final):
    B, L = grp.B, grp.L
    const2 = lambda *_: (0, 0)
    wspec = pl.BlockSpec((None, D_MODEL, D_MODEL), lambda *_: (layer, 0, 0))
    (zg, kz), (scg, ks) = zg, scg
    if grp.two_stage:
        n2 = grp.N2
        grid = (B, L // ROWS)
        nat = lambda wd: pl.BlockSpec((None, ROWS, wd), lambda b, j: (b, j, 0))
        cvl = lambda k: pl.BlockSpec((None, n2, N1_STEP, HYENA_W), lambda b, j: (b, 0, j, k))
        in_specs = [nat(D_MODEL), pl.BlockSpec((None, 1, 3 * D_MODEL), lambda b, j: (b, 0, 0)),
                    nat(ATTN_W), cvl(kz), cvl(ks), wspec, pl.BlockSpec((1, D_MODEL), const2)]
        scratch = [pltpu.VMEM((2 * HYENA_W // LANE, N1_STEP * ROW_PITCH, LANE), F32)]
        sem = ("arbitrary", "arbitrary")
    else:
        bb = ROWS // L
        grid = (B // bb,)
        nat = lambda wd: pl.BlockSpec((bb, L, wd), lambda i: (i, 0, 0))
        cvl = lambda k: pl.BlockSpec((bb, None, L, HYENA_W), lambda i: (i, 0, 0, k))
        in_specs = [nat(D_MODEL), pl.BlockSpec((None, 1, 3 * D_MODEL), lambda i: (0, 0, 0)),
                    nat(ATTN_W), cvl(kz), cvl(ks), wspec, pl.BlockSpec((1, D_MODEL), const2)]
        scratch = []
        sem = ("arbitrary",)
    return pl.pallas_call(
        functools.partial(_out_proj_kernel, permute=grp.two_stage, final=final),
        grid=grid, in_specs=in_specs, out_specs=nat(D_MODEL),
        out_shape=jax.ShapeDtypeStruct((B, L, D_MODEL), F32),
        scratch_shapes=scratch, compiler_params=_params(*sem),
        name="out_proj_lat" if grp.two_stage else "out_proj_ctx",
    )(x, mod, attn, zg, scg, w, final_g)


def _group_tables(grp):
    fa, faf, g = _dft_tables(grp)
    tabs = {"fa": jnp.asarray(fa).astype(BF16), "fat": jnp.asarray(fa.T.copy()).astype(BF16),
            "faf": jnp.asarray(faf).astype(BF16)}
    if g is not None:
        tabs["g"] = jnp.asarray(g).astype(BF16)
    tabs["feats"] = tuple(jnp.asarray(a) for a in _filter_features(grp))
    return tabs


def kernel(x_prompt, x_sample, cache_k, cache_v, c, c_ctx, norm_g, mod_w, mod_b, w_in, attn_sink,
           hy_conv_w, hy_filt_w1, hy_filt_b1, hy_filt_w2, hy_filt_b2, hy_filt_w3, hy_filt_freq, hy_d,
           sc_conv_w, w_out, final_g):
    ctx = Group(x_prompt.shape[0], x_prompt.shape[1], 2 * x_prompt.shape[1], 1)
    lat = Group(x_sample.shape[0], x_sample.shape[1], 128, 2 * x_sample.shape[1] // 128)
    nlat = lat.B

    cond = jnp.zeros((16, D_MODEL), F32).at[0].set(c_ctx).at[1:1 + nlat].set(c)
    mods = _modulation(cond, mod_w, mod_b)

    w1t = jnp.pad(hy_filt_w1, ((0, 0), (0, LANE - FILTER_EMB), (0, 0))).transpose(0, 2, 1)
    w2t = hy_filt_w2.transpose(0, 2, 1)
    b1 = hy_filt_b1[:, :, None]
    b2 = hy_filt_b2[:, :, None]
    fq = hy_filt_freq[:, :, None]
    deltas = jnp.asarray(_decay_rates())
    tabs, spec = {}, {}
    for grp in (ctx, lat):
        t = _group_tables(grp)
        kf, kb = _filters(grp, t["feats"], w1t, b1, w2t, b2, hy_filt_w3, fq, deltas)
        spec[grp] = _spectrum(grp, kf, kb, t["faf"], t.get("g"))
        tabs[grp] = t
    rope = tuple(jnp.asarray(a) for a in _rope_tables(lat.L))

    w_in_b = w_in.astype(BF16)
    w_out_b = w_out.astype(BF16)
    wconv = jnp.concatenate([hy_conv_w.reshape(DEPTH, 3, 3, HYENA_W), sc_conv_w[:, :, None, :]], axis=2)
    dskip = hy_d.reshape(DEPTH, 1, 2 * HYENA_W)
    fg = final_g[None, :]
    lc = cache_k.shape[2]
    zpad = jnp.zeros((nlat, DEPTH, lc, HEAD_DIM), F32)
    ck_pad = jnp.concatenate([zpad, cache_k[:, :, :, 0], zpad, cache_k[:, :, :, 1]], axis=-1).astype(BF16)
    cv_nat = cache_v.reshape(nlat, DEPTH, lc, KV_W).astype(BF16)

    xp, xs = x_prompt, x_sample
    caches = (jnp.zeros((ctx.B, DEPTH, ctx.L, KV_W), F32), jnp.zeros((ctx.B, DEPTH, ctx.L, KV_W), F32))
    for l in range(DEPTH):
        g = norm_g[l][None, :]
        last = l == DEPTH - 1
        for grp in (ctx, lat):
            is_lat = grp is lat
            x = xs if is_lat else xp
            mod = mods[l, 1:1 + nlat][:, None, :] if is_lat else mods[l, 0:1][:, None, :]
            if is_lat:
                qq, kk, vv, ga, cv = _in_proj(grp, x, mod, g, w_in_b, wconv[l], l, rope_tabs=rope)
                attn = _attention(grp, attn_sink[l], qq, kk, vv, ga, l, ctx=(ck_pad, cv_nat))
            else:
                qq, kk, vv, ga, cv, *caches = _in_proj(grp, x, mod, g, w_in_b, wconv[l], l, caches=caches)
                attn = _attention(grp, attn_sink[l], qq, kk, vv, ga)
            z1 = _long_conv(grp, (cv, 0), (cv, 1), spec[grp], dskip, l, 0, tabs[grp])
            zg = _long_conv(grp, (z1, 0), (cv, 2), spec[grp], dskip, l, 1, tabs[grp])
            xn = _out_proj(grp, x, mod, attn, (zg, 0), (cv, 3), w_out_b, l, fg, last)
            if is_lat:
                xs = xn
            else:
                xp = xn
    shape = (ctx.B, DEPTH, ctx.L, N_KV_HEADS, HEAD_DIM)
    return (xp, xs, caches[0].reshape(shape), caches[1].reshape(shape))
```

```python
import functools
import math

import numpy as np
import jax
import jax.numpy as jnp
from jax import lax
from jax.experimental import pallas as pl
from jax.experimental.pallas import tpu as pltpu

F32 = jnp.float32
BF16 = jnp.bfloat16

D_MODEL = 1024
DEPTH = 4
GRID_W = 64
N_HEADS = 8
N_KV_HEADS = 2
HEAD_DIM = 64
ATTN_W = N_HEADS * HEAD_DIM
KV_W = N_KV_HEADS * HEAD_DIM
HYENA_W = 256
CONV_W = 256
WINDOW = 128
BLOCK = 128
FILTER_EMB = 33
FILTER_HIDDEN = 64
HYENA_TARGET = 1e-2
FAST_DECAY_PCT = 0.3
SLOW_DECAY_PCT = 1.5
ROPE_BASE = 10000.0
RMS_EPS = 1e-6
NEG_INF = -1e30
IN_COLS = 3328
C_Q, C_K, C_V, C_GA, C_CV = 0, 512, 640, 768, 1280
CV_W = 2048

LANE = 128
ROWS = 1024
N1_STEP = 16
ROW_PITCH = 72
VMEM_LIMIT = 56 * 1024 * 1024
PITCH_PAD = 8
UNROLL = 8
UNROLL_WIDE = 16
ATTN_QBLOCKS = 8
CTX_CONV_SEQS = 16
CTX_SEQS = 2


class Group:
    def __init__(self, batch, seq, n1, n2):
        self.B, self.L, self.N1, self.N2 = batch, seq, n1, n2
        self.N1h = n1 // 2
        assert self.N1h * n2 == seq
        self.two_stage = n2 > 1


def _mm(a, b):
    return jnp.dot(a, b, preferred_element_type=F32)


def _mm_hi(a, b):
    return jnp.dot(a, b, preferred_element_type=F32, precision=lax.Precision.HIGHEST)


def _silu(x):
    return x * (1.0 / (1.0 + jnp.exp(-x)))


def _params(*sem):
    return pltpu.CompilerParams(dimension_semantics=sem, vmem_limit_bytes=VMEM_LIMIT)


def _rope_tables(seq):
    t = np.arange(seq)
    n_freq = HEAD_DIM // 4
    inv = ROPE_BASE ** (-np.arange(n_freq, dtype=np.float64) / n_freq)
    row = (t // GRID_W)[:, None] * inv
    col = (t % GRID_W)[:, None] * inv
    cos = np.concatenate([np.cos(row), np.cos(row), np.cos(col), np.cos(col)], axis=1)
    sin = np.concatenate([-np.sin(row), np.sin(row), -np.sin(col), np.sin(col)], axis=1)
    return (np.tile(cos, (1, 2)).astype(np.float32), np.tile(sin, (1, 2)).astype(np.float32))


def _conv_order(grp, table):
    return table.reshape(grp.N1h, grp.N2, -1).transpose(1, 0, 2).reshape(grp.L, -1)


def _filter_features(grp):
    L = grp.L
    t = np.linspace(0.0, 1.0, L)[:, None]
    bands = (FILTER_EMB - 1) // 2
    ang = (2.0 * math.pi / L) * np.arange(L)[:, None]
    fr = np.linspace(1e-4, bands - 1, bands)[None, :]
    feats = np.concatenate([t, np.cos(fr * ang), -np.sin(fr * ang)], axis=-1)
    feats = np.pad(feats, ((0, 0), (0, LANE - FILTER_EMB)))
    feats = _conv_order(grp, feats).astype(np.float32)
    return feats[:, 0:1], np.ascontiguousarray(feats.T)


def _decay_rates():
    max_decay = math.log(HYENA_TARGET) / FAST_DECAY_PCT
    min_decay = math.log(HYENA_TARGET) / SLOW_DECAY_PCT
    return np.abs(np.linspace(min_decay, max_decay, HYENA_W))[None, :].astype(np.float32)


def _dft_tables(grp):
    n1, n2 = grp.N1, grp.N2
    n = n1 * n2
    k = np.arange(n1)
    f = np.exp(-2j * np.pi * ((k[:, None] * k[None, :]) % n1) / n1)
    fh = f[:, : grp.N1h]
    fa = np.block([[fh.real, -fh.imag], [fh.imag, fh.real]])
    frev = np.concatenate([f[:, : grp.N1h], f[:, grp.N1h:][:, ::-1]], axis=1)
    faf = np.concatenate([frev.real, frev.imag], axis=0) / n
    g = None
    if grp.two_stage:
        j = np.arange(n2)
        ph = (k[:, None, None] * j[None, None, :] + n1 * j[None, :, None] * j[None, None, :]) % n
        gc = np.exp(-2j * np.pi * ph / n)
        g = np.concatenate([np.concatenate([gc.real, -gc.imag], axis=2),
                            np.concatenate([gc.imag, gc.real], axis=2)], axis=1)
        g = g.astype(np.float32)
    return fa.astype(np.float32), faf.astype(np.float32), g


def _mod_kernel(c_ref, w_ref, b_ref, o_ref):
    o_ref[...] = _mm_hi(_silu(c_ref[...]), w_ref[...]) + b_ref[...]


def _modulation(cond, mod_w, mod_b):
    nb = 3 * D_MODEL // 1024
    return pl.pallas_call(
        _mod_kernel,
        grid=(DEPTH, nb),
        in_specs=[
            pl.BlockSpec((16, D_MODEL), lambda l, j: (0, 0)),
            pl.BlockSpec((None, D_MODEL, 1024), lambda l, j: (l, 0, j)),
            pl.BlockSpec((None, 1, 1024), lambda l, j: (l, 0, j)),
        ],
        out_specs=pl.BlockSpec((None, 16, 1024), lambda l, j: (l, 0, j)),
        out_shape=jax.ShapeDtypeStruct((DEPTH, 16, 3 * D_MODEL), F32),
        compiler_params=_params("arbitrary", "arbitrary"),
        name="modulation",
    )(cond, mod_w, mod_b.reshape(DEPTH, 1, 3 * D_MODEL))


def _lane_lt64(shape):
    return lax.broadcasted_iota(jnp.int32, shape, 1) < HEAD_DIM


def _rope(x, cos, sin):
    lane = lax.broadcasted_iota(jnp.int32, x.shape, 1)
    first = (lane % 32) < 16
    partner = jnp.where(first, pltpu.roll(x, LANE - 16, axis=1), pltpu.roll(x, 16, axis=1))
    return x * cos + partner * sin


def _in_proj_kernel(*refs, latent, seq):
    if latent:
        x_ref, xp_ref, xn_ref, mod_ref, g_ref, w_ref, wc_ref, cos_ref, sin_ref = refs[:9]
        qq_ref, kk_ref, vv_ref, ga_ref, cv_ref, tmp_ref = refs[9:]
        first_row = pl.program_id(1) * ROWS
    else:
        x_ref, mod_ref, g_ref, w_ref, wc_ref = refs[:5]
        qq_ref, kk_ref, vv_ref, ga_ref, cv_ref, kf_ref, vf_ref = refs[7:]
        first_row = 0
    rope = latent

    shift = mod_ref[:, 0:D_MODEL]
    scale = mod_ref[:, D_MODEL:2 * D_MODEL]

    def modulated_norm(x):
        ms = jnp.mean(x * x, axis=-1, keepdims=True)
        y = x * lax.rsqrt(ms + RMS_EPS) * g_ref[...]
        return (y * (1.0 + scale) + shift).astype(BF16)

    h = modulated_norm(x_ref[...].reshape(ROWS, D_MODEL))
    h_ext = (jnp.concatenate([h, modulated_norm(jnp.concatenate([xp_ref[...], xn_ref[...]], axis=0))], axis=0)
             if latent else h)

    if rope:
        cos = cos_ref[...]
        sin = sin_ref[...]
    lt64 = _lane_lt64((ROWS, LANE))

    def put(ref, lo, val):
        lead = ref.shape[:-1]
        width = val.shape[-1]
        ref[(slice(None),) * len(lead) + (slice(lo, lo + width),)] = val.reshape(lead + (width,))

    yq = _mm(h, w_ref[:, C_Q:C_Q + ATTN_W]) * (HEAD_DIM ** -0.5)
    for j in range(ATTN_W // LANE):
        xj = yq[:, j * LANE:(j + 1) * LANE]
        rj = _rope(xj, cos, sin) if rope else xj
        even = jnp.where(lt64, rj, pltpu.roll(xj, HEAD_DIM, axis=1))
        odd = jnp.where(lt64, pltpu.roll(rj, HEAD_DIM, axis=1), xj)
        put(qq_ref, (2 * j) * LANE, even.astype(BF16))
        put(qq_ref, (2 * j + 1) * LANE, odd.astype(BF16))

    ykv = _mm(h, w_ref[:, C_K:C_V + KV_W])
    yk, yv = ykv[:, :KV_W], ykv[:, KV_W:]
    rk = _rope(yk, cos, sin) if rope else yk
    put(kk_ref, 0, jnp.where(lt64, rk, 0.0).astype(BF16))
    put(kk_ref, LANE, jnp.where(lt64, pltpu.roll(rk, HEAD_DIM, axis=1), 0.0).astype(BF16))
    put(vv_ref, 0, yv.astype(BF16))
    if not latent:
        put(kf_ref, 0, yk)
        put(vf_ref, 0, yv)

    put(ga_ref, 0, _silu(_mm(h, w_ref[:, C_GA:C_GA + ATTN_W])).astype(BF16))

    wd = HYENA_W
    col = lambda k, n=1: slice(C_CV + k * wd, C_CV + (k + n) * wd)
    sub = lax.broadcasted_iota(jnp.int32, (8, 1), 0)

    def set_row(arr, r, value):
        g0 = r - r % 8
        fixed = jnp.where(sub == r % 8, value, arr[g0:g0 + 8])
        parts = ([arr[:g0]] if g0 else []) + [fixed] + ([arr[g0 + 8:]] if g0 + 8 < arr.shape[0] else [])
        return jnp.concatenate(parts, axis=0)

    def dwconv3(u, u_halo, k):
        prev = pltpu.roll(u, 1, axis=0)
        nxt = pltpu.roll(u, ROWS - 1, axis=0)
        if latent:
            prev = set_row(prev, 0, jnp.where(first_row == 0, 0.0, u_halo[7:8]))
            nxt = set_row(nxt, ROWS - 1, jnp.where(first_row == seq - ROWS, 0.0, u_halo[8:9]))
        else:
            for s in range(ROWS // seq):
                prev = set_row(prev, s * seq, 0.0)
                nxt = set_row(nxt, (s + 1) * seq - 1, 0.0)
        return prev * wc_ref[0:1, k, :] + u * wc_ref[1:2, k, :] + nxt * wc_ref[2:3, k, :]

    def emit(k, val):
        if not latent:
            put(cv_ref, k * wd, val.astype(BF16))
            return
        n2 = cv_ref.shape[0]
        nsl = wd // LANE
        for sl in range(nsl):
            for i in range(N1_STEP):
                tmp_ref[k * nsl + sl, i * ROW_PITCH:i * ROW_PITCH + n2, :] = (
                    val[i * n2:(i + 1) * n2, sl * LANE:(sl + 1) * LANE])
        for j in range(n2):
            for sl in range(nsl):
                lo = k * wd + sl * LANE
                cv_ref[j, :, lo:lo + LANE] = (
                    tmp_ref[k * nsl + sl, pl.ds(j, N1_STEP, stride=ROW_PITCH), :].astype(BF16))

    def project(cols):
        y = _mm(h_ext, w_ref[:, cols])
        return (y[:ROWS], y[ROWS:]) if latent else (y, None)

    def part(y_halo, lo):
        return None if y_halo is None else y_halo[:, lo:lo + wd]

    ya, ya_halo = project(col(0, 2))
    emit(0, dwconv3(ya[:, :wd], part(ya_halo, 0), 0))
    emit(1, dwconv3(ya[:, wd:], part(ya_halo, wd), 1))
    yb, yb_halo = project(col(2, 2))
    emit(2, dwconv3(yb[:, :wd], part(yb_halo, 0), 2) * _silu(yb[:, wd:]))
    yc, yc_halo = project(col(4, 2))
    yd, yd_halo = project(col(6, 2))
    inner_halo = part(yc_halo, wd) * part(yd_halo, 0) if latent else None
    inner = dwconv3(yc[:, wd:] * yd[:, :wd], inner_halo, 3)
    emit(3, yc[:, :wd] * inner * _silu(yd[:, wd:]))


def _in_proj(grp, x, mod, g, w, wconv, layer, rope_tabs=None, caches=None):
    B, L = grp.B, grp.L
    cvw = 4 * HYENA_W
    const2 = lambda *_: (0, 0)
    wspec = pl.BlockSpec((None, D_MODEL, IN_COLS), lambda *_: (layer, 0, 0))
    wcspec = pl.BlockSpec((3, 4, HYENA_W), lambda *_: (0, 0, 0))
    aliases = {}
    if grp.two_stage:
        n1h, n2 = grp.N1h, grp.N2
        assert ROWS == N1_STEP * n2
        nblk = L // ROWS
        grid = (B, nblk)
        nat = lambda wd: pl.BlockSpec((None, ROWS, wd), lambda b, j: (b, j, 0))
        nat_shape = lambda wd, dt: jax.ShapeDtypeStruct((B, L, wd), dt)
        x8 = x.reshape(B, L // 8, 8, D_MODEL)
        per8 = ROWS // 8
        in_specs = [nat(D_MODEL),
                    pl.BlockSpec((None, None, 8, D_MODEL), lambda b, j: (b, jnp.maximum(j * per8 - 1, 0), 0, 0)),
                    pl.BlockSpec((None, None, 8, D_MODEL),
                                 lambda b, j: (b, jnp.minimum((j + 1) * per8, L // 8 - 1), 0, 0)),
                    pl.BlockSpec((None, 1, 3 * D_MODEL), lambda b, j: (b, 0, 0)),
                    pl.BlockSpec((1, D_MODEL), const2),
                    wspec, wcspec,
                    pl.BlockSpec((ROWS, LANE), lambda b, j: (j, 0)),
                    pl.BlockSpec((ROWS, LANE), lambda b, j: (j, 0))]
        args = [x, x8, x8, mod, g, w, wconv, rope_tabs[0], rope_tabs[1]]
        out_specs = [nat(2 * ATTN_W), nat(2 * LANE), nat(KV_W), nat(ATTN_W),
                     pl.BlockSpec((None, n2, N1_STEP, cvw), lambda b, j: (b, 0, j, 0))]
        out_shape = [nat_shape(2 * ATTN_W, BF16), nat_shape(2 * LANE, BF16), nat_shape(KV_W, BF16),
                     nat_shape(ATTN_W, BF16), jax.ShapeDtypeStruct((B, n2, n1h, cvw), BF16)]
        scratch = [pltpu.VMEM((cvw // LANE, N1_STEP * ROW_PITCH, LANE), F32)]
        sem = ("arbitrary", "arbitrary")
    else:
        bb = ROWS // L
        grid = (B // bb,)
        nat = lambda wd: pl.BlockSpec((bb, L, wd), lambda i: (i, 0, 0))
        nat_shape = lambda wd, dt: jax.ShapeDtypeStruct((B, L, wd), dt)
        in_specs = [nat(D_MODEL),
                    pl.BlockSpec((None, 1, 3 * D_MODEL), lambda i: (0, 0, 0)),
                    pl.BlockSpec((1, D_MODEL), const2),
                    wspec, wcspec,
                    pl.BlockSpec(memory_space=pl.ANY),
                    pl.BlockSpec(memory_space=pl.ANY)]
        args = [x, mod, g, w, wconv, caches[0], caches[1]]
        aliases = {5: 5, 6: 6}
        cache_spec = pl.BlockSpec((bb, None, L, KV_W), lambda i: (i, layer, 0, 0))
        cache_shape = jax.ShapeDtypeStruct((B, DEPTH, L, KV_W), F32)
        out_specs = [nat(2 * ATTN_W), nat(2 * LANE), nat(KV_W), nat(ATTN_W), nat(cvw), cache_spec, cache_spec]
        out_shape = [nat_shape(2 * ATTN_W, BF16), nat_shape(2 * LANE, BF16), nat_shape(KV_W, BF16),
                     nat_shape(ATTN_W, BF16), nat_shape(cvw, BF16), cache_shape, cache_shape]
        scratch = []
        sem = ("arbitrary",)
    outs = pl.pallas_call(
        functools.partial(_in_proj_kernel, latent=grp.two_stage, seq=L),
        grid=grid, in_specs=in_specs, out_specs=out_specs, out_shape=out_shape,
        scratch_shapes=scratch, input_output_aliases=aliases, compiler_params=_params(*sem),
        name="in_proj_lat" if grp.two_stage else "in_proj_ctx",
    )(*args)
    outs = list(outs)
    if not grp.two_stage:
        outs[4] = outs[4].reshape(B, 1, L, cvw)
    return outs


def _window_bias():
    qi = np.arange(BLOCK)[:, None]
    kj = np.arange(3 * BLOCK)[None, :]
    return np.where(np.abs(kj - BLOCK - qi) <= WINDOW, 0.0, NEG_INF).astype(np.float32)


def _attn_kernel(*refs, lq, local, nblocks, qblocks):
    if local:
        sink_ref, bias_ref, qq_ref, kk_ref, vv_ref, ck_ref, cv_ref, ga_ref, o_ref = refs
    else:
        sink_ref, qq_ref, kk_ref, vv_ref, ga_ref, o_ref = refs
    group = N_HEADS // N_KV_HEADS
    rows = group * lq
    nt = (((1,), (1,)), ((), ()))
    lt64 = _lane_lt64((lq, LANE))

    for qb in range(qblocks):
        qrows = slice(qb * lq, (qb + 1) * lq)
        if local:
            n = pl.program_id(1) * qblocks + qb
            starts = [pl.multiple_of(jnp.maximum(n - 1, 0) * BLOCK, BLOCK),
                      pl.multiple_of(n * BLOCK, BLOCK),
                      pl.multiple_of(jnp.minimum(n + 1, nblocks - 1) * BLOCK, BLOCK)]
            kj = lax.broadcasted_iota(jnp.int32, (1, 3 * BLOCK), 1)
            edge = (jnp.where((kj < BLOCK) & (n == 0), NEG_INF, 0.0)
                    + jnp.where((kj >= 2 * BLOCK) & (n == nblocks - 1), NEG_INF, 0.0))
            bias = bias_ref[...] + edge
        for g in range(N_KV_HEADS):
            lhs = jnp.concatenate(
                [qq_ref[qrows, (group * g + i) * LANE:(group * g + i + 1) * LANE] for i in range(group)], axis=0)
            gl = slice(g * LANE, (g + 1) * LANE)
            if local:
                kwin = jnp.concatenate([kk_ref[pl.ds(s, BLOCK), gl] for s in starts], axis=0)
                vals = jnp.concatenate([vv_ref[pl.ds(s, BLOCK), :] for s in starts] + [cv_ref[...]], axis=0)
                s_loc = lax.dot_general(lhs, kwin, nt, preferred_element_type=F32)
                s_ctx = lax.dot_general(lhs, ck_ref[:, gl], nt, preferred_element_type=F32)
            else:
                vals = vv_ref[qrows, :]
                s_loc = lax.dot_general(lhs, kk_ref[qrows, gl], nt, preferred_element_type=F32)
            row_head = lax.broadcasted_iota(jnp.int32, (rows, 1), 0) // lq
            snk = jnp.zeros((rows, 1), F32)
            for i in range(group):
                snk = jnp.where(row_head == i, sink_ref[group * g + i], snk)
            if local:
                s_loc = s_loc + jnp.concatenate([bias] * group, axis=0)
            m = jnp.maximum(jnp.max(s_loc, axis=-1, keepdims=True), snk)
            if local:
                m = jnp.maximum(m, jnp.max(s_ctx, axis=-1, keepdims=True))
            ones = jnp.ones((vals.shape[0], LANE), BF16)
            vext = jnp.concatenate([vals, ones], axis=1)
            nk = s_loc.shape[1]
            o = _mm(jnp.exp(s_loc - m).astype(BF16), vext[:nk])
            if local:
                o = o + _mm(jnp.exp(s_ctx - m).astype(BF16), vext[nk:])
            o = o[:, :LANE] / (o[:, LANE:] + jnp.exp(snk - m))
            for jj in range(group // 2):
                a = o[(2 * jj) * lq:(2 * jj + 1) * lq]
                b = o[(2 * jj + 1) * lq:(2 * jj + 2) * lq]
                if g == 0:
                    tile = jnp.where(lt64, a, pltpu.roll(b, HEAD_DIM, axis=1))
                else:
                    tile = jnp.where(lt64, pltpu.roll(a, HEAD_DIM, axis=1), b)
                j = (group // 2) * g + jj
                cols = slice(j * LANE, (j + 1) * LANE)
                o_ref[qrows, cols] = (tile * ga_ref[qrows, cols].astype(F32)).astype(BF16)


def _attention(grp, sink, qq, kk, vv, ga, layer=0, ctx=None):
    B, L = grp.B, grp.L
    smem = pl.BlockSpec(memory_space=pltpu.SMEM)
    if ctx is not None:
        ck, cv = ctx
        nb = L // BLOCK
        lc = ck.shape[2]
        qrows = ATTN_QBLOCKS * BLOCK
        grid = (B, nb // ATTN_QBLOCKS)
        blk = lambda wd: pl.BlockSpec((None, qrows, wd), lambda b, n: (b, n, 0))
        full = lambda rows, wd: pl.BlockSpec((None, rows, wd), lambda b, n: (b, 0, 0))
        cache = lambda wd: pl.BlockSpec((None, None, lc, wd), lambda b, n: (b, layer, 0, 0))
        in_specs = [smem, pl.BlockSpec((BLOCK, 3 * BLOCK), lambda b, n: (0, 0)),
                    blk(2 * ATTN_W), full(L, 2 * LANE), full(L, KV_W), cache(2 * LANE), cache(KV_W),
                    blk(ATTN_W)]
        args = [sink, jnp.asarray(_window_bias()), qq, kk, vv, ck, cv, ga]
        out_spec = blk(ATTN_W)
        out_shape = jax.ShapeDtypeStruct((B, L, ATTN_W), BF16)
        kern = functools.partial(_attn_kernel, lq=BLOCK, local=True, nblocks=nb, qblocks=ATTN_QBLOCKS)
        sem = ("arbitrary", "arbitrary")
    else:
        grid = (B // CTX_SEQS,)
        flat = lambda a: a.reshape(B * L, a.shape[-1])
        blk = lambda wd: pl.BlockSpec((CTX_SEQS * L, wd), lambda i: (i, 0))
        in_specs = [smem, blk(2 * ATTN_W), blk(2 * LANE), blk(KV_W), blk(ATTN_W)]
        args = [sink, flat(qq), flat(kk), flat(vv), flat(ga)]
        out_spec = blk(ATTN_W)
        out_shape = jax.ShapeDtypeStruct((B * L, ATTN_W), BF16)
        kern = functools.partial(_attn_kernel, lq=L, local=False, nblocks=1, qblocks=CTX_SEQS)
        sem = ("arbitrary",)
    out = pl.pallas_call(
        kern, grid=grid, in_specs=in_specs, out_specs=out_spec, out_shape=out_shape,
        compiler_params=_params(*sem), name="attn_lat" if ctx is not None else "attn_ctx",
    )(*args)
    return out.reshape(B, L, ATTN_W)


def _filter_kernel(t_ref, f_ref, w1_ref, b1_ref, w2_ref, b2_ref, w3_ref, fq_ref, dl_ref,
                   kf_ref, kb_ref, *, chunk):
    rows = t_ref.shape[0]
    width = 2 * HYENA_W
    fq = fq_ref[...]
    nn = (((0,), (0,)), ((), ()))

    def body(i, acc):
        r0 = pl.multiple_of(i * chunk, chunk)
        h = jnp.sin(fq * (_mm_hi(w1_ref[...], f_ref[:, pl.ds(r0, chunk)]) + b1_ref[...]))
        h = jnp.sin(fq * (_mm_hi(w2_ref[...], h) + b2_ref[...]))
        dec = jnp.exp(-(t_ref[pl.ds(r0, chunk), :] * dl_ref[...]))
        dec = jnp.concatenate([dec, dec], axis=1)
        hi = lax.Precision.HIGHEST
        kf = lax.dot_general(h, w3_ref[:, 0:width], nn, preferred_element_type=F32, precision=hi) * dec
        kb = lax.dot_general(h, w3_ref[:, width:2 * width], nn, preferred_element_type=F32, precision=hi) * dec
        rid = lax.broadcasted_iota(jnp.int32, kb.shape, 0) + r0
        kb = jnp.where(rid == 0, 0.0, kb)
        kf_ref[pl.ds(r0, chunk), :] = kf
        kb_ref[pl.ds(r0, chunk), :] = kb
        return acc + jnp.sum(jnp.abs(kf) + jnp.abs(kb), axis=0, keepdims=True)

    total = lax.fori_loop(0, rows // chunk, body, jnp.zeros((1, width), F32))

    def scale(i, c):
        r0 = pl.multiple_of(i * chunk, chunk)
        kf_ref[pl.ds(r0, chunk), :] = kf_ref[pl.ds(r0, chunk), :] / total
        kb_ref[pl.ds(r0, chunk), :] = kb_ref[pl.ds(r0, chunk), :] / total
        return c

    lax.fori_loop(0, rows // chunk, scale, 0)


def _filters(grp, feats, w1, b1, w2, b2, w3, fq, deltas):
    L = grp.L
    chunk = min(512, L)
    c2 = lambda l: (0, 0)
    per = lambda a, b: pl.BlockSpec((None, a, b), lambda l: (l, 0, 0))
    shape = jax.ShapeDtypeStruct((DEPTH, L, 2 * HYENA_W), F32)
    return pl.pallas_call(
        functools.partial(_filter_kernel, chunk=chunk), grid=(DEPTH,),
        in_specs=[pl.BlockSpec((L, 1), c2), pl.BlockSpec((LANE, L), c2),
                  per(FILTER_HIDDEN, LANE), per(FILTER_HIDDEN, 1), per(FILTER_HIDDEN, FILTER_HIDDEN),
                  per(FILTER_HIDDEN, 1), per(FILTER_HIDDEN, 4 * HYENA_W), per(FILTER_HIDDEN, 1),
                  pl.BlockSpec((1, HYENA_W), c2)],
        out_specs=[per(L, 2 * HYENA_W)] * 2, out_shape=[shape] * 2,
        compiler_params=_params("arbitrary"),
        name="filters_lat" if grp.two_stage else "filters_ctx",
    )(*feats, w1, b1, w2, b2, w3, fq, deltas)


def _store_spectrum_rows(s_ref, base, a, n1):
    for comp in range(2):
        for sl in range(2):
            s_ref[comp, sl, pl.ds(base, n1), :] = a[comp * n1:(comp + 1) * n1, sl * LANE:(sl + 1) * LANE]


def _load_column(s_ref, k1, n2, pitch):
    parts = [jnp.concatenate([s_ref[comp, sl, pl.ds(k1, n2, stride=pitch), :] for sl in range(2)], axis=1)
             for comp in range(2)]
    return jnp.concatenate(parts, axis=0).astype(BF16)


def _spectrum2_kernel(kf_ref, kb_ref, faf_ref, g_ref, o_ref, s_ref, *, n1, n2):
    n1h = n1 // 2
    pitch = n1 + PITCH_PAD
    faf = faf_ref[...]

    def stage_a(i, c):
        for t in range(UNROLL):
            j = i * UNROLL + t
            r0 = pl.multiple_of(j * n1h, n1h)
            rb = pl.multiple_of(((n2 - j) % n2) * n1h, n1h)
            back = kb_ref[pl.ds(rb, n1h), :]
            back = jnp.where(j == 0, pltpu.roll(back, n1h - 1, axis=0), back)
            rhs = jnp.concatenate([kf_ref[pl.ds(r0, n1h), :], back], axis=0).astype(BF16)
            _store_spectrum_rows(s_ref, pl.multiple_of(j * pitch, 8), _mm(faf, rhs), n1)
        return c

    lax.fori_loop(0, n2 // UNROLL, stage_a, 0)

    def stage_c(i, c):
        for t in range(UNROLL_WIDE):
            k1 = i * UNROLL_WIDE + t
            o_ref[k1] = _mm(g_ref[k1], _load_column(s_ref, k1, n2, pitch)).astype(BF16)
        return c

    lax.fori_loop(0, n1 // UNROLL_WIDE, stage_c, 0)


def _spectrum1_kernel(kf_ref, kb_ref, faf_ref, o_ref):
    back = kb_ref[...]
    rhs = jnp.concatenate([kf_ref[...], pltpu.roll(back, back.shape[0] - 1, axis=0)], axis=0).astype(BF16)
    o_ref[...] = _mm(faf_ref[...], rhs).astype(BF16)


def _spectrum(grp, kf, kb, faf, g):
    n1, n2, L = grp.N1, grp.N2, grp.L
    wd = 2 * LANE
    nblk = 2 * HYENA_W // wd
    kin = pl.BlockSpec((None, L, wd), lambda l, c: (l, 0, c))
    if grp.two_stage:
        pitch = n1 + PITCH_PAD
        return pl.pallas_call(
            functools.partial(_spectrum2_kernel, n1=n1, n2=n2), grid=(DEPTH, nblk),
            in_specs=[kin, kin, pl.BlockSpec((2 * n1, n1), lambda l, c: (0, 0)),
                      pl.BlockSpec((n1, 2 * n2, 2 * n2), lambda l, c: (0, 0, 0))],
            out_specs=pl.BlockSpec((None, n1, 2 * n2, wd), lambda l, c: (l, 0, 0, c)),
            out_shape=jax.ShapeDtypeStruct((DEPTH, n1, 2 * n2, 2 * HYENA_W), BF16),
            scratch_shapes=[pltpu.VMEM((2, 2, n2 * pitch, LANE), F32)],
            compiler_params=_params("arbitrary", "arbitrary"), name="spectrum_lat",
        )(kf, kb, faf, g)
    return pl.pallas_call(
        _spectrum1_kernel, grid=(DEPTH, nblk),
        in_specs=[kin, kin, pl.BlockSpec((2 * n1, n1), lambda l, c: (0, 0))],
        out_specs=pl.BlockSpec((None, 2 * n1, wd), lambda l, c: (l, 0, c)),
        out_shape=jax.ShapeDtypeStruct((DEPTH, 2 * n1, 2 * HYENA_W), BF16),
        compiler_params=_params("arbitrary", "arbitrary"), name="spectrum_ctx",
    )(kf, kb, faf)


def _pair_operand(u_ref, j):
    nb = u_ref.shape[0]
    re = jnp.concatenate([u_ref[b, j] for b in range(0, nb, 2)], axis=1)
    im = jnp.concatenate([u_ref[b, j] for b in range(1, nb, 2)], axis=1)
    return jnp.concatenate([re, im], axis=0)


def _cmul(x, k, half):
    k = k.astype(F32)
    npair = x.shape[1] // LANE
    kr = jnp.concatenate([k[:half]] * npair, axis=1)
    ki = jnp.concatenate([k[half:]] * npair, axis=1)
    xr, xi = x[:half], x[half:]
    return jnp.concatenate([xr * kr - xi * ki, xr * ki + xi * kr], axis=0).astype(BF16)


def _conv_epilogue(u_ref, m_ref, o_ref, d, y, j, n1h):
    for b in range(u_ref.shape[0]):
        yb = y[(b % 2) * n1h:(b % 2 + 1) * n1h, (b // 2) * LANE:(b // 2 + 1) * LANE]
        u = u_ref[b, j].astype(F32)
        o_ref[b, j] = (m_ref[b, j].astype(F32) * (yb + u * d)).astype(BF16)


def _conv2_kernel(u_ref, m_ref, fa_ref, fat_ref, g_ref, k_ref, d_ref, o_ref, s_ref, *, n1, n2):
    n1h = n1 // 2
    pitch = n1 + PITCH_PAD
    fa = fa_ref[...]
    fat = fat_ref[...]
    d = d_ref[...]

    def stage_a(i, c):
        for t in range(UNROLL_WIDE):
            j = i * UNROLL_WIDE + t
            _store_spectrum_rows(s_ref, pl.multiple_of(j * pitch, 8), _mm(fa, _pair_operand(u_ref, j)), n1)
        return c

    lax.fori_loop(0, n2 // UNROLL_WIDE, stage_a, 0)

    def stage_c(i, c):
        ks = [i * UNROLL_WIDE + t for t in range(UNROLL_WIDE)]
        cols = [_load_column(s_ref, k1, n2, pitch) for k1 in ks]
        backs = []
        for k1, col in zip(ks, cols):
            g = g_ref[k1]
            y = _cmul(_mm(g, col), k_ref[k1], n2)
            backs.append(lax.dot_general(g, y, (((0,), (0,)), ((), ())), preferred_element_type=F32))
        for k1, back in zip(ks, backs):
            for comp in range(2):
                for sl in range(2):
                    s_ref[comp, sl, pl.ds(k1, n2, stride=pitch), :] = (
                        back[comp * n2:(comp + 1) * n2, sl * LANE:(sl + 1) * LANE])
        return c

    lax.fori_loop(0, n1 // UNROLL_WIDE, stage_c, 0)

    def stage_inv(i, c):
        for t in range(UNROLL):
            j = i * UNROLL + t
            base = pl.multiple_of(j * pitch, 8)
            parts = [jnp.concatenate([s_ref[comp, sl, pl.ds(base, n1), :] for sl in range(2)], axis=1)
                     for comp in range(2)]
            y = _mm(fat, jnp.concatenate(parts, axis=0).astype(BF16))
            _conv_epilogue(u_ref, m_ref, o_ref, d, y, j, n1h)
        return c

    lax.fori_loop(0, n2 // UNROLL, stage_inv, 0)


def _conv1_kernel(u_ref, m_ref, fa_ref, fat_ref, k_ref, d_ref, o_ref, *, n1):
    x = _mm(fa_ref[...], _pair_operand(u_ref, 0))
    y = _mm(fat_ref[...], _cmul(x, k_ref[...], n1))
    _conv_epilogue(u_ref, m_ref, o_ref, d_ref[...], y, 0, n1 // 2)


def _long_conv(grp, u, m, spec, d, layer, order, tabs):
    B, n1, n2, n1h = grp.B, grp.N1, grp.N2, grp.N1h
    ncb = HYENA_W // LANE
    nbatch = 4 if grp.two_stage else CTX_CONV_SEQS
    grid = (ncb, B // nbatch)
    group = lambda k: pl.BlockSpec((nbatch, n2, n1h, LANE), lambda c, q: (q, 0, 0, k * ncb + c))
    (u, ku), (m, km) = u, m
    uspec, mspec = group(ku), group(km)
    data = group(0)
    dspec = pl.BlockSpec((None, 1, LANE), lambda c, q: (layer, 0, order * ncb + c))
    fa = pl.BlockSpec((2 * n1, n1), lambda c, q: (0, 0))
    fat = pl.BlockSpec((n1, 2 * n1), lambda c, q: (0, 0))
    out_shape = jax.ShapeDtypeStruct((B, n2, n1h, HYENA_W), BF16)
    if grp.two_stage:
        pitch = n1 + PITCH_PAD
        return pl.pallas_call(
            functools.partial(_conv2_kernel, n1=n1, n2=n2), grid=grid,
            in_specs=[uspec, mspec, fa, fat,
                      pl.BlockSpec((n1, 2 * n2, 2 * n2), lambda c, q: (0, 0, 0)),
                      pl.BlockSpec((None, n1, 2 * n2, LANE), lambda c, q: (layer, 0, 0, order * ncb + c)),
                      dspec],
            out_specs=data, out_shape=out_shape,
            scratch_shapes=[pltpu.VMEM((2, 2, n2 * pitch, LANE), F32)],
            compiler_params=_params("arbitrary", "arbitrary"), name="long_conv_lat",
        )(u, m, tabs["fa"], tabs["fat"], tabs["g"], spec, d)
    return pl.pallas_call(
        functools.partial(_conv1_kernel, n1=n1), grid=grid,
        in_specs=[uspec, mspec, fa, fat,
                  pl.BlockSpec((None, 2 * n1, LANE), lambda c, q: (layer, 0, order * ncb + c)), dspec],
        out_specs=data, out_shape=out_shape,
        compiler_params=_params("arbitrary", "arbitrary"), name="long_conv_ctx",
    )(u, m, tabs["fa"], tabs["fat"], spec, d)


def _out_proj_kernel(*refs, permute, final):
    if permute:
        x_ref, mod_ref, a_ref, z_ref, s_ref, w_ref, fg_ref, o_ref, tmp_ref = refs
    else:
        x_ref, mod_ref, a_ref, z_ref, s_ref, w_ref, fg_ref, o_ref = refs
    a = a_ref[...].reshape(ROWS, ATTN_W)
    nsl = HYENA_W // LANE
    if permute:
        n2 = z_ref.shape[0]
        for j in range(n2):
            for sl in range(nsl):
                cols = slice(sl * LANE, (sl + 1) * LANE)
                tmp_ref[sl, pl.ds(j, N1_STEP, stride=ROW_PITCH), :] = z_ref[j, :, cols].astype(F32)
                tmp_ref[nsl + sl, pl.ds(j, N1_STEP, stride=ROW_PITCH), :] = s_ref[j, :, cols].astype(F32)
        conv = jnp.concatenate(
            [jnp.concatenate([tmp_ref[i, k * ROW_PITCH:k * ROW_PITCH + n2, :] for k in range(N1_STEP)], axis=0)
             for i in range(2 * nsl)], axis=1).astype(BF16)
    else:
        conv = jnp.concatenate([z_ref[...].reshape(ROWS, HYENA_W), s_ref[...].reshape(ROWS, CONV_W)], axis=1)
    y = _mm(a, w_ref[0:ATTN_W, :]) + _mm(conv, w_ref[ATTN_W:, :])
    gate = mod_ref[:, 2 * D_MODEL:3 * D_MODEL]
    xn = x_ref[...].reshape(ROWS, D_MODEL) + gate * y
    if final:
        ms = jnp.mean(xn * xn, axis=-1, keepdims=True)
        xn = xn * lax.rsqrt(ms + RMS_EPS) * fg_ref[...]
    o_ref[...] = xn.reshape(o_ref.shape)


def _out_proj(grp, x, mod, attn, zg, scg, w, layer, final_g, final):
    B, L = grp.B, grp.L
    const2 = lambda *_: (0, 0)
    wspec = pl.BlockSpec((None, D_MODEL, D_MODEL), lambda *_: (layer, 0, 0))
    (zg, kz), (scg, ks) = zg, scg
    if grp.two_stage:
        n2 = grp.N2
        grid = (B, L // ROWS)
        nat = lambda wd: pl.BlockSpec((None, ROWS, wd), lambda b, j: (b, j, 0))
        cvl = lambda k: pl.BlockSpec((None, n2, N1_STEP, HYENA_W), lambda b, j: (b, 0, j, k))
        in_specs = [nat(D_MODEL), pl.BlockSpec((None, 1, 3 * D_MODEL), lambda b, j: (b, 0, 0)),
                    nat(ATTN_W), cvl(kz), cvl(ks), wspec, pl.BlockSpec((1, D_MODEL), const2)]
        scratch = [pltpu.VMEM((2 * HYENA_W // LANE, N1_STEP * ROW_PITCH, LANE), F32)]
        sem = ("arbitrary", "arbitrary")
    else:
        bb = ROWS // L
        grid = (B // bb,)
        nat = lambda wd: pl.BlockSpec((bb, L, wd), lambda i: (i, 0, 0))
        cvl = lambda k: pl.BlockSpec((bb, None, L, HYENA_W), lambda i: (i, 0, 0, k))
        in_specs = [nat(D_MODEL), pl.BlockSpec((None, 1, 3 * D_MODEL), lambda i: (0, 0, 0)),
                    nat(ATTN_W), cvl(kz), cvl(ks), wspec, pl.BlockSpec((1, D_MODEL), const2)]
        scratch = []
        sem = ("arbitrary",)
    return pl.pallas_call(
        functools.partial(_out_proj_kernel, permute=grp.two_stage, final=final),
        grid=grid, in_specs=in_specs, out_specs=nat(D_MODEL),
        out_shape=jax.ShapeDtypeStruct((B, L, D_MODEL), F32),
        scratch_shapes=scratch, compiler_params=_params(*sem),
        name="out_proj_lat" if grp.two_stage else "out_proj_ctx",
    )(x, mod, attn, zg, scg, w, final_g)


def _group_tables(grp):
    fa, faf, g = _dft_tables(grp)
    tabs = {"fa": jnp.asarray(fa).astype(BF16), "fat": jnp.asarray(fa.T.copy()).astype(BF16),
            "faf": jnp.asarray(faf).astype(BF16)}
    if g is not None:
        tabs["g"] = jnp.asarray(g).astype(BF16)
    tabs["feats"] = tuple(jnp.asarray(a) for a in _filter_features(grp))
    return tabs


def kernel(x_prompt, x_sample, cache_k, cache_v, c, c_ctx, norm_g, mod_w, mod_b, w_in, attn_sink,
           hy_conv_w, hy_filt_w1, hy_filt_b1, hy_filt_w2, hy_filt_b2, hy_filt_w3, hy_filt_freq, hy_d,
           sc_conv_w, w_out, final_g):
    ctx = Group(x_prompt.shape[0], x_prompt.shape[1], 2 * x_prompt.shape[1], 1)
    lat = Group(x_sample.shape[0], x_sample.shape[1], 128, 2 * x_sample.shape[1] // 128)
    nlat = lat.B

    cond = jnp.zeros((16, D_MODEL), F32).at[0].set(c_ctx).at[1:1 + nlat].set(c)
    mods = _modulation(cond, mod_w, mod_b)

    w1t = jnp.pad(hy_filt_w1, ((0, 0), (0, LANE - FILTER_EMB), (0, 0))).transpose(0, 2, 1)
    w2t = hy_filt_w2.transpose(0, 2, 1)
    b1 = hy_filt_b1[:, :, None]
    b2 = hy_filt_b2[:, :, None]
    fq = hy_filt_freq[:, :, None]
    deltas = jnp.asarray(_decay_rates())
    tabs, spec = {}, {}
    for grp in (ctx, lat):
        t = _group_tables(grp)
        kf, kb = _filters(grp, t["feats"], w1t, b1, w2t, b2, hy_filt_w3, fq, deltas)
        spec[grp] = _spectrum(grp, kf, kb, t["faf"], t.get("g"))
        tabs[grp] = t
    rope = tuple(jnp.asarray(a) for a in _rope_tables(lat.L))

    w_in_b = w_in.astype(BF16)
    w_out_b = w_out.astype(BF16)
    wconv = jnp.concatenate([hy_conv_w.reshape(DEPTH, 3, 3, HYENA_W), sc_conv_w[:, :, None, :]], axis=2)
    dskip = hy_d.reshape(DEPTH, 1, 2 * HYENA_W)
    fg = final_g[None, :]
    lc = cache_k.shape[2]
    zpad = jnp.zeros((nlat, DEPTH, lc, HEAD_DIM), F32)
    ck_pad = jnp.concatenate([zpad, cache_k[:, :, :, 0], zpad, cache_k[:, :, :, 1]], axis=-1).astype(BF16)
    cv_nat = cache_v.reshape(nlat, DEPTH, lc, KV_W).astype(BF16)

    xp, xs = x_prompt, x_sample
    caches = (jnp.zeros((ctx.B, DEPTH, ctx.L, KV_W), F32), jnp.zeros((ctx.B, DEPTH, ctx.L, KV_W), F32))
    for l in range(DEPTH):
        g = norm_g[l][None, :]
        last = l == DEPTH - 1
        for grp in (ctx, lat):
            is_lat = grp is lat
            x = xs if is_lat else xp
            mod = mods[l, 1:1 + nlat][:, None, :] if is_lat else mods[l, 0:1][:, None, :]
            if is_lat:
                qq, kk, vv, ga, cv = _in_proj(grp, x, mod, g, w_in_b, wconv[l], l, rope_tabs=rope)
                attn = _attention(grp, attn_sink[l], qq, kk, vv, ga, l, ctx=(ck_pad, cv_nat))
            else:
                qq, kk, vv, ga, cv, *caches = _in_proj(grp, x, mod, g, w_in_b, wconv[l], l, caches=caches)
                attn = _attention(grp, attn_sink[l], qq, kk, vv, ga)
            z1 = _long_conv(grp, (cv, 0), (cv, 1), spec[grp], dskip, l, 0, tabs[grp])
            zg = _long_conv(grp, (z1, 0), (cv, 2), spec[grp], dskip, l, 1, tabs[grp])
            xn = _out_proj(grp, x, mod, attn, (zg, 0), (cv, 3), w_out_b, l, fg, last)
            if is_lat:
                xs = xn
            else:
                xp = xn
    shape = (ctx.B, DEPTH, ctx.L, N_KV_HEADS, HEAD_DIM)
    return (xp, xs, caches[0].reshape(shape), caches[1].reshape(shape))
```

```python
import functools
import math

import numpy as np
import jax
import jax.numpy as jnp
from jax import lax
from jax.experimental import pallas as pl
from jax.experimental.pallas import tpu as pltpu

F32 = jnp.float32
BF16 = jnp.bfloat16

D_MODEL = 1024
DEPTH = 4
GRID_W = 64
N_HEADS = 8
N_KV_HEADS = 2
HEAD_DIM = 64
ATTN_W = N_HEADS * HEAD_DIM
KV_W = N_KV_HEADS * HEAD_DIM
HYENA_W = 256
CONV_W = 256
WINDOW = 128
BLOCK = 128
FILTER_EMB = 33
FILTER_HIDDEN = 64
HYENA_TARGET = 1e-2
FAST_DECAY_PCT = 0.3
SLOW_DECAY_PCT = 1.5
ROPE_BASE = 10000.0
RMS_EPS = 1e-6
NEG_INF = -1e30
IN_COLS = 3328
C_Q, C_K, C_V, C_GA, C_CV = 0, 512, 640, 768, 1280

LANE = 128
ROWS = 1024
N1_STEP = 16
ROW_PITCH = 72
VMEM_LIMIT = 56 * 1024 * 1024
PITCH_PAD = 8
UNROLL = 8
UNROLL_WIDE = 16
ATTN_QBLOCKS = 8
CTX_CONV_SEQS = 16
CTX_SEQS = 2


class Group:
    def __init__(self, batch, seq, n1, n2):
        self.B, self.L, self.N1, self.N2 = batch, seq, n1, n2
        self.N1h = n1 // 2
        assert self.N1h * n2 == seq
        self.two_stage = n2 > 1


def _mm(a, b):
    return jnp.dot(a, b, preferred_element_type=F32)


def _mm_hi(a, b):
    return jnp.dot(a, b, preferred_element_type=F32, precision=lax.Precision.HIGHEST)


def _silu(x):
    return x * (1.0 / (1.0 + jnp.exp(-x)))


def _params(*sem):
    return pltpu.CompilerParams(dimension_semantics=sem, vmem_limit_bytes=VMEM_LIMIT)


def _rope_tables(seq):
    t = np.arange(seq)
    n_freq = HEAD_DIM // 4
    inv = ROPE_BASE ** (-np.arange(n_freq, dtype=np.float64) / n_freq)
    row = (t // GRID_W)[:, None] * inv
    col = (t % GRID_W)[:, None] * inv
    cos = np.concatenate([np.cos(row), np.cos(row), np.cos(col), np.cos(col)], axis=1)
    sin = np.concatenate([-np.sin(row), np.sin(row), -np.sin(col), np.sin(col)], axis=1)
    return (np.tile(cos, (1, 2)).astype(np.float32), np.tile(sin, (1, 2)).astype(np.float32))


def _conv_order(grp, table):
    return table.reshape(grp.N1h, grp.N2, -1).transpose(1, 0, 2).reshape(grp.L, -1)


def _filter_features(grp):
    L = grp.L
    t = np.linspace(0.0, 1.0, L)[:, None]
    bands = (FILTER_EMB - 1) // 2
    ang = (2.0 * math.pi / L) * np.arange(L)[:, None]
    fr = np.linspace(1e-4, bands - 1, bands)[None, :]
    feats = np.concatenate([t, np.cos(fr * ang), -np.sin(fr * ang)], axis=-1)
    feats = np.pad(feats, ((0, 0), (0, LANE - FILTER_EMB)))
    feats = _conv_order(grp, feats).astype(np.float32)
    return feats[:, 0:1], np.ascontiguousarray(feats.T)


def _decay_rates():
    max_decay = math.log(HYENA_TARGET) / FAST_DECAY_PCT
    min_decay = math.log(HYENA_TARGET) / SLOW_DECAY_PCT
    return np.abs(np.linspace(min_decay, max_decay, HYENA_W))[None, :].astype(np.float32)


def _dft_tables(grp):
    n1, n2 = grp.N1, grp.N2
    n = n1 * n2
    k = np.arange(n1)
    f = np.exp(-2j * np.pi * ((k[:, None] * k[None, :]) % n1) / n1)
    fh = f[:, : grp.N1h]
    fa = np.block([[fh.real, -fh.imag], [fh.imag, fh.real]])
    frev = np.concatenate([f[:, : grp.N1h], f[:, grp.N1h:][:, ::-1]], axis=1)
    faf = np.concatenate([frev.real, frev.imag], axis=0) / n
    g = None
    if grp.two_stage:
        j = np.arange(n2)
        ph = (k[:, None, None] * j[None, None, :] + n1 * j[None, :, None] * j[None, None, :]) % n
        gc = np.exp(-2j * np.pi * ph / n)
        g = np.concatenate([np.concatenate([gc.real, -gc.imag], axis=2),
                            np.concatenate([gc.imag, gc.real], axis=2)], axis=1)
        g = g.astype(np.float32)
    return fa.astype(np.float32), faf.astype(np.float32), g


def _mod_kernel(c_ref, w_ref, b_ref, o_ref):
    o_ref[...] = _mm_hi(_silu(c_ref[...]), w_ref[...]) + b_ref[...]


def _modulation(cond, mod_w, mod_b):
    nb = 3 * D_MODEL // 1024
    return pl.pallas_call(
        _mod_kernel,
        grid=(DEPTH, nb),
        in_specs=[
            pl.BlockSpec((16, D_MODEL), lambda l, j: (0, 0)),
            pl.BlockSpec((None, D_MODEL, 1024), lambda l, j: (l, 0, j)),
            pl.BlockSpec((None, 1, 1024), lambda l, j: (l, 0, j)),
        ],
        out_specs=pl.BlockSpec((None, 16, 1024), lambda l, j: (l, 0, j)),
        out_shape=jax.ShapeDtypeStruct((DEPTH, 16, 3 * D_MODEL), F32),
        compiler_params=_params("arbitrary", "arbitrary"),
        name="modulation",
    )(cond, mod_w, mod_b.reshape(DEPTH, 1, 3 * D_MODEL))


def _lane_lt64(shape):
    return lax.broadcasted_iota(jnp.int32, shape, 1) < HEAD_DIM


def _rope(x, cos, sin):
    lane = lax.broadcasted_iota(jnp.int32, x.shape, 1)
    first = (lane % 32) < 16
    partner = jnp.where(first, pltpu.roll(x, LANE - 16, axis=1), pltpu.roll(x, 16, axis=1))
    return x * cos + partner * sin


def _in_proj_kernel(*refs, latent, seq):
    if latent:
        x_ref, xp_ref, xn_ref, mod_ref, g_ref, w_ref, wc_ref, cos_ref, sin_ref = refs[:9]
        qq_ref, kk_ref, vv_ref, ga_ref, cv_ref, tmp_ref = refs[9:]
        first_row = pl.program_id(1) * ROWS
    else:
        x_ref, mod_ref, g_ref, w_ref, wc_ref = refs[:5]
        qq_ref, kk_ref, vv_ref, ga_ref, cv_ref, kf_ref, vf_ref = refs[7:]
        first_row = 0
    rope = latent

    shift = mod_ref[:, 0:D_MODEL]
    scale = mod_ref[:, D_MODEL:2 * D_MODEL]

    def modulated_norm(x):
        ms = jnp.mean(x * x, axis=-1, keepdims=True)
        y = x * lax.rsqrt(ms + RMS_EPS) * g_ref[...]
        return (y * (1.0 + scale) + shift).astype(BF16)

    h = modulated_norm(x_ref[...].reshape(ROWS, D_MODEL))
    h_ext = (jnp.concatenate([h, modulated_norm(jnp.concatenate([xp_ref[...], xn_ref[...]], axis=0))], axis=0)
             if latent else h)

    if rope:
        cos = cos_ref[...]
        sin = sin_ref[...]
    lt64 = _lane_lt64((ROWS, LANE))

    def put(ref, lo, val):
        lead = ref.shape[:-1]
        width = val.shape[-1]
        ref[(slice(None),) * len(lead) + (slice(lo, lo + width),)] = val.reshape(lead + (width,))

    yq = _mm(h, w_ref[:, C_Q:C_Q + ATTN_W]) * (HEAD_DIM ** -0.5)
    for j in range(ATTN_W // LANE):
        xj = yq[:, j * LANE:(j + 1) * LANE]
        rj = _rope(xj, cos, sin) if rope else xj
        even = jnp.where(lt64, rj, pltpu.roll(xj, HEAD_DIM, axis=1))
        odd = jnp.where(lt64, pltpu.roll(rj, HEAD_DIM, axis=1), xj)
        put(qq_ref, (2 * j) * LANE, even.astype(BF16))
        put(qq_ref, (2 * j + 1) * LANE, odd.astype(BF16))

    ykv = _mm(h, w_ref[:, C_K:C_V + KV_W])
    yk, yv = ykv[:, :KV_W], ykv[:, KV_W:]
    rk = _rope(yk, cos, sin) if rope else yk
    put(kk_ref, 0, jnp.where(lt64, rk, 0.0).astype(BF16))
    put(kk_ref, LANE, jnp.where(lt64, pltpu.roll(rk, HEAD_DIM, axis=1), 0.0).astype(BF16))
    put(vv_ref, 0, yv.astype(BF16))
    if not latent:
        put(kf_ref, 0, yk)
        put(vf_ref, 0, yv)

    put(ga_ref, 0, _silu(_mm(h, w_ref[:, C_GA:C_GA + ATTN_W])).astype(BF16))

    wd = HYENA_W
    col = lambda k, n=1: slice(C_CV + k * wd, C_CV + (k + n) * wd)
    sub = lax.broadcasted_iota(jnp.int32, (8, 1), 0)

    def set_row(arr, r, value):
        g0 = r - r % 8
        fixed = jnp.where(sub == r % 8, value, arr[g0:g0 + 8])
        parts = ([arr[:g0]] if g0 else []) + [fixed] + ([arr[g0 + 8:]] if g0 + 8 < arr.shape[0] else [])
        return jnp.concatenate(parts, axis=0)

    def dwconv3(u, u_halo, k):
        prev = pltpu.roll(u, 1, axis=0)
        nxt = pltpu.roll(u, ROWS - 1, axis=0)
        if latent:
            prev = set_row(prev, 0, jnp.where(first_row == 0, 0.0, u_halo[7:8]))
            nxt = set_row(nxt, ROWS - 1, jnp.where(first_row == seq - ROWS, 0.0, u_halo[8:9]))
        else:
            for s in range(ROWS // seq):
                prev = set_row(prev, s * seq, 0.0)
                nxt = set_row(nxt, (s + 1) * seq - 1, 0.0)
        return prev * wc_ref[0:1, k, :] + u * wc_ref[1:2, k, :] + nxt * wc_ref[2:3, k, :]

    def emit(k, val):
        if not latent:
            put(cv_ref, k * wd, val.astype(BF16))
            return
        n2 = cv_ref.shape[0]
        nsl = wd // LANE
        for sl in range(nsl):
            for i in range(N1_STEP):
                tmp_ref[k * nsl + sl, i * ROW_PITCH:i * ROW_PITCH + n2, :] = (
                    val[i * n2:(i + 1) * n2, sl * LANE:(sl + 1) * LANE])
        for j in range(n2):
            for sl in range(nsl):
                lo = k * wd + sl * LANE
                cv_ref[j, :, lo:lo + LANE] = (
                    tmp_ref[k * nsl + sl, pl.ds(j, N1_STEP, stride=ROW_PITCH), :].astype(BF16))

    def project(cols):
        y = _mm(h_ext, w_ref[:, cols])
        return (y[:ROWS], y[ROWS:]) if latent else (y, None)

    def part(y_halo, lo):
        return None if y_halo is None else y_halo[:, lo:lo + wd]

    ya, ya_halo = project(col(0, 2))
    emit(0, dwconv3(ya[:, :wd], part(ya_halo, 0), 0))
    emit(1, dwconv3(ya[:, wd:], part(ya_halo, wd), 1))
    yb, yb_halo = project(col(2, 2))
    emit(2, dwconv3(yb[:, :wd], part(yb_halo, 0), 2) * _silu(yb[:, wd:]))
    yc, yc_halo = project(col(4, 2))
    yd, yd_halo = project(col(6, 2))
    inner_halo = part(yc_halo, wd) * part(yd_halo, 0) if latent else None
    inner = dwconv3(yc[:, wd:] * yd[:, :wd], inner_halo, 3)
    emit(3, yc[:, :wd] * inner * _silu(yd[:, wd:]))


def _in_proj(grp, x, mod, g, w, wconv, layer, rope_tabs=None, caches=None):
    B, L = grp.B, grp.L
    cvw = 4 * HYENA_W
    const2 = lambda *_: (0, 0)
    wspec = pl.BlockSpec((None, D_MODEL, IN_COLS), lambda *_: (layer, 0, 0))
    wcspec = pl.BlockSpec((3, 4, HYENA_W), lambda *_: (0, 0, 0))
    aliases = {}
    if grp.two_stage:
        n1h, n2 = grp.N1h, grp.N2
        assert ROWS == N1_STEP * n2
        nblk = L // ROWS
        grid = (B, nblk)
        nat = lambda wd: pl.BlockSpec((None, ROWS, wd), lambda b, j: (b, j, 0))
        nat_shape = lambda wd, dt: jax.ShapeDtypeStruct((B, L, wd), dt)
        x8 = x.reshape(B, L // 8, 8, D_MODEL)
        per8 = ROWS // 8
        in_specs = [nat(D_MODEL),
                    pl.BlockSpec((None, None, 8, D_MODEL), lambda b, j: (b, jnp.maximum(j * per8 - 1, 0), 0, 0)),
                    pl.BlockSpec((None, None, 8, D_MODEL),
                                 lambda b, j: (b, jnp.minimum((j + 1) * per8, L // 8 - 1), 0, 0)),
                    pl.BlockSpec((None, 1, 3 * D_MODEL), lambda b, j: (b, 0, 0)),
                    pl.BlockSpec((1, D_MODEL), const2),
                    wspec, wcspec,
                    pl.BlockSpec((ROWS, LANE), lambda b, j: (j, 0)),
                    pl.BlockSpec((ROWS, LANE), lambda b, j: (j, 0))]
        args = [x, x8, x8, mod, g, w, wconv, rope_tabs[0], rope_tabs[1]]
        out_specs = [nat(2 * ATTN_W), nat(2 * LANE), nat(KV_W), nat(ATTN_W),
                     pl.BlockSpec((None, n2, N1_STEP, cvw), lambda b, j: (b, 0, j, 0))]
        out_shape = [nat_shape(2 * ATTN_W, BF16), nat_shape(2 * LANE, BF16), nat_shape(KV_W, BF16),
                     nat_shape(ATTN_W, BF16), jax.ShapeDtypeStruct((B, n2, n1h, cvw), BF16)]
        scratch = [pltpu.VMEM((cvw // LANE, N1_STEP * ROW_PITCH, LANE), F32)]
        sem = ("arbitrary", "arbitrary")
    else:
        bb = ROWS // L
        grid = (B // bb,)
        nat = lambda wd: pl.BlockSpec((bb, L, wd), lambda i: (i, 0, 0))
        nat_shape = lambda wd, dt: jax.ShapeDtypeStruct((B, L, wd), dt)
        in_specs = [nat(D_MODEL),
                    pl.BlockSpec((None, 1, 3 * D_MODEL), lambda i: (0, 0, 0)),
                    pl.BlockSpec((1, D_MODEL), const2),
                    wspec, wcspec,
                    pl.BlockSpec(memory_space=pl.ANY),
                    pl.BlockSpec(memory_space=pl.ANY)]
        args = [x, mod, g, w, wconv, caches[0], caches[1]]
        aliases = {5: 5, 6: 6}
        cache_spec = pl.BlockSpec((bb, None, L, KV_W), lambda i: (i, layer, 0, 0))
        cache_shape = jax.ShapeDtypeStruct((B, DEPTH, L, KV_W), F32)
        out_specs = [nat(2 * ATTN_W), nat(2 * LANE), nat(KV_W), nat(ATTN_W), nat(cvw), cache_spec, cache_spec]
        out_shape = [nat_shape(2 * ATTN_W, BF16), nat_shape(2 * LANE, BF16), nat_shape(KV_W, BF16),
                     nat_shape(ATTN_W, BF16), nat_shape(cvw, BF16), cache_shape, cache_shape]
        scratch = []
        sem = ("arbitrary",)
    outs = pl.pallas_call(
        functools.partial(_in_proj_kernel, latent=grp.two_stage, seq=L),
        grid=grid, in_specs=in_specs, out_specs=out_specs, out_shape=out_shape,
        scratch_shapes=scratch, input_output_aliases=aliases, compiler_params=_params(*sem),
        name="in_proj_lat" if grp.two_stage else "in_proj_ctx",
    )(*args)
    outs = list(outs)
    if not grp.two_stage:
        outs[4] = outs[4].reshape(B, 1, L, cvw)
    return outs


def _window_bias():
    qi = np.arange(BLOCK)[:, None]
    kj = np.arange(3 * BLOCK)[None, :]
    return np.where(np.abs(kj - BLOCK - qi) <= WINDOW, 0.0, NEG_INF).astype(np.float32)


def _attn_kernel(*refs, lq, local, nblocks, qblocks):
    if local:
        sink_ref, bias_ref, qq_ref, kk_ref, vv_ref, ck_ref, cv_ref, ga_ref, o_ref = refs
    else:
        sink_ref, qq_ref, kk_ref, vv_ref, ga_ref, o_ref = refs
    group = N_HEADS // N_KV_HEADS
    rows = group * lq
    nt = (((1,), (1,)), ((), ()))
    lt64 = _lane_lt64((lq, LANE))

    for qb in range(qblocks):
        qrows = slice(qb * lq, (qb + 1) * lq)
        if local:
            n = pl.program_id(1) * qblocks + qb
            starts = [pl.multiple_of(jnp.maximum(n - 1, 0) * BLOCK, BLOCK),
                      pl.multiple_of(n * BLOCK, BLOCK),
                      pl.multiple_of(jnp.minimum(n + 1, nblocks - 1) * BLOCK, BLOCK)]
            kj = lax.broadcasted_iota(jnp.int32, (1, 3 * BLOCK), 1)
            edge = (jnp.where((kj < BLOCK) & (n == 0), NEG_INF, 0.0)
                    + jnp.where((kj >= 2 * BLOCK) & (n == nblocks - 1), NEG_INF, 0.0))
            bias = bias_ref[...] + edge
        for g in range(N_KV_HEADS):
            lhs = jnp.concatenate(
                [qq_ref[qrows, (group * g + i) * LANE:(group * g + i + 1) * LANE] for i in range(group)], axis=0)
            gl = slice(g * LANE, (g + 1) * LANE)
            if local:
                kwin = jnp.concatenate([kk_ref[pl.ds(s, BLOCK), gl] for s in starts], axis=0)
                vals = jnp.concatenate([vv_ref[pl.ds(s, BLOCK), :] for s in starts] + [cv_ref[...]], axis=0)
                s_loc = lax.dot_general(lhs, kwin, nt, preferred_element_type=F32)
                s_ctx = lax.dot_general(lhs, ck_ref[:, gl], nt, preferred_element_type=F32)
            else:
                vals = vv_ref[qrows, :]
                s_loc = lax.dot_general(lhs, kk_ref[qrows, gl], nt, preferred_element_type=F32)
            row_head = lax.broadcasted_iota(jnp.int32, (rows, 1), 0) // lq
            snk = jnp.zeros((rows, 1), F32)
            for i in range(group):
                snk = jnp.where(row_head == i, sink_ref[group * g + i], snk)
            if local:
                s_loc = s_loc + jnp.concatenate([bias] * group, axis=0)
            m = jnp.maximum(jnp.max(s_loc, axis=-1, keepdims=True), snk)
            if local:
                m = jnp.maximum(m, jnp.max(s_ctx, axis=-1, keepdims=True))
            ones = jnp.ones((vals.shape[0], LANE), BF16)
            vext = jnp.concatenate([vals, ones], axis=1)
            nk = s_loc.shape[1]
            o = _mm(jnp.exp(s_loc - m).astype(BF16), vext[:nk])
            if local:
                o = o + _mm(jnp.exp(s_ctx - m).astype(BF16), vext[nk:])
            o = o[:, :LANE] / (o[:, LANE:] + jnp.exp(snk - m))
            for jj in range(group // 2):
                a = o[(2 * jj) * lq:(2 * jj + 1) * lq]
                b = o[(2 * jj + 1) * lq:(2 * jj + 2) * lq]
                if g == 0:
                    tile = jnp.where(lt64, a, pltpu.roll(b, HEAD_DIM, axis=1))
                else:
                    tile = jnp.where(lt64, pltpu.roll(a, HEAD_DIM, axis=1), b)
                j = (group // 2) * g + jj
                cols = slice(j * LANE, (j + 1) * LANE)
                o_ref[qrows, cols] = (tile * ga_ref[qrows, cols].astype(F32)).astype(BF16)


def _attention(grp, sink, qq, kk, vv, ga, layer=0, ctx=None):
    B, L = grp.B, grp.L
    smem = pl.BlockSpec(memory_space=pltpu.SMEM)
    if ctx is not None:
        ck, cv = ctx
        nb = L // BLOCK
        lc = ck.shape[2]
        qrows = ATTN_QBLOCKS * BLOCK
        grid = (B, nb // ATTN_QBLOCKS)
        blk = lambda wd: pl.BlockSpec((None, qrows, wd), lambda b, n: (b, n, 0))
        full = lambda rows, wd: pl.BlockSpec((None, rows, wd), lambda b, n: (b, 0, 0))
        cache = lambda wd: pl.BlockSpec((None, None, lc, wd), lambda b, n: (b, layer, 0, 0))
        in_specs = [smem, pl.BlockSpec((BLOCK, 3 * BLOCK), lambda b, n: (0, 0)),
                    blk(2 * ATTN_W), full(L, 2 * LANE), full(L, KV_W), cache(2 * LANE), cache(KV_W),
                    blk(ATTN_W)]
        args = [sink, jnp.asarray(_window_bias()), qq, kk, vv, ck, cv, ga]
        out_spec = blk(ATTN_W)
        out_shape = jax.ShapeDtypeStruct((B, L, ATTN_W), BF16)
        kern = functools.partial(_attn_kernel, lq=BLOCK, local=True, nblocks=nb, qblocks=ATTN_QBLOCKS)
        sem = ("arbitrary", "arbitrary")
    else:
        grid = (B // CTX_SEQS,)
        flat = lambda a: a.reshape(B * L, a.shape[-1])
        blk = lambda wd: pl.BlockSpec((CTX_SEQS * L, wd), lambda i: (i, 0))
        in_specs = [smem, blk(2 * ATTN_W), blk(2 * LANE), blk(KV_W), blk(ATTN_W)]
        args = [sink, flat(qq), flat(kk), flat(vv), flat(ga)]
        out_spec = blk(ATTN_W)
        out_shape = jax.ShapeDtypeStruct((B * L, ATTN_W), BF16)
        kern = functools.partial(_attn_kernel, lq=L, local=False, nblocks=1, qblocks=CTX_SEQS)
        sem = ("arbitrary",)
    out = pl.pallas_call(
        kern, grid=grid, in_specs=in_specs, out_specs=out_spec, out_shape=out_shape,
        compiler_params=_params(*sem), name="attn_lat" if ctx is not None else "attn_ctx",
    )(*args)
    return out.reshape(B, L, ATTN_W)


def _filter_kernel(t_ref, f_ref, w1_ref, b1_ref, w2_ref, b2_ref, w3_ref, fq_ref, dl_ref,
                   kf_ref, kb_ref, *, chunk):
    rows = t_ref.shape[0]
    width = 2 * HYENA_W
    fq = fq_ref[...]
    nn = (((0,), (0,)), ((), ()))

    def body(i, acc):
        r0 = pl.multiple_of(i * chunk, chunk)
        h = jnp.sin(fq * (_mm_hi(w1_ref[...], f_ref[:, pl.ds(r0, chunk)]) + b1_ref[...]))
        h = jnp.sin(fq * (_mm_hi(w2_ref[...], h) + b2_ref[...]))
        dec = jnp.exp(-(t_ref[pl.ds(r0, chunk), :] * dl_ref[...]))
        dec = jnp.concatenate([dec, dec], axis=1)
        hb = h.astype(BF16)
        kf = lax.dot_general(hb, w3_ref[:, 0:width].astype(BF16), nn, preferred_element_type=F32) * dec
        kb = lax.dot_general(hb, w3_ref[:, width:2 * width].astype(BF16), nn, preferred_element_type=F32) * dec
        rid = lax.broadcasted_iota(jnp.int32, kb.shape, 0) + r0
        kb = jnp.where(rid == 0, 0.0, kb)
        kf_ref[pl.ds(r0, chunk), :] = kf
        kb_ref[pl.ds(r0, chunk), :] = kb
        return acc + jnp.sum(jnp.abs(kf) + jnp.abs(kb), axis=0, keepdims=True)

    total = lax.fori_loop(0, rows // chunk, body, jnp.zeros((1, width), F32))

    def scale(i, c):
        r0 = pl.multiple_of(i * chunk, chunk)
        kf_ref[pl.ds(r0, chunk), :] = kf_ref[pl.ds(r0, chunk), :] / total
        kb_ref[pl.ds(r0, chunk), :] = kb_ref[pl.ds(r0, chunk), :] / total
        return c

    lax.fori_loop(0, rows // chunk, scale, 0)


def _filters(grp, feats, w1, b1, w2, b2, w3, fq, deltas):
    L = grp.L
    chunk = min(512, L)
    c2 = lambda l: (0, 0)
    per = lambda a, b: pl.BlockSpec((None, a, b), lambda l: (l, 0, 0))
    shape = jax.ShapeDtypeStruct((DEPTH, L, 2 * HYENA_W), F32)
    return pl.pallas_call(
        functools.partial(_filter_kernel, chunk=chunk), grid=(DEPTH,),
        in_specs=[pl.BlockSpec((L, 1), c2), pl.BlockSpec((LANE, L), c2),
                  per(FILTER_HIDDEN, LANE), per(FILTER_HIDDEN, 1), per(FILTER_HIDDEN, FILTER_HIDDEN),
                  per(FILTER_HIDDEN, 1), per(FILTER_HIDDEN, 4 * HYENA_W), per(FILTER_HIDDEN, 1),
                  pl.BlockSpec((1, HYENA_W), c2)],
        out_specs=[per(L, 2 * HYENA_W)] * 2, out_shape=[shape] * 2,
        compiler_params=_params("arbitrary"),
        name="filters_lat" if grp.two_stage else "filters_ctx",
    )(*feats, w1, b1, w2, b2, w3, fq, deltas)


def _store_spectrum_rows(s_ref, base, a, n1):
    for comp in range(2):
        for sl in range(2):
            s_ref[comp, sl, pl.ds(base, n1), :] = a[comp * n1:(comp + 1) * n1, sl * LANE:(sl + 1) * LANE]


def _load_column(s_ref, k1, n2, pitch):
    parts = [jnp.concatenate([s_ref[comp, sl, pl.ds(k1, n2, stride=pitch), :] for sl in range(2)], axis=1)
             for comp in range(2)]
    return jnp.concatenate(parts, axis=0).astype(BF16)


def _spectrum2_kernel(kf_ref, kb_ref, faf_ref, g_ref, o_ref, s_ref, *, n1, n2):
    n1h = n1 // 2
    pitch = n1 + PITCH_PAD
    faf = faf_ref[...]

    def stage_a(i, c):
        for t in range(UNROLL):
            j = i * UNROLL + t
            r0 = pl.multiple_of(j * n1h, n1h)
            rb = pl.multiple_of(((n2 - j) % n2) * n1h, n1h)
            back = kb_ref[pl.ds(rb, n1h), :]
            back = jnp.where(j == 0, pltpu.roll(back, n1h - 1, axis=0), back)
            rhs = jnp.concatenate([kf_ref[pl.ds(r0, n1h), :], back], axis=0).astype(BF16)
            _store_spectrum_rows(s_ref, pl.multiple_of(j * pitch, 8), _mm(faf, rhs), n1)
        return c

    lax.fori_loop(0, n2 // UNROLL, stage_a, 0)

    def stage_c(i, c):
        for t in range(UNROLL_WIDE):
            k1 = i * UNROLL_WIDE + t
            o_ref[k1] = _mm(g_ref[k1], _load_column(s_ref, k1, n2, pitch)).astype(BF16)
        return c

    lax.fori_loop(0, n1 // UNROLL_WIDE, stage_c, 0)


def _spectrum1_kernel(kf_ref, kb_ref, faf_ref, o_ref):
    back = kb_ref[...]
    rhs = jnp.concatenate([kf_ref[...], pltpu.roll(back, back.shape[0] - 1, axis=0)], axis=0).astype(BF16)
    o_ref[...] = _mm(faf_ref[...], rhs).astype(BF16)


def _spectrum(grp, kf, kb, faf, g):
    n1, n2, L = grp.N1, grp.N2, grp.L
    wd = 2 * LANE
    nblk = 2 * HYENA_W // wd
    kin = pl.BlockSpec((None, L, wd), lambda l, c: (l, 0, c))
    if grp.two_stage:
        pitch = n1 + PITCH_PAD
        return pl.pallas_call(
            functools.partial(_spectrum2_kernel, n1=n1, n2=n2), grid=(DEPTH, nblk),
            in_specs=[kin, kin, pl.BlockSpec((2 * n1, n1), lambda l, c: (0, 0)),
                      pl.BlockSpec((n1, 2 * n2, 2 * n2), lambda l, c: (0, 0, 0))],
            out_specs=pl.BlockSpec((None, n1, 2 * n2, wd), lambda l, c: (l, 0, 0, c)),
            out_shape=jax.ShapeDtypeStruct((DEPTH, n1, 2 * n2, 2 * HYENA_W), BF16),
            scratch_shapes=[pltpu.VMEM((2, 2, n2 * pitch, LANE), F32)],
            compiler_params=_params("arbitrary", "arbitrary"), name="spectrum_lat",
        )(kf, kb, faf, g)
    return pl.pallas_call(
        _spectrum1_kernel, grid=(DEPTH, nblk),
        in_specs=[kin, kin, pl.BlockSpec((2 * n1, n1), lambda l, c: (0, 0))],
        out_specs=pl.BlockSpec((None, 2 * n1, wd), lambda l, c: (l, 0, c)),
        out_shape=jax.ShapeDtypeStruct((DEPTH, 2 * n1, 2 * HYENA_W), BF16),
        compiler_params=_params("arbitrary", "arbitrary"), name="spectrum_ctx",
    )(kf, kb, faf)


def _pair_operand(u_ref, j):
    nb = u_ref.shape[0]
    re = jnp.concatenate([u_ref[b, j] for b in range(0, nb, 2)], axis=1)
    im = jnp.concatenate([u_ref[b, j] for b in range(1, nb, 2)], axis=1)
    return jnp.concatenate([re, im], axis=0)


def _cmul(x, k, half):
    k = k.astype(F32)
    npair = x.shape[1] // LANE
    kr = jnp.concatenate([k[:half]] * npair, axis=1)
    ki = jnp.concatenate([k[half:]] * npair, axis=1)
    xr, xi = x[:half], x[half:]
    return jnp.concatenate([xr * kr - xi * ki, xr * ki + xi * kr], axis=0).astype(BF16)


def _conv_epilogue(u_ref, m_ref, o_ref, d, y, j, n1h):
    for b in range(u_ref.shape[0]):
        yb = y[(b % 2) * n1h:(b % 2 + 1) * n1h, (b // 2) * LANE:(b // 2 + 1) * LANE]
        u = u_ref[b, j].astype(F32)
        o_ref[b, j] = (m_ref[b, j].astype(F32) * (yb + u * d)).astype(BF16)


def _conv2_kernel(u_ref, m_ref, fa_ref, fat_ref, g_ref, k_ref, d_ref, o_ref, s_ref, *, n1, n2):
    n1h = n1 // 2
    pitch = n1 + PITCH_PAD
    fa = fa_ref[...]
    fat = fat_ref[...]
    d = d_ref[...]

    def stage_a(i, c):
        for t in range(UNROLL_WIDE):
            j = i * UNROLL_WIDE + t
            _store_spectrum_rows(s_ref, pl.multiple_of(j * pitch, 8), _mm(fa, _pair_operand(u_ref, j)), n1)
        return c

    lax.fori_loop(0, n2 // UNROLL_WIDE, stage_a, 0)

    def stage_c(i, c):
        ks = [i * UNROLL_WIDE + t for t in range(UNROLL_WIDE)]
        cols = [_load_column(s_ref, k1, n2, pitch) for k1 in ks]
        backs = []
        for k1, col in zip(ks, cols):
            g = g_ref[k1]
            y = _cmul(_mm(g, col), k_ref[k1], n2)
            backs.append(lax.dot_general(g, y, (((0,), (0,)), ((), ())), preferred_element_type=F32))
        for k1, back in zip(ks, backs):
            for comp in range(2):
                for sl in range(2):
                    s_ref[comp, sl, pl.ds(k1, n2, stride=pitch), :] = (
                        back[comp * n2:(comp + 1) * n2, sl * LANE:(sl + 1) * LANE])
        return c

    lax.fori_loop(0, n1 // UNROLL_WIDE, stage_c, 0)

    def stage_inv(i, c):
        for t in range(UNROLL):
            j = i * UNROLL + t
            base = pl.multiple_of(j * pitch, 8)
            parts = [jnp.concatenate([s_ref[comp, sl, pl.ds(base, n1), :] for sl in range(2)], axis=1)
                     for comp in range(2)]
            y = _mm(fat, jnp.concatenate(parts, axis=0).astype(BF16))
            _conv_epilogue(u_ref, m_ref, o_ref, d, y, j, n1h)
        return c

    lax.fori_loop(0, n2 // UNROLL, stage_inv, 0)


def _conv1_kernel(u_ref, m_ref, fa_ref, fat_ref, k_ref, d_ref, o_ref, *, n1):
    x = _mm(fa_ref[...], _pair_operand(u_ref, 0))
    y = _mm(fat_ref[...], _cmul(x, k_ref[...], n1))
    _conv_epilogue(u_ref, m_ref, o_ref, d_ref[...], y, 0, n1 // 2)


def _long_conv(grp, u, m, spec, d, layer, order, tabs):
    B, n1, n2, n1h = grp.B, grp.N1, grp.N2, grp.N1h
    ncb = HYENA_W // LANE
    nbatch = 4 if grp.two_stage else CTX_CONV_SEQS
    grid = (ncb, B // nbatch)
    group = lambda k: pl.BlockSpec((nbatch, n2, n1h, LANE), lambda c, q: (q, 0, 0, k * ncb + c))
    (u, ku), (m, km) = u, m
    uspec, mspec = group(ku), group(km)
    data = group(0)
    dspec = pl.BlockSpec((None, 1, LANE), lambda c, q: (layer, 0, order * ncb + c))
    fa = pl.BlockSpec((2 * n1, n1), lambda c, q: (0, 0))
    fat = pl.BlockSpec((n1, 2 * n1), lambda c, q: (0, 0))
    out_shape = jax.ShapeDtypeStruct((B, n2, n1h, HYENA_W), BF16)
    if grp.two_stage:
        pitch = n1 + PITCH_PAD
        return pl.pallas_call(
            functools.partial(_conv2_kernel, n1=n1, n2=n2), grid=grid,
            in_specs=[uspec, mspec, fa, fat,
                      pl.BlockSpec((n1, 2 * n2, 2 * n2), lambda c, q: (0, 0, 0)),
                      pl.BlockSpec((None, n1, 2 * n2, LANE), lambda c, q: (layer, 0, 0, order * ncb + c)),
                      dspec],
            out_specs=data, out_shape=out_shape,
            scratch_shapes=[pltpu.VMEM((2, 2, n2 * pitch, LANE), F32)],
            compiler_params=_params("arbitrary", "arbitrary"), name="long_conv_lat",
        )(u, m, tabs["fa"], tabs["fat"], tabs["g"], spec, d)
    return pl.pallas_call(
        functools.partial(_conv1_kernel, n1=n1), grid=grid,
        in_specs=[uspec, mspec, fa, fat,
                  pl.BlockSpec((None, 2 * n1, LANE), lambda c, q: (layer, 0, order * ncb + c)), dspec],
        out_specs=data, out_shape=out_shape,
        compiler_params=_params("arbitrary", "arbitrary"), name="long_conv_ctx",
    )(u, m, tabs["fa"], tabs["fat"], spec, d)


def _out_proj_kernel(*refs, permute, final):
    if permute:
        x_ref, mod_ref, a_ref, z_ref, s_ref, w_ref, fg_ref, o_ref, tmp_ref = refs
    else:
        x_ref, mod_ref, a_ref, z_ref, s_ref, w_ref, fg_ref, o_ref = refs
    a = a_ref[...].reshape(ROWS, ATTN_W)
    nsl = HYENA_W // LANE
    if permute:
        n2 = z_ref.shape[0]
        for j in range(n2):
            for sl in range(nsl):
                cols = slice(sl * LANE, (sl + 1) * LANE)
                tmp_ref[sl, pl.ds(j, N1_STEP, stride=ROW_PITCH), :] = z_ref[j, :, cols].astype(F32)
                tmp_ref[nsl + sl, pl.ds(j, N1_STEP, stride=ROW_PITCH), :] = s_ref[j, :, cols].astype(F32)
        conv = jnp.concatenate(
            [jnp.concatenate([tmp_ref[i, k * ROW_PITCH:k * ROW_PITCH + n2, :] for k in range(N1_STEP)], axis=0)
             for i in range(2 * nsl)], axis=1).astype(BF16)
    else:
        conv = jnp.concatenate([z_ref[...].reshape(ROWS, HYENA_W), s_ref[...].reshape(ROWS, CONV_W)], axis=1)
    y = _mm(a, w_ref[0:ATTN_W, :]) + _mm(conv, w_ref[ATTN_W:, :])
    gate = mod_ref[:, 2 * D_MODEL:3 * D_MODEL]
    xn = x_ref[...].reshape(ROWS, D_MODEL) + gate * y
    if final:
        ms = jnp.mean(xn * xn, axis=-1, keepdims=True)
        xn = xn * lax.rsqrt(ms + RMS_EPS) * fg_ref[...]
    o_ref[...] = xn.reshape(o_ref.shape)


def _out_proj(grp, x, mod, attn, zg, scg, w, layer, final_g, final):
    B, L = grp.B, grp.L
    const2 = lambda *_: (0, 0)
    wspec = pl.BlockSpec((None, D_MODEL, D_MODEL), lambda *_: (layer, 0, 0))
    (zg, kz), (scg, ks) = zg, scg
    if grp.two_stage:
        n2 = grp.N2
        grid = (B, L // ROWS)
        nat = lambda wd: pl.BlockSpec((None, ROWS, wd), lambda b, j: (b, j, 0))
        cvl = lambda k: pl.BlockSpec((None, n2, N1_STEP, HYENA_W), lambda b, j: (b, 0, j, k))
        in_specs = [nat(D_MODEL), pl.BlockSpec((None, 1, 3 * D_MODEL), lambda b, j: (b, 0, 0)),
                    nat(ATTN_W), cvl(kz), cvl(ks), wspec, pl.BlockSpec((1, D_MODEL), const2)]
        scratch = [pltpu.VMEM((2 * HYENA_W // LANE, N1_STEP * ROW_PITCH, LANE), F32)]
        sem = ("arbitrary", "arbitrary")
    else:
        bb = ROWS // L
        grid = (B // bb,)
        nat = lambda wd: pl.BlockSpec((bb, L, wd), lambda i: (i, 0, 0))
        cvl = lambda k: pl.BlockSpec((bb, None, L, HYENA_W), lambda i: (i, 0, 0, k))
        in_specs = [nat(D_MODEL), pl.BlockSpec((None, 1, 3 * D_MODEL), lambda i: (0, 0, 0)),
                    nat(ATTN_W), cvl(kz), cvl(ks), wspec, pl.BlockSpec((1, D_MODEL), const2)]
        scratch = []
        sem = ("arbitrary",)
    return pl.pallas_call(
        functools.partial(_out_proj_kernel, permute=grp.two_stage, final=final),
        grid=grid, in_specs=in_specs, out_specs=nat(D_MODEL),
        out_shape=jax.ShapeDtypeStruct((B, L, D_MODEL), F32),
        scratch_shapes=scratch, compiler_params=_params(*sem),
        name="out_proj_lat" if grp.two_stage else "out_proj_ctx",
    )(x, mod, attn, zg, scg, w, final_g)


def _group_tables(grp):
    fa, faf, g = _dft_tables(grp)
    tabs = {"fa": jnp.asarray(fa).astype(BF16), "fat": jnp.asarray(fa.T.copy()).astype(BF16),
            "faf": jnp.asarray(faf).astype(BF16)}
    if g is not None:
        tabs["g"] = jnp.asarray(g).astype(BF16)
    tabs["feats"] = tuple(jnp.asarray(a) for a in _filter_features(grp))
    return tabs


def kernel(x_prompt, x_sample, cache_k, cache_v, c, c_ctx, norm_g, mod_w, mod_b, w_in, attn_sink,
           hy_conv_w, hy_filt_w1, hy_filt_b1, hy_filt_w2, hy_filt_b2, hy_filt_w3, hy_filt_freq, hy_d,
           sc_conv_w, w_out, final_g):
    ctx = Group(x_prompt.shape[0], x_prompt.shape[1], 2 * x_prompt.shape[1], 1)
    lat = Group(x_sample.shape[0], x_sample.shape[1], 128, 2 * x_sample.shape[1] // 128)
    nlat = lat.B

    cond = jnp.zeros((16, D_MODEL), F32).at[0].set(c_ctx).at[1:1 + nlat].set(c)
    mods = _modulation(cond, mod_w, mod_b)

    w1t = jnp.pad(hy_filt_w1, ((0, 0), (0, LANE - FILTER_EMB), (0, 0))).transpose(0, 2, 1)
    w2t = hy_filt_w2.transpose(0, 2, 1)
    b1 = hy_filt_b1[:, :, None]
    b2 = hy_filt_b2[:, :, None]
    fq = hy_filt_freq[:, :, None]
    deltas = jnp.asarray(_decay_rates())
    tabs, spec = {}, {}
    for grp in (ctx, lat):
        t = _group_tables(grp)
        kf, kb = _filters(grp, t["feats"], w1t, b1, w2t, b2, hy_filt_w3, fq, deltas)
        spec[grp] = _spectrum(grp, kf, kb, t["faf"], t.get("g"))
        tabs[grp] = t
    rope = tuple(jnp.asarray(a) for a in _rope_tables(lat.L))

    w_in_b = w_in.astype(BF16)
    w_out_b = w_out.astype(BF16)
    wconv = jnp.concatenate([hy_conv_w.reshape(DEPTH, 3, 3, HYENA_W), sc_conv_w[:, :, None, :]], axis=2)
    dskip = hy_d.reshape(DEPTH, 1, 2 * HYENA_W)
    fg = final_g[None, :]
    lc = cache_k.shape[2]
    zpad = jnp.zeros((nlat, DEPTH, lc, HEAD_DIM), F32)
    ck_pad = jnp.concatenate([zpad, cache_k[:, :, :, 0], zpad, cache_k[:, :, :, 1]], axis=-1).astype(BF16)
    cv_nat = cache_v.reshape(nlat, DEPTH, lc, KV_W).astype(BF16)

    xp, xs = x_prompt, x_sample
    caches = (jnp.zeros((ctx.B, DEPTH, ctx.L, KV_W), F32), jnp.zeros((ctx.B, DEPTH, ctx.L, KV_W), F32))
    for l in range(DEPTH):
        g = norm_g[l][None, :]
        last = l == DEPTH - 1
        for grp in (ctx, lat):
            is_lat = grp is lat
            x = xs if is_lat else xp
            mod = mods[l, 1:1 + nlat][:, None, :] if is_lat else mods[l, 0:1][:, None, :]
            if is_lat:
                qq, kk, vv, ga, cv = _in_proj(grp, x, mod, g, w_in_b, wconv[l], l, rope_tabs=rope)
                attn = _attention(grp, attn_sink[l], qq, kk, vv, ga, l, ctx=(ck_pad, cv_nat))
            else:
                qq, kk, vv, ga, cv, *caches = _in_proj(grp, x, mod, g, w_in_b, wconv[l], l, caches=caches)
                attn = _attention(grp, attn_sink[l], qq, kk, vv, ga)
            z1 = _long_conv(grp, (cv, 0), (cv, 1), spec[grp], dskip, l, 0, tabs[grp])
            zg = _long_conv(grp, (z1, 0), (cv, 2), spec[grp], dskip, l, 1, tabs[grp])
            xn = _out_proj(grp, x, mod, attn, (zg, 0), (cv, 3), w_out_b, l, fg, last)
            if is_lat:
                xs = xn
            else:
                xp = xn
    shape = (ctx.B, DEPTH, ctx.L, N_KV_HEADS, HEAD_DIM)
    return (xp, xs, caches[0].reshape(shape), caches[1].reshape(shape))
```

```python
import functools
import math

import numpy as np
import jax
import jax.numpy as jnp
from jax import lax
from jax.experimental import pallas as pl
from jax.experimental.pallas import tpu as pltpu

F32 = jnp.float32
BF16 = jnp.bfloat16

D_MODEL = 1024
DEPTH = 4
GRID_W = 64
N_HEADS = 8
N_KV_HEADS = 2
HEAD_DIM = 64
ATTN_W = N_HEADS * HEAD_DIM
KV_W = N_KV_HEADS * HEAD_DIM
HYENA_W = 256
CONV_W = 256
WINDOW = 128
BLOCK = 128
FILTER_EMB = 33
FILTER_HIDDEN = 64
HYENA_TARGET = 1e-2
FAST_DECAY_PCT = 0.3
SLOW_DECAY_PCT = 1.5
ROPE_BASE = 10000.0
RMS_EPS = 1e-6
NEG_INF = -1e30
IN_COLS = 3328
C_Q, C_K, C_V, C_GA, C_CV = 0, 512, 640, 768, 1280

LANE = 128
ROWS = 1024
N1_STEP = 16
ROW_PITCH = 72
VMEM_LIMIT = 56 * 1024 * 1024
PITCH_PAD = 8
UNROLL = 8
UNROLL_WIDE = 16
ATTN_QBLOCKS = 8
CTX_CONV_SEQS = 16
CTX_SEQS = 2


class Group:
    def __init__(self, batch, seq, n1, n2):
        self.B, self.L, self.N1, self.N2 = batch, seq, n1, n2
        self.N1h = n1 // 2
        assert self.N1h * n2 == seq
        self.two_stage = n2 > 1


def _mm(a, b):
    return jnp.dot(a, b, preferred_element_type=F32)


def _mm_hi(a, b):
    return jnp.dot(a, b, preferred_element_type=F32, precision=lax.Precision.HIGHEST)


def _silu(x):
    return x * (1.0 / (1.0 + jnp.exp(-x)))


def _params(*sem):
    return pltpu.CompilerParams(dimension_semantics=sem, vmem_limit_bytes=VMEM_LIMIT)


def _rope_tables(seq):
    t = np.arange(seq)
    n_freq = HEAD_DIM // 4
    inv = ROPE_BASE ** (-np.arange(n_freq, dtype=np.float64) / n_freq)
    row = (t // GRID_W)[:, None] * inv
    col = (t % GRID_W)[:, None] * inv
    cos = np.concatenate([np.cos(row), np.cos(row), np.cos(col), np.cos(col)], axis=1)
    sin = np.concatenate([-np.sin(row), np.sin(row), -np.sin(col), np.sin(col)], axis=1)
    return (np.tile(cos, (1, 2)).astype(np.float32), np.tile(sin, (1, 2)).astype(np.float32))


def _conv_order(grp, table):
    return table.reshape(grp.N1h, grp.N2, -1).transpose(1, 0, 2).reshape(grp.L, -1)


def _filter_features(grp):
    L = grp.L
    t = np.linspace(0.0, 1.0, L)[:, None]
    bands = (FILTER_EMB - 1) // 2
    ang = (2.0 * math.pi / L) * np.arange(L)[:, None]
    fr = np.linspace(1e-4, bands - 1, bands)[None, :]
    feats = np.concatenate([t, np.cos(fr * ang), -np.sin(fr * ang)], axis=-1)
    feats = np.pad(feats, ((0, 0), (0, LANE - FILTER_EMB)))
    feats = _conv_order(grp, feats).astype(np.float32)
    return feats[:, 0:1], np.ascontiguousarray(feats.T)


def _decay_rates():
    max_decay = math.log(HYENA_TARGET) / FAST_DECAY_PCT
    min_decay = math.log(HYENA_TARGET) / SLOW_DECAY_PCT
    return np.abs(np.linspace(min_decay, max_decay, HYENA_W))[None, :].astype(np.float32)


def _dft_tables(grp):
    n1, n2 = grp.N1, grp.N2
    n = n1 * n2
    k = np.arange(n1)
    f = np.exp(-2j * np.pi * ((k[:, None] * k[None, :]) % n1) / n1)
    fh = f[:, : grp.N1h]
    fa = np.block([[fh.real, -fh.imag], [fh.imag, fh.real]])
    frev = np.concatenate([f[:, : grp.N1h], f[:, grp.N1h:][:, ::-1]], axis=1)
    faf = np.concatenate([frev.real, frev.imag], axis=0) / n
    g = None
    if grp.two_stage:
        j = np.arange(n2)
        ph = (k[:, None, None] * j[None, None, :] + n1 * j[None, :, None] * j[None, None, :]) % n
        gc = np.exp(-2j * np.pi * ph / n)
        g = np.concatenate([np.concatenate([gc.real, -gc.imag], axis=2),
                            np.concatenate([gc.imag, gc.real], axis=2)], axis=1)
        g = g.astype(np.float32)
    return fa.astype(np.float32), faf.astype(np.float32), g


def _mod_kernel(c_ref, w_ref, b_ref, o_ref):
    o_ref[...] = _mm_hi(_silu(c_ref[...]), w_ref[...]) + b_ref[...]


def _modulation(cond, mod_w, mod_b):
    nb = 3 * D_MODEL // 1024
    return pl.pallas_call(
        _mod_kernel,
        grid=(DEPTH, nb),
        in_specs=[
            pl.BlockSpec((16, D_MODEL), lambda l, j: (0, 0)),
            pl.BlockSpec((None, D_MODEL, 1024), lambda l, j: (l, 0, j)),
            pl.BlockSpec((None, 1, 1024), lambda l, j: (l, 0, j)),
        ],
        out_specs=pl.BlockSpec((None, 16, 1024), lambda l, j: (l, 0, j)),
        out_shape=jax.ShapeDtypeStruct((DEPTH, 16, 3 * D_MODEL), F32),
        compiler_params=_params("arbitrary", "arbitrary"),
        name="modulation",
    )(cond, mod_w, mod_b.reshape(DEPTH, 1, 3 * D_MODEL))


def _lane_lt64(shape):
    return lax.broadcasted_iota(jnp.int32, shape, 1) < HEAD_DIM


def _rope(x, cos, sin):
    lane = lax.broadcasted_iota(jnp.int32, x.shape, 1)
    first = (lane % 32) < 16
    partner = jnp.where(first, pltpu.roll(x, LANE - 16, axis=1), pltpu.roll(x, 16, axis=1))
    return x * cos + partner * sin


def _in_proj_kernel(*refs, latent, seq):
    if latent:
        x_ref, xp_ref, xn_ref, mod_ref, g_ref, w_ref, wc_ref, cos_ref, sin_ref = refs[:9]
        qq_ref, kk_ref, vv_ref, ga_ref, cv_ref, tmp_ref = refs[9:]
        first_row = pl.program_id(1) * ROWS
    else:
        x_ref, mod_ref, g_ref, w_ref, wc_ref = refs[:5]
        qq_ref, kk_ref, vv_ref, ga_ref, cv_ref, kf_ref, vf_ref = refs[7:]
        first_row = 0
    rope = latent

    shift = mod_ref[:, 0:D_MODEL]
    scale = mod_ref[:, D_MODEL:2 * D_MODEL]

    def modulated_norm(x):
        ms = jnp.mean(x * x, axis=-1, keepdims=True)
        y = x * lax.rsqrt(ms + RMS_EPS) * g_ref[...]
        return (y * (1.0 + scale) + shift).astype(BF16)

    h = modulated_norm(x_ref[...].reshape(ROWS, D_MODEL))
    h_ext = (jnp.concatenate([h, modulated_norm(jnp.concatenate([xp_ref[...], xn_ref[...]], axis=0))], axis=0)
             if latent else h)

    if rope:
        cos = cos_ref[...]
        sin = sin_ref[...]
    lt64 = _lane_lt64((ROWS, LANE))

    def put(ref, lo, val):
        lead = ref.shape[:-1]
        width = val.shape[-1]
        ref[(slice(None),) * len(lead) + (slice(lo, lo + width),)] = val.reshape(lead + (width,))

    def project_rows(a, cols):
        if latent:
            return _mm(a, w_ref[:, cols])
        return jnp.concatenate([_mm(a[r:r + seq], w_ref[:, cols]) for r in range(0, ROWS, seq)], axis=0)

    yq = project_rows(h, slice(C_Q, C_Q + ATTN_W)) * (HEAD_DIM ** -0.5)
    for j in range(ATTN_W // LANE):
        xj = yq[:, j * LANE:(j + 1) * LANE]
        rj = _rope(xj, cos, sin) if rope else xj
        even = jnp.where(lt64, rj, pltpu.roll(xj, HEAD_DIM, axis=1))
        odd = jnp.where(lt64, pltpu.roll(rj, HEAD_DIM, axis=1), xj)
        put(qq_ref, (2 * j) * LANE, even.astype(BF16))
        put(qq_ref, (2 * j + 1) * LANE, odd.astype(BF16))

    ykv = project_rows(h, slice(C_K, C_V + KV_W))
    yk, yv = ykv[:, :KV_W], ykv[:, KV_W:]
    rk = _rope(yk, cos, sin) if rope else yk
    put(kk_ref, 0, jnp.where(lt64, rk, 0.0).astype(BF16))
    put(kk_ref, LANE, jnp.where(lt64, pltpu.roll(rk, HEAD_DIM, axis=1), 0.0).astype(BF16))
    put(vv_ref, 0, yv.astype(BF16))
    if not latent:
        put(kf_ref, 0, yk)
        put(vf_ref, 0, yv)

    put(ga_ref, 0, _silu(project_rows(h, slice(C_GA, C_GA + ATTN_W))).astype(BF16))

    wd = HYENA_W
    col = lambda k, n=1: slice(C_CV + k * wd, C_CV + (k + n) * wd)
    sub = lax.broadcasted_iota(jnp.int32, (8, 1), 0)

    def set_row(arr, r, value):
        g0 = r - r % 8
        fixed = jnp.where(sub == r % 8, value, arr[g0:g0 + 8])
        parts = ([arr[:g0]] if g0 else []) + [fixed] + ([arr[g0 + 8:]] if g0 + 8 < arr.shape[0] else [])
        return jnp.concatenate(parts, axis=0)

    def dwconv3(u, u_halo, k):
        prev = pltpu.roll(u, 1, axis=0)
        nxt = pltpu.roll(u, ROWS - 1, axis=0)
        if latent:
            prev = set_row(prev, 0, jnp.where(first_row == 0, 0.0, u_halo[7:8]))
            nxt = set_row(nxt, ROWS - 1, jnp.where(first_row == seq - ROWS, 0.0, u_halo[8:9]))
        else:
            for s in range(ROWS // seq):
                prev = set_row(prev, s * seq, 0.0)
                nxt = set_row(nxt, (s + 1) * seq - 1, 0.0)
        return prev * wc_ref[0:1, k, :] + u * wc_ref[1:2, k, :] + nxt * wc_ref[2:3, k, :]

    def emit(k, val):
        if not latent:
            put(cv_ref, k * wd, val.astype(BF16))
            return
        n2 = cv_ref.shape[0]
        nsl = wd // LANE
        for sl in range(nsl):
            for i in range(N1_STEP):
                tmp_ref[k * nsl + sl, i * ROW_PITCH:i * ROW_PITCH + n2, :] = (
                    val[i * n2:(i + 1) * n2, sl * LANE:(sl + 1) * LANE])
        for j in range(n2):
            for sl in range(nsl):
                lo = k * wd + sl * LANE
                cv_ref[j, :, lo:lo + LANE] = (
                    tmp_ref[k * nsl + sl, pl.ds(j, N1_STEP, stride=ROW_PITCH), :].astype(BF16))

    def project(cols):
        y = project_rows(h_ext, cols)
        return (y[:ROWS], y[ROWS:]) if latent else (y, None)

    def part(y_halo, lo):
        return None if y_halo is None else y_halo[:, lo:lo + wd]

    ya, ya_halo = project(col(0, 2))
    emit(0, dwconv3(ya[:, :wd], part(ya_halo, 0), 0))
    emit(1, dwconv3(ya[:, wd:], part(ya_halo, wd), 1))
    yb, yb_halo = project(col(2, 2))
    emit(2, dwconv3(yb[:, :wd], part(yb_halo, 0), 2) * _silu(yb[:, wd:]))
    yc, yc_halo = project(col(4, 2))
    yd, yd_halo = project(col(6, 2))
    inner_halo = part(yc_halo, wd) * part(yd_halo, 0) if latent else None
    inner = dwconv3(yc[:, wd:] * yd[:, :wd], inner_halo, 3)
    emit(3, yc[:, :wd] * inner * _silu(yd[:, wd:]))


def _in_proj(grp, x, mod, g, w, wconv, layer, rope_tabs=None, caches=None):
    B, L = grp.B, grp.L
    cvw = 4 * HYENA_W
    const2 = lambda *_: (0, 0)
    wspec = pl.BlockSpec((None, D_MODEL, IN_COLS), lambda *_: (layer, 0, 0))
    wcspec = pl.BlockSpec((3, 4, HYENA_W), lambda *_: (0, 0, 0))
    aliases = {}
    if grp.two_stage:
        n1h, n2 = grp.N1h, grp.N2
        assert ROWS == N1_STEP * n2
        nblk = L // ROWS
        grid = (B, nblk)
        nat = lambda wd: pl.BlockSpec((None, ROWS, wd), lambda b, j: (b, j, 0))
        nat_shape = lambda wd, dt: jax.ShapeDtypeStruct((B, L, wd), dt)
        x8 = x.reshape(B, L // 8, 8, D_MODEL)
        per8 = ROWS // 8
        in_specs = [nat(D_MODEL),
                    pl.BlockSpec((None, None, 8, D_MODEL), lambda b, j: (b, jnp.maximum(j * per8 - 1, 0), 0, 0)),
                    pl.BlockSpec((None, None, 8, D_MODEL),
                                 lambda b, j: (b, jnp.minimum((j + 1) * per8, L // 8 - 1), 0, 0)),
                    pl.BlockSpec((None, 1, 3 * D_MODEL), lambda b, j: (b, 0, 0)),
                    pl.BlockSpec((1, D_MODEL), const2),
                    wspec, wcspec,
                    pl.BlockSpec((ROWS, LANE), lambda b, j: (j, 0)),
                    pl.BlockSpec((ROWS, LANE), lambda b, j: (j, 0))]
        args = [x, x8, x8, mod, g, w, wconv, rope_tabs[0], rope_tabs[1]]
        out_specs = [nat(2 * ATTN_W), nat(2 * LANE), nat(KV_W), nat(ATTN_W),
                     pl.BlockSpec((None, n2, N1_STEP, cvw), lambda b, j: (b, 0, j, 0))]
        out_shape = [nat_shape(2 * ATTN_W, BF16), nat_shape(2 * LANE, BF16), nat_shape(KV_W, BF16),
                     nat_shape(ATTN_W, BF16), jax.ShapeDtypeStruct((B, n2, n1h, cvw), BF16)]
        scratch = [pltpu.VMEM((cvw // LANE, N1_STEP * ROW_PITCH, LANE), F32)]
        sem = ("arbitrary", "arbitrary")
    else:
        bb = ROWS // L
        grid = (B // bb,)
        nat = lambda wd: pl.BlockSpec((bb, L, wd), lambda i: (i, 0, 0))
        nat_shape = lambda wd, dt: jax.ShapeDtypeStruct((B, L, wd), dt)
        in_specs = [nat(D_MODEL),
                    pl.BlockSpec((None, 1, 3 * D_MODEL), lambda i: (0, 0, 0)),
                    pl.BlockSpec((1, D_MODEL), const2),
                    wspec, wcspec,
                    pl.BlockSpec(memory_space=pl.ANY),
                    pl.BlockSpec(memory_space=pl.ANY)]
        args = [x, mod, g, w, wconv, caches[0], caches[1]]
        aliases = {5: 5, 6: 6}
        cache_spec = pl.BlockSpec((bb, None, L, KV_W), lambda i: (i, layer, 0, 0))
        cache_shape = jax.ShapeDtypeStruct((B, DEPTH, L, KV_W), F32)
        out_specs = [nat(2 * ATTN_W), nat(2 * LANE), nat(KV_W), nat(ATTN_W), nat(cvw), cache_spec, cache_spec]
        out_shape = [nat_shape(2 * ATTN_W, BF16), nat_shape(2 * LANE, BF16), nat_shape(KV_W, BF16),
                     nat_shape(ATTN_W, BF16), nat_shape(cvw, BF16), cache_shape, cache_shape]
        scratch = []
        sem = ("arbitrary",)
    outs = pl.pallas_call(
        functools.partial(_in_proj_kernel, latent=grp.two_stage, seq=L),
        grid=grid, in_specs=in_specs, out_specs=out_specs, out_shape=out_shape,
        scratch_shapes=scratch, input_output_aliases=aliases, compiler_params=_params(*sem),
        name="in_proj_lat" if grp.two_stage else "in_proj_ctx",
    )(*args)
    outs = list(outs)
    if not grp.two_stage:
        outs[4] = outs[4].reshape(B, 1, L, cvw)
    return outs


def _window_bias():
    qi = np.arange(BLOCK)[:, None]
    kj = np.arange(3 * BLOCK)[None, :]
    return np.where(np.abs(kj - BLOCK - qi) <= WINDOW, 0.0, NEG_INF).astype(np.float32)


def _attn_kernel(*refs, lq, local, nblocks, qblocks):
    if local:
        sink_ref, bias_ref, qq_ref, kk_ref, vv_ref, ck_ref, cv_ref, ga_ref, o_ref = refs
    else:
        sink_ref, qq_ref, kk_ref, vv_ref, ga_ref, o_ref = refs
    group = N_HEADS // N_KV_HEADS
    rows = group * lq
    nt = (((1,), (1,)), ((), ()))
    lt64 = _lane_lt64((lq, LANE))

    for qb in range(qblocks):
        qrows = slice(qb * lq, (qb + 1) * lq)
        if local:
            n = pl.program_id(1) * qblocks + qb
            starts = [pl.multiple_of(jnp.maximum(n - 1, 0) * BLOCK, BLOCK),
                      pl.multiple_of(n * BLOCK, BLOCK),
                      pl.multiple_of(jnp.minimum(n + 1, nblocks - 1) * BLOCK, BLOCK)]
            kj = lax.broadcasted_iota(jnp.int32, (1, 3 * BLOCK), 1)
            edge = (jnp.where((kj < BLOCK) & (n == 0), NEG_INF, 0.0)
                    + jnp.where((kj >= 2 * BLOCK) & (n == nblocks - 1), NEG_INF, 0.0))
            bias = bias_ref[...] + edge
        for g in range(N_KV_HEADS):
            lhs = jnp.concatenate(
                [qq_ref[qrows, (group * g + i) * LANE:(group * g + i + 1) * LANE] for i in range(group)], axis=0)
            gl = slice(g * LANE, (g + 1) * LANE)
            if local:
                kwin = jnp.concatenate([kk_ref[pl.ds(s, BLOCK), gl] for s in starts], axis=0)
                vals = jnp.concatenate([vv_ref[pl.ds(s, BLOCK), :] for s in starts] + [cv_ref[...]], axis=0)
                s_loc = lax.dot_general(lhs, kwin, nt, preferred_element_type=F32)
                s_ctx = lax.dot_general(lhs, ck_ref[:, gl], nt, preferred_element_type=F32)
            else:
                vals = vv_ref[qrows, :]
                s_loc = lax.dot_general(lhs, kk_ref[qrows, gl], nt, preferred_element_type=F32)
            row_head = lax.broadcasted_iota(jnp.int32, (rows, 1), 0) // lq
            snk = jnp.zeros((rows, 1), F32)
            for i in range(group):
                snk = jnp.where(row_head == i, sink_ref[group * g + i], snk)
            if local:
                s_loc = s_loc + jnp.concatenate([bias] * group, axis=0)
            m = jnp.maximum(jnp.max(s_loc, axis=-1, keepdims=True), snk)
            if local:
                m = jnp.maximum(m, jnp.max(s_ctx, axis=-1, keepdims=True))
            ones = jnp.ones((vals.shape[0], LANE), BF16)
            vext = jnp.concatenate([vals, ones], axis=1)
            nk = s_loc.shape[1]
            o = _mm(jnp.exp(s_loc - m).astype(BF16), vext[:nk])
            if local:
                o = o + _mm(jnp.exp(s_ctx - m).astype(BF16), vext[nk:])
            o = o[:, :LANE] / (o[:, LANE:] + jnp.exp(snk - m))
            for jj in range(group // 2):
                a = o[(2 * jj) * lq:(2 * jj + 1) * lq]
                b = o[(2 * jj + 1) * lq:(2 * jj + 2) * lq]
                if g == 0:
                    tile = jnp.where(lt64, a, pltpu.roll(b, HEAD_DIM, axis=1))
                else:
                    tile = jnp.where(lt64, pltpu.roll(a, HEAD_DIM, axis=1), b)
                j = (group // 2) * g + jj
                cols = slice(j * LANE, (j + 1) * LANE)
                o_ref[qrows, cols] = (tile * ga_ref[qrows, cols].astype(F32)).astype(BF16)


def _attention(grp, sink, qq, kk, vv, ga, layer=0, ctx=None):
    B, L = grp.B, grp.L
    smem = pl.BlockSpec(memory_space=pltpu.SMEM)
    if ctx is not None:
        ck, cv = ctx
        nb = L // BLOCK
        lc = ck.shape[2]
        qrows = ATTN_QBLOCKS * BLOCK
        grid = (B, nb // ATTN_QBLOCKS)
        blk = lambda wd: pl.BlockSpec((None, qrows, wd), lambda b, n: (b, n, 0))
        full = lambda rows, wd: pl.BlockSpec((None, rows, wd), lambda b, n: (b, 0, 0))
        cache = lambda wd: pl.BlockSpec((None, None, lc, wd), lambda b, n: (b, layer, 0, 0))
        in_specs = [smem, pl.BlockSpec((BLOCK, 3 * BLOCK), lambda b, n: (0, 0)),
                    blk(2 * ATTN_W), full(L, 2 * LANE), full(L, KV_W), cache(2 * LANE), cache(KV_W),
                    blk(ATTN_W)]
        args = [sink, jnp.asarray(_window_bias()), qq, kk, vv, ck, cv, ga]
        out_spec = blk(ATTN_W)
        out_shape = jax.ShapeDtypeStruct((B, L, ATTN_W), BF16)
        kern = functools.partial(_attn_kernel, lq=BLOCK, local=True, nblocks=nb, qblocks=ATTN_QBLOCKS)
        sem = ("arbitrary", "arbitrary")
    else:
        grid = (B // CTX_SEQS,)
        flat = lambda a: a.reshape(B * L, a.shape[-1])
        blk = lambda wd: pl.BlockSpec((CTX_SEQS * L, wd), lambda i: (i, 0))
        in_specs = [smem, blk(2 * ATTN_W), blk(2 * LANE), blk(KV_W), blk(ATTN_W)]
        args = [sink, flat(qq), flat(kk), flat(vv), flat(ga)]
        out_spec = blk(ATTN_W)
        out_shape = jax.ShapeDtypeStruct((B * L, ATTN_W), BF16)
        kern = functools.partial(_attn_kernel, lq=L, local=False, nblocks=1, qblocks=CTX_SEQS)
        sem = ("arbitrary",)
    out = pl.pallas_call(
        kern, grid=grid, in_specs=in_specs, out_specs=out_spec, out_shape=out_shape,
        compiler_params=_params(*sem), name="attn_lat" if ctx is not None else "attn_ctx",
    )(*args)
    return out.reshape(B, L, ATTN_W)


def _filter_kernel(t_ref, f_ref, w1_ref, b1_ref, w2_ref, b2_ref, w3_ref, fq_ref, dl_ref,
                   kf_ref, kb_ref, *, chunk):
    rows = t_ref.shape[0]
    width = 2 * HYENA_W
    fq = fq_ref[...]
    nn = (((0,), (0,)), ((), ()))

    def body(i, acc):
        r0 = pl.multiple_of(i * chunk, chunk)
        h = jnp.sin(fq * (_mm_hi(w1_ref[...], f_ref[:, pl.ds(r0, chunk)]) + b1_ref[...]))
        h = jnp.sin(fq * (_mm_hi(w2_ref[...], h) + b2_ref[...]))
        dec = jnp.exp(-(t_ref[pl.ds(r0, chunk), :] * dl_ref[...]))
        dec = jnp.concatenate([dec, dec], axis=1)
        hb = h.astype(BF16)
        kf = lax.dot_general(hb, w3_ref[:, 0:width].astype(BF16), nn, preferred_element_type=F32) * dec
        kb = lax.dot_general(hb, w3_ref[:, width:2 * width].astype(BF16), nn, preferred_element_type=F32) * dec
        rid = lax.broadcasted_iota(jnp.int32, kb.shape, 0) + r0
        kb = jnp.where(rid == 0, 0.0, kb)
        kf_ref[pl.ds(r0, chunk), :] = kf
        kb_ref[pl.ds(r0, chunk), :] = kb
        return acc + jnp.sum(jnp.abs(kf) + jnp.abs(kb), axis=0, keepdims=True)

    total = lax.fori_loop(0, rows // chunk, body, jnp.zeros((1, width), F32))

    def scale(i, c):
        r0 = pl.multiple_of(i * chunk, chunk)
        kf_ref[pl.ds(r0, chunk), :] = kf_ref[pl.ds(r0, chunk), :] / total
        kb_ref[pl.ds(r0, chunk), :] = kb_ref[pl.ds(r0, chunk), :] / total
        return c

    lax.fori_loop(0, rows // chunk, scale, 0)


def _filters(grp, feats, w1, b1, w2, b2, w3, fq, deltas):
    L = grp.L
    chunk = min(512, L)
    c2 = lambda l: (0, 0)
    per = lambda a, b: pl.BlockSpec((None, a, b), lambda l: (l, 0, 0))
    shape = jax.ShapeDtypeStruct((DEPTH, L, 2 * HYENA_W), F32)
    return pl.pallas_call(
        functools.partial(_filter_kernel, chunk=chunk), grid=(DEPTH,),
        in_specs=[pl.BlockSpec((L, 1), c2), pl.BlockSpec((LANE, L), c2),
                  per(FILTER_HIDDEN, LANE), per(FILTER_HIDDEN, 1), per(FILTER_HIDDEN, FILTER_HIDDEN),
                  per(FILTER_HIDDEN, 1), per(FILTER_HIDDEN, 4 * HYENA_W), per(FILTER_HIDDEN, 1),
                  pl.BlockSpec((1, HYENA_W), c2)],
        out_specs=[per(L, 2 * HYENA_W)] * 2, out_shape=[shape] * 2,
        compiler_params=_params("arbitrary"),
        name="filters_lat" if grp.two_stage else "filters_ctx",
    )(*feats, w1, b1, w2, b2, w3, fq, deltas)


def _store_spectrum_rows(s_ref, base, a, n1):
    for comp in range(2):
        for sl in range(2):
            s_ref[comp, sl, pl.ds(base, n1), :] = a[comp * n1:(comp + 1) * n1, sl * LANE:(sl + 1) * LANE]


def _load_column(s_ref, k1, n2, pitch):
    parts = [jnp.concatenate([s_ref[comp, sl, pl.ds(k1, n2, stride=pitch), :] for sl in range(2)], axis=1)
             for comp in range(2)]
    return jnp.concatenate(parts, axis=0).astype(BF16)


def _spectrum2_kernel(kf_ref, kb_ref, faf_ref, g_ref, o_ref, s_ref, *, n1, n2):
    n1h = n1 // 2
    pitch = n1 + PITCH_PAD
    faf = faf_ref[...]

    def stage_a(i, c):
        for t in range(UNROLL):
            j = i * UNROLL + t
            r0 = pl.multiple_of(j * n1h, n1h)
            rb = pl.multiple_of(((n2 - j) % n2) * n1h, n1h)
            back = kb_ref[pl.ds(rb, n1h), :]
            back = jnp.where(j == 0, pltpu.roll(back, n1h - 1, axis=0), back)
            rhs = jnp.concatenate([kf_ref[pl.ds(r0, n1h), :], back], axis=0).astype(BF16)
            _store_spectrum_rows(s_ref, pl.multiple_of(j * pitch, 8), _mm(faf, rhs), n1)
        return c

    lax.fori_loop(0, n2 // UNROLL, stage_a, 0)

    def stage_c(i, c):
        for t in range(UNROLL_WIDE):
            k1 = i * UNROLL_WIDE + t
            o_ref[k1] = _mm(g_ref[k1], _load_column(s_ref, k1, n2, pitch)).astype(BF16)
        return c

    lax.fori_loop(0, n1 // UNROLL_WIDE, stage_c, 0)


def _spectrum1_kernel(kf_ref, kb_ref, faf_ref, o_ref):
    back = kb_ref[...]
    rhs = jnp.concatenate([kf_ref[...], pltpu.roll(back, back.shape[0] - 1, axis=0)], axis=0).astype(BF16)
    o_ref[...] = _mm(faf_ref[...], rhs).astype(BF16)


def _spectrum(grp, kf, kb, faf, g):
    n1, n2, L = grp.N1, grp.N2, grp.L
    wd = 2 * LANE
    nblk = 2 * HYENA_W // wd
    kin = pl.BlockSpec((None, L, wd), lambda l, c: (l, 0, c))
    if grp.two_stage:
        pitch = n1 + PITCH_PAD
        return pl.pallas_call(
            functools.partial(_spectrum2_kernel, n1=n1, n2=n2), grid=(DEPTH, nblk),
            in_specs=[kin, kin, pl.BlockSpec((2 * n1, n1), lambda l, c: (0, 0)),
                      pl.BlockSpec((n1, 2 * n2, 2 * n2), lambda l, c: (0, 0, 0))],
            out_specs=pl.BlockSpec((None, n1, 2 * n2, wd), lambda l, c: (l, 0, 0, c)),
            out_shape=jax.ShapeDtypeStruct((DEPTH, n1, 2 * n2, 2 * HYENA_W), BF16),
            scratch_shapes=[pltpu.VMEM((2, 2, n2 * pitch, LANE), F32)],
            compiler_params=_params("arbitrary", "arbitrary"), name="spectrum_lat",
        )(kf, kb, faf, g)
    return pl.pallas_call(
        _spectrum1_kernel, grid=(DEPTH, nblk),
        in_specs=[kin, kin, pl.BlockSpec((2 * n1, n1), lambda l, c: (0, 0))],
        out_specs=pl.BlockSpec((None, 2 * n1, wd), lambda l, c: (l, 0, c)),
        out_shape=jax.ShapeDtypeStruct((DEPTH, 2 * n1, 2 * HYENA_W), BF16),
        compiler_params=_params("arbitrary", "arbitrary"), name="spectrum_ctx",
    )(kf, kb, faf)


def _pair_operand(u_ref, j):
    nb = u_ref.shape[0]
    re = jnp.concatenate([u_ref[b, j] for b in range(0, nb, 2)], axis=1)
    im = jnp.concatenate([u_ref[b, j] for b in range(1, nb, 2)], axis=1)
    return jnp.concatenate([re, im], axis=0)


def _cmul(x, k, half):
    k = k.astype(F32)
    npair = x.shape[1] // LANE
    kr = jnp.concatenate([k[:half]] * npair, axis=1)
    ki = jnp.concatenate([k[half:]] * npair, axis=1)
    xr, xi = x[:half], x[half:]
    return jnp.concatenate([xr * kr - xi * ki, xr * ki + xi * kr], axis=0).astype(BF16)


def _conv_epilogue(u_ref, m_ref, o_ref, d, y, j, n1h):
    for b in range(u_ref.shape[0]):
        yb = y[(b % 2) * n1h:(b % 2 + 1) * n1h, (b // 2) * LANE:(b // 2 + 1) * LANE]
        u = u_ref[b, j].astype(F32)
        o_ref[b, j] = (m_ref[b, j].astype(F32) * (yb + u * d)).astype(BF16)


def _conv2_kernel(u_ref, m_ref, fa_ref, fat_ref, g_ref, k_ref, d_ref, o_ref, s_ref, *, n1, n2):
    n1h = n1 // 2
    pitch = n1 + PITCH_PAD
    fa = fa_ref[...]
    fat = fat_ref[...]
    d = d_ref[...]

    def stage_a(i, c):
        for t in range(UNROLL_WIDE):
            j = i * UNROLL_WIDE + t
            _store_spectrum_rows(s_ref, pl.multiple_of(j * pitch, 8), _mm(fa, _pair_operand(u_ref, j)), n1)
        return c

    lax.fori_loop(0, n2 // UNROLL_WIDE, stage_a, 0)

    def stage_c(i, c):
        ks = [i * UNROLL_WIDE + t for t in range(UNROLL_WIDE)]
        cols = [_load_column(s_ref, k1, n2, pitch) for k1 in ks]
        backs = []
        for k1, col in zip(ks, cols):
            g = g_ref[k1]
            y = _cmul(_mm(g, col), k_ref[k1], n2)
            backs.append(lax.dot_general(g, y, (((0,), (0,)), ((), ())), preferred_element_type=F32))
        for k1, back in zip(ks, backs):
            for comp in range(2):
                for sl in range(2):
                    s_ref[comp, sl, pl.ds(k1, n2, stride=pitch), :] = (
                        back[comp * n2:(comp + 1) * n2, sl * LANE:(sl + 1) * LANE])
        return c

    lax.fori_loop(0, n1 // UNROLL_WIDE, stage_c, 0)

    def stage_inv(i, c):
        for t in range(UNROLL):
            j = i * UNROLL + t
            base = pl.multiple_of(j * pitch, 8)
            parts = [jnp.concatenate([s_ref[comp, sl, pl.ds(base, n1), :] for sl in range(2)], axis=1)
                     for comp in range(2)]
            y = _mm(fat, jnp.concatenate(parts, axis=0).astype(BF16))
            _conv_epilogue(u_ref, m_ref, o_ref, d, y, j, n1h)
        return c

    lax.fori_loop(0, n2 // UNROLL, stage_inv, 0)


def _conv1_kernel(u_ref, m_ref, fa_ref, fat_ref, k_ref, d_ref, o_ref, *, n1):
    x = _mm(fa_ref[...], _pair_operand(u_ref, 0))
    y = _mm(fat_ref[...], _cmul(x, k_ref[...], n1))
    _conv_epilogue(u_ref, m_ref, o_ref, d_ref[...], y, 0, n1 // 2)


def _long_conv(grp, u, m, spec, d, layer, order, tabs):
    B, n1, n2, n1h = grp.B, grp.N1, grp.N2, grp.N1h
    ncb = HYENA_W // LANE
    nbatch = 4 if grp.two_stage else CTX_CONV_SEQS
    grid = (ncb, B // nbatch)
    group = lambda k: pl.BlockSpec((nbatch, n2, n1h, LANE), lambda c, q: (q, 0, 0, k * ncb + c))
    (u, ku), (m, km) = u, m
    uspec, mspec = group(ku), group(km)
    data = group(0)
    dspec = pl.BlockSpec((None, 1, LANE), lambda c, q: (layer, 0, order * ncb + c))
    fa = pl.BlockSpec((2 * n1, n1), lambda c, q: (0, 0))
    fat = pl.BlockSpec((n1, 2 * n1), lambda c, q: (0, 0))
    out_shape = jax.ShapeDtypeStruct((B, n2, n1h, HYENA_W), BF16)
    if grp.two_stage:
        pitch = n1 + PITCH_PAD
        return pl.pallas_call(
            functools.partial(_conv2_kernel, n1=n1, n2=n2), grid=grid,
            in_specs=[uspec, mspec, fa, fat,
                      pl.BlockSpec((n1, 2 * n2, 2 * n2), lambda c, q: (0, 0, 0)),
                      pl.BlockSpec((None, n1, 2 * n2, LANE), lambda c, q: (layer, 0, 0, order * ncb + c)),
                      dspec],
            out_specs=data, out_shape=out_shape,
            scratch_shapes=[pltpu.VMEM((2, 2, n2 * pitch, LANE), F32)],
            compiler_params=_params("arbitrary", "arbitrary"), name="long_conv_lat",
        )(u, m, tabs["fa"], tabs["fat"], tabs["g"], spec, d)
    return pl.pallas_call(
        functools.partial(_conv1_kernel, n1=n1), grid=grid,
        in_specs=[uspec, mspec, fa, fat,
                  pl.BlockSpec((None, 2 * n1, LANE), lambda c, q: (layer, 0, order * ncb + c)), dspec],
        out_specs=data, out_shape=out_shape,
        compiler_params=_params("arbitrary", "arbitrary"), name="long_conv_ctx",
    )(u, m, tabs["fa"], tabs["fat"], spec, d)


def _out_proj_kernel(*refs, permute, final):
    if permute:
        x_ref, mod_ref, a_ref, z_ref, s_ref, w_ref, fg_ref, o_ref, tmp_ref = refs
    else:
        x_ref, mod_ref, a_ref, z_ref, s_ref, w_ref, fg_ref, o_ref = refs
    a = a_ref[...].reshape(ROWS, ATTN_W)
    nsl = HYENA_W // LANE
    if permute:
        n2 = z_ref.shape[0]
        for j in range(n2):
            for sl in range(nsl):
                cols = slice(sl * LANE, (sl + 1) * LANE)
                tmp_ref[sl, pl.ds(j, N1_STEP, stride=ROW_PITCH), :] = z_ref[j, :, cols].astype(F32)
                tmp_ref[nsl + sl, pl.ds(j, N1_STEP, stride=ROW_PITCH), :] = s_ref[j, :, cols].astype(F32)
        conv = jnp.concatenate(
            [jnp.concatenate([tmp_ref[i, k * ROW_PITCH:k * ROW_PITCH + n2, :] for k in range(N1_STEP)], axis=0)
             for i in range(2 * nsl)], axis=1).astype(BF16)
    else:
        conv = jnp.concatenate([z_ref[...].reshape(ROWS, HYENA_W), s_ref[...].reshape(ROWS, CONV_W)], axis=1)
    y = _mm(a, w_ref[0:ATTN_W, :]) + _mm(conv, w_ref[ATTN_W:, :])
    gate = mod_ref[:, 2 * D_MODEL:3 * D_MODEL]
    xn = x_ref[...].reshape(ROWS, D_MODEL) + gate * y
    if final:
        ms = jnp.mean(xn * xn, axis=-1, keepdims=True)
        xn = xn * lax.rsqrt(ms + RMS_EPS) * fg_ref[...]
    o_ref[...] = xn.reshape(o_ref.shape)


def _out_proj(grp, x, mod, attn, zg, scg, w, layer, final_g, final):
    B, L = grp.B, grp.L
    const2 = lambda *_: (0, 0)
    wspec = pl.BlockSpec((None, D_MODEL, D_MODEL), lambda *_: (layer, 0, 0))
    (zg, kz), (scg, ks) = zg, scg
    if grp.two_stage:
        n2 = grp.N2
        grid = (B, L // ROWS)
        nat = lambda wd: pl.BlockSpec((None, ROWS, wd), lambda b, j: (b, j, 0))
        cvl = lambda k: pl.BlockSpec((None, n2, N1_STEP, HYENA_W), lambda b, j: (b, 0, j, k))
        in_specs = [nat(D_MODEL), pl.BlockSpec((None, 1, 3 * D_MODEL), lambda b, j: (b, 0, 0)),
                    nat(ATTN_W), cvl(kz), cvl(ks), wspec, pl.BlockSpec((1, D_MODEL), const2)]
        scratch = [pltpu.VMEM((2 * HYENA_W // LANE, N1_STEP * ROW_PITCH, LANE), F32)]
        sem = ("arbitrary", "arbitrary")
    else:
        bb = ROWS // L
        grid = (B // bb,)
        nat = lambda wd: pl.BlockSpec((bb, L, wd), lambda i: (i, 0, 0))
        cvl = lambda k: pl.BlockSpec((bb, None, L, HYENA_W), lambda i: (i, 0, 0, k))
        in_specs = [nat(D_MODEL), pl.BlockSpec((None, 1, 3 * D_MODEL), lambda i: (0, 0, 0)),
                    nat(ATTN_W), cvl(kz), cvl(ks), wspec, pl.BlockSpec((1, D_MODEL), const2)]
        scratch = []
        sem = ("arbitrary",)
    return pl.pallas_call(
        functools.partial(_out_proj_kernel, permute=grp.two_stage, final=final),
        grid=grid, in_specs=in_specs, out_specs=nat(D_MODEL),
        out_shape=jax.ShapeDtypeStruct((B, L, D_MODEL), F32),
        scratch_shapes=scratch, compiler_params=_params(*sem),
        name="out_proj_lat" if grp.two_stage else "out_proj_ctx",
    )(x, mod, attn, zg, scg, w, final_g)


def _group_tables(grp):
    fa, faf, g = _dft_tables(grp)
    tabs = {"fa": jnp.asarray(fa).astype(BF16), "fat": jnp.asarray(fa.T.copy()).astype(BF16),
            "faf": jnp.asarray(faf).astype(BF16)}
    if g is not None:
        tabs["g"] = jnp.asarray(g).astype(BF16)
    tabs["feats"] = tuple(jnp.asarray(a) for a in _filter_features(grp))
    return tabs


def kernel(x_prompt, x_sample, cache_k, cache_v, c, c_ctx, norm_g, mod_w, mod_b, w_in, attn_sink,
           hy_conv_w, hy_filt_w1, hy_filt_b1, hy_filt_w2, hy_filt_b2, hy_filt_w3, hy_filt_freq, hy_d,
           sc_conv_w, w_out, final_g):
    ctx = Group(x_prompt.shape[0], x_prompt.shape[1], 2 * x_prompt.shape[1], 1)
    lat = Group(x_sample.shape[0], x_sample.shape[1], 128, 2 * x_sample.shape[1] // 128)
    nlat = lat.B

    cond = jnp.zeros((16, D_MODEL), F32).at[0].set(c_ctx).at[1:1 + nlat].set(c)
    mods = _modulation(cond, mod_w, mod_b)

    w1t = jnp.pad(hy_filt_w1, ((0, 0), (0, LANE - FILTER_EMB), (0, 0))).transpose(0, 2, 1)
    w2t = hy_filt_w2.transpose(0, 2, 1)
    b1 = hy_filt_b1[:, :, None]
    b2 = hy_filt_b2[:, :, None]
    fq = hy_filt_freq[:, :, None]
    deltas = jnp.asarray(_decay_rates())
    tabs, spec = {}, {}
    for grp in (ctx, lat):
        t = _group_tables(grp)
        kf, kb = _filters(grp, t["feats"], w1t, b1, w2t, b2, hy_filt_w3, fq, deltas)
        spec[grp] = _spectrum(grp, kf, kb, t["faf"], t.get("g"))
        tabs[grp] = t
    rope = tuple(jnp.asarray(a) for a in _rope_tables(lat.L))

    w_in_b = w_in.astype(BF16)
    w_out_b = w_out.astype(BF16)
    wconv = jnp.concatenate([hy_conv_w.reshape(DEPTH, 3, 3, HYENA_W), sc_conv_w[:, :, None, :]], axis=2)
    dskip = hy_d.reshape(DEPTH, 1, 2 * HYENA_W)
    fg = final_g[None, :]
    lc = cache_k.shape[2]
    zpad = jnp.zeros((nlat, DEPTH, lc, HEAD_DIM), F32)
    ck_pad = jnp.concatenate([zpad, cache_k[:, :, :, 0], zpad, cache_k[:, :, :, 1]], axis=-1).astype(BF16)
    cv_nat = cache_v.reshape(nlat, DEPTH, lc, KV_W).astype(BF16)

    xp, xs = x_prompt, x_sample
    caches = (jnp.zeros((ctx.B, DEPTH, ctx.L, KV_W), F32), jnp.zeros((ctx.B, DEPTH, ctx.L, KV_W), F32))
    for l in range(DEPTH):
        g = norm_g[l][None, :]
        last = l == DEPTH - 1
        for grp in (ctx, lat):
            is_lat = grp is lat
            x = xs if is_lat else xp
            mod = mods[l, 1:1 + nlat][:, None, :] if is_lat else mods[l, 0:1][:, None, :]
            if is_lat:
                qq, kk, vv, ga, cv = _in_proj(grp, x, mod, g, w_in_b, wconv[l], l, rope_tabs=rope)
                attn = _attention(grp, attn_sink[l], qq, kk, vv, ga, l, ctx=(ck_pad, cv_nat))
            else:
                qq, kk, vv, ga, cv, *caches = _in_proj(grp, x, mod, g, w_in_b, wconv[l], l, caches=caches)
                attn = _attention(grp, attn_sink[l], qq, kk, vv, ga)
            z1 = _long_conv(grp, (cv, 0), (cv, 1), spec[grp], dskip, l, 0, tabs[grp])
            zg = _long_conv(grp, (z1, 0), (cv, 2), spec[grp], dskip, l, 1, tabs[grp])
            xn = _out_proj(grp, x, mod, attn, (zg, 0), (cv, 3), w_out_b, l, fg, last)
            if is_lat:
                xs = xn
            else:
                xp = xn
    shape = (ctx.B, DEPTH, ctx.L, N_KV_HEADS, HEAD_DIM)
    return (xp, xs, caches[0].reshape(shape), caches[1].reshape(shape))
```

```python
import functools
import math

import numpy as np
import jax
import jax.numpy as jnp
from jax import lax
from jax.experimental import pallas as pl
from jax.experimental.pallas import tpu as pltpu

F32 = jnp.float32
BF16 = jnp.bfloat16

D_MODEL = 1024
DEPTH = 4
GRID_W = 64
N_HEADS = 8
N_KV_HEADS = 2
HEAD_DIM = 64
ATTN_W = N_HEADS * HEAD_DIM
KV_W = N_KV_HEADS * HEAD_DIM
HYENA_W = 256
CONV_W = 256
WINDOW = 128
BLOCK = 128
FILTER_EMB = 33
FILTER_HIDDEN = 64
HYENA_TARGET = 1e-2
FAST_DECAY_PCT = 0.3
SLOW_DECAY_PCT = 1.5
ROPE_BASE = 10000.0
RMS_EPS = 1e-6
NEG_INF = -1e30
IN_COLS = 3328
C_Q, C_K, C_V, C_GA, C_CV = 0, 512, 640, 768, 1280

LANE = 128
ROWS = 1024
N1_STEP = 16
ROW_PITCH = 68
VMEM_LIMIT = 56 * 1024 * 1024
PITCH_PAD = 4
UNROLL = 8
UNROLL_WIDE = 16
ATTN_QBLOCKS = 8
CTX_CONV_SEQS = 16
CTX_SEQS = 2


class Group:
    def __init__(self, batch, seq, n1, n2):
        self.B, self.L, self.N1, self.N2 = batch, seq, n1, n2
        self.N1h = n1 // 2
        assert self.N1h * n2 == seq
        self.two_stage = n2 > 1


def _mm(a, b):
    return jnp.dot(a, b, preferred_element_type=F32)


def _mm_hi(a, b):
    return jnp.dot(a, b, preferred_element_type=F32, precision=lax.Precision.HIGHEST)


def _silu(x):
    return x * (1.0 / (1.0 + jnp.exp(-x)))


def _params(*sem):
    return pltpu.CompilerParams(dimension_semantics=sem, vmem_limit_bytes=VMEM_LIMIT)


def _rope_tables(seq):
    t = np.arange(seq)
    n_freq = HEAD_DIM // 4
    inv = ROPE_BASE ** (-np.arange(n_freq, dtype=np.float64) / n_freq)
    row = (t // GRID_W)[:, None] * inv
    col = (t % GRID_W)[:, None] * inv
    cos = np.concatenate([np.cos(row), np.cos(row), np.cos(col), np.cos(col)], axis=1)
    sin = np.concatenate([-np.sin(row), np.sin(row), -np.sin(col), np.sin(col)], axis=1)
    return (np.tile(cos, (1, 2)).astype(np.float32), np.tile(sin, (1, 2)).astype(np.float32))


def _conv_order(grp, table):
    return table.reshape(grp.N1h, grp.N2, -1).transpose(1, 0, 2).reshape(grp.L, -1)


def _filter_features(grp):
    L = grp.L
    t = np.linspace(0.0, 1.0, L)[:, None]
    bands = (FILTER_EMB - 1) // 2
    ang = (2.0 * math.pi / L) * np.arange(L)[:, None]
    fr = np.linspace(1e-4, bands - 1, bands)[None, :]
    feats = np.concatenate([t, np.cos(fr * ang), -np.sin(fr * ang)], axis=-1)
    feats = np.pad(feats, ((0, 0), (0, LANE - FILTER_EMB)))
    feats = _conv_order(grp, feats).astype(np.float32)
    return feats[:, 0:1], np.ascontiguousarray(feats.T)


def _decay_rates():
    max_decay = math.log(HYENA_TARGET) / FAST_DECAY_PCT
    min_decay = math.log(HYENA_TARGET) / SLOW_DECAY_PCT
    return np.abs(np.linspace(min_decay, max_decay, HYENA_W))[None, :].astype(np.float32)


def _dft_tables(grp):
    n1, n2 = grp.N1, grp.N2
    n = n1 * n2
    k = np.arange(n1)
    f = np.exp(-2j * np.pi * ((k[:, None] * k[None, :]) % n1) / n1)
    fh = f[:, : grp.N1h]
    fa = np.block([[fh.real, -fh.imag], [fh.imag, fh.real]])
    frev = np.concatenate([f[:, : grp.N1h], f[:, grp.N1h:][:, ::-1]], axis=1)
    faf = np.concatenate([frev.real, frev.imag], axis=0) / n
    g = None
    if grp.two_stage:
        j = np.arange(n2)
        ph = (k[:, None, None] * j[None, None, :] + n1 * j[None, :, None] * j[None, None, :]) % n
        gc = np.exp(-2j * np.pi * ph / n)
        g = np.concatenate([np.concatenate([gc.real, -gc.imag], axis=2),
                            np.concatenate([gc.imag, gc.real], axis=2)], axis=1)
        g = g.astype(np.float32)
    return fa.astype(np.float32), faf.astype(np.float32), g


def _mod_kernel(c_ref, w_ref, b_ref, o_ref):
    o_ref[...] = _mm_hi(_silu(c_ref[...]), w_ref[...]) + b_ref[...]


def _modulation(cond, mod_w, mod_b):
    nb = 3 * D_MODEL // 1024
    return pl.pallas_call(
        _mod_kernel,
        grid=(DEPTH, nb),
        in_specs=[
            pl.BlockSpec((16, D_MODEL), lambda l, j: (0, 0)),
            pl.BlockSpec((None, D_MODEL, 1024), lambda l, j: (l, 0, j)),
            pl.BlockSpec((None, 1, 1024), lambda l, j: (l, 0, j)),
        ],
        out_specs=pl.BlockSpec((None, 16, 1024), lambda l, j: (l, 0, j)),
        out_shape=jax.ShapeDtypeStruct((DEPTH, 16, 3 * D_MODEL), F32),
        compiler_params=_params("arbitrary", "arbitrary"),
        name="modulation",
    )(cond, mod_w, mod_b.reshape(DEPTH, 1, 3 * D_MODEL))


def _lane_lt64(shape):
    return lax.broadcasted_iota(jnp.int32, shape, 1) < HEAD_DIM


def _rope(x, cos, sin):
    lane = lax.broadcasted_iota(jnp.int32, x.shape, 1)
    first = (lane % 32) < 16
    partner = jnp.where(first, pltpu.roll(x, LANE - 16, axis=1), pltpu.roll(x, 16, axis=1))
    return x * cos + partner * sin


def _in_proj_kernel(*refs, latent, seq):
    if latent:
        x_ref, xp_ref, xn_ref, mod_ref, g_ref, w_ref, wc_ref, cos_ref, sin_ref = refs[:9]
        qq_ref, kk_ref, vv_ref, ga_ref, cv_ref, tmp_ref = refs[9:]
        first_row = pl.program_id(1) * ROWS
    else:
        x_ref, mod_ref, g_ref, w_ref, wc_ref = refs[:5]
        qq_ref, kk_ref, vv_ref, ga_ref, cv_ref, kf_ref, vf_ref = refs[7:]
        first_row = 0
    rope = latent

    shift = mod_ref[:, 0:D_MODEL]
    scale = mod_ref[:, D_MODEL:2 * D_MODEL]

    def modulated_norm(x):
        ms = jnp.mean(x * x, axis=-1, keepdims=True)
        y = x * lax.rsqrt(ms + RMS_EPS) * g_ref[...]
        return (y * (1.0 + scale) + shift).astype(BF16)

    h = modulated_norm(x_ref[...].reshape(ROWS, D_MODEL))
    h_ext = (jnp.concatenate([h, modulated_norm(jnp.concatenate([xp_ref[...], xn_ref[...]], axis=0))], axis=0)
             if latent else h)

    if rope:
        cos = cos_ref[...]
        sin = sin_ref[...]
    lt64 = _lane_lt64((ROWS, LANE))

    def put(ref, lo, val):
        lead = ref.shape[:-1]
        width = val.shape[-1]
        ref[(slice(None),) * len(lead) + (slice(lo, lo + width),)] = val.reshape(lead + (width,))

    def project_rows(a, cols):
        if latent:
            return _mm(a, w_ref[:, cols])
        return jnp.concatenate([_mm(a[r:r + seq], w_ref[:, cols]) for r in range(0, ROWS, seq)], axis=0)

    yq = project_rows(h, slice(C_Q, C_Q + ATTN_W)) * (HEAD_DIM ** -0.5)
    for j in range(ATTN_W // LANE):
        xj = yq[:, j * LANE:(j + 1) * LANE]
        rj = _rope(xj, cos, sin) if rope else xj
        even = jnp.where(lt64, rj, pltpu.roll(xj, HEAD_DIM, axis=1))
        odd = jnp.where(lt64, pltpu.roll(rj, HEAD_DIM, axis=1), xj)
        put(qq_ref, (2 * j) * LANE, even.astype(BF16))
        put(qq_ref, (2 * j + 1) * LANE, odd.astype(BF16))

    ykv = project_rows(h, slice(C_K, C_V + KV_W))
    yk, yv = ykv[:, :KV_W], ykv[:, KV_W:]
    rk = _rope(yk, cos, sin) if rope else yk
    put(kk_ref, 0, jnp.where(lt64, rk, 0.0).astype(BF16))
    put(kk_ref, LANE, jnp.where(lt64, pltpu.roll(rk, HEAD_DIM, axis=1), 0.0).astype(BF16))
    put(vv_ref, 0, yv.astype(BF16))
    if not latent:
        put(kf_ref, 0, yk)
        put(vf_ref, 0, yv)

    put(ga_ref, 0, _silu(project_rows(h, slice(C_GA, C_GA + ATTN_W))).astype(BF16))

    wd = HYENA_W
    col = lambda k, n=1: slice(C_CV + k * wd, C_CV + (k + n) * wd)
    sub = lax.broadcasted_iota(jnp.int32, (8, 1), 0)

    def set_row(arr, r, value):
        g0 = r - r % 8
        fixed = jnp.where(sub == r % 8, value, arr[g0:g0 + 8])
        parts = ([arr[:g0]] if g0 else []) + [fixed] + ([arr[g0 + 8:]] if g0 + 8 < arr.shape[0] else [])
        return jnp.concatenate(parts, axis=0)

    def dwconv3(u, u_halo, k):
        prev = pltpu.roll(u, 1, axis=0)
        nxt = pltpu.roll(u, ROWS - 1, axis=0)
        if latent:
            prev = set_row(prev, 0, jnp.where(first_row == 0, 0.0, u_halo[7:8]))
            nxt = set_row(nxt, ROWS - 1, jnp.where(first_row == seq - ROWS, 0.0, u_halo[8:9]))
        else:
            for s in range(ROWS // seq):
                prev = set_row(prev, s * seq, 0.0)
                nxt = set_row(nxt, (s + 1) * seq - 1, 0.0)
        return prev * wc_ref[0:1, k, :] + u * wc_ref[1:2, k, :] + nxt * wc_ref[2:3, k, :]

    def emit(k, val):
        if not latent:
            put(cv_ref, k * wd, val.astype(BF16))
            return
        n2 = cv_ref.shape[0]
        nsl = wd // LANE
        for sl in range(nsl):
            for i in range(N1_STEP):
                tmp_ref[k * nsl + sl, i * ROW_PITCH:i * ROW_PITCH + n2, :] = (
                    val[i * n2:(i + 1) * n2, sl * LANE:(sl + 1) * LANE])
        for j in range(n2):
            for sl in range(nsl):
                lo = k * wd + sl * LANE
                cv_ref[j, :, lo:lo + LANE] = (
                    tmp_ref[k * nsl + sl, pl.ds(j, N1_STEP, stride=ROW_PITCH), :].astype(BF16))

    def project(cols):
        y = project_rows(h_ext, cols)
        return (y[:ROWS], y[ROWS:]) if latent else (y, None)

    def part(y_halo, lo):
        return None if y_halo is None else y_halo[:, lo:lo + wd]

    ya, ya_halo = project(col(0, 2))
    emit(0, dwconv3(ya[:, :wd], part(ya_halo, 0), 0))
    emit(1, dwconv3(ya[:, wd:], part(ya_halo, wd), 1))
    yb, yb_halo = project(col(2, 2))
    emit(2, dwconv3(yb[:, :wd], part(yb_halo, 0), 2) * _silu(yb[:, wd:]))
    yc, yc_halo = project(col(4, 2))
    yd, yd_halo = project(col(6, 2))
    inner_halo = part(yc_halo, wd) * part(yd_halo, 0) if latent else None
    inner = dwconv3(yc[:, wd:] * yd[:, :wd], inner_halo, 3)
    emit(3, yc[:, :wd] * inner * _silu(yd[:, wd:]))


def _in_proj(grp, x, mod, g, w, wconv, layer, rope_tabs=None, caches=None):
    B, L = grp.B, grp.L
    cvw = 4 * HYENA_W
    const2 = lambda *_: (0, 0)
    wspec = pl.BlockSpec((None, D_MODEL, IN_COLS), lambda *_: (layer, 0, 0))
    wcspec = pl.BlockSpec((3, 4, HYENA_W), lambda *_: (0, 0, 0))
    aliases = {}
    if grp.two_stage:
        n1h, n2 = grp.N1h, grp.N2
        assert ROWS == N1_STEP * n2
        nblk = L // ROWS
        grid = (B, nblk)
        nat = lambda wd: pl.BlockSpec((None, ROWS, wd), lambda b, j: (b, j, 0))
        nat_shape = lambda wd, dt: jax.ShapeDtypeStruct((B, L, wd), dt)
        x8 = x.reshape(B, L // 8, 8, D_MODEL)
        per8 = ROWS // 8
        in_specs = [nat(D_MODEL),
                    pl.BlockSpec((None, None, 8, D_MODEL), lambda b, j: (b, jnp.maximum(j * per8 - 1, 0), 0, 0)),
                    pl.BlockSpec((None, None, 8, D_MODEL),
                                 lambda b, j: (b, jnp.minimum((j + 1) * per8, L // 8 - 1), 0, 0)),
                    pl.BlockSpec((None, 1, 3 * D_MODEL), lambda b, j: (b, 0, 0)),
                    pl.BlockSpec((1, D_MODEL), const2),
                    wspec, wcspec,
                    pl.BlockSpec((ROWS, LANE), lambda b, j: (j, 0)),
                    pl.BlockSpec((ROWS, LANE), lambda b, j: (j, 0))]
        args = [x, x8, x8, mod, g, w, wconv, rope_tabs[0], rope_tabs[1]]
        out_specs = [nat(2 * ATTN_W), nat(2 * LANE), nat(KV_W), nat(ATTN_W),
                     pl.BlockSpec((None, n2, N1_STEP, cvw), lambda b, j: (b, 0, j, 0))]
        out_shape = [nat_shape(2 * ATTN_W, BF16), nat_shape(2 * LANE, BF16), nat_shape(KV_W, BF16),
                     nat_shape(ATTN_W, BF16), jax.ShapeDtypeStruct((B, n2, n1h, cvw), BF16)]
        scratch = [pltpu.VMEM((cvw // LANE, N1_STEP * ROW_PITCH, LANE), F32)]
        sem = ("arbitrary", "arbitrary")
    else:
        bb = ROWS // L
        grid = (B // bb,)
        nat = lambda wd: pl.BlockSpec((bb, L, wd), lambda i: (i, 0, 0))
        nat_shape = lambda wd, dt: jax.ShapeDtypeStruct((B, L, wd), dt)
        in_specs = [nat(D_MODEL),
                    pl.BlockSpec((None, 1, 3 * D_MODEL), lambda i: (0, 0, 0)),
                    pl.BlockSpec((1, D_MODEL), const2),
                    wspec, wcspec,
                    pl.BlockSpec(memory_space=pl.ANY),
                    pl.BlockSpec(memory_space=pl.ANY)]
        args = [x, mod, g, w, wconv, caches[0], caches[1]]
        aliases = {5: 5, 6: 6}
        cache_spec = pl.BlockSpec((bb, None, L, KV_W), lambda i: (i, layer, 0, 0))
        cache_shape = jax.ShapeDtypeStruct((B, DEPTH, L, KV_W), F32)
        out_specs = [nat(2 * ATTN_W), nat(2 * LANE), nat(KV_W), nat(ATTN_W), nat(cvw), cache_spec, cache_spec]
        out_shape = [nat_shape(2 * ATTN_W, BF16), nat_shape(2 * LANE, BF16), nat_shape(KV_W, BF16),
                     nat_shape(ATTN_W, BF16), nat_shape(cvw, BF16), cache_shape, cache_shape]
        scratch = []
        sem = ("arbitrary",)
    outs = pl.pallas_call(
        functools.partial(_in_proj_kernel, latent=grp.two_stage, seq=L),
        grid=grid, in_specs=in_specs, out_specs=out_specs, out_shape=out_shape,
        scratch_shapes=scratch, input_output_aliases=aliases, compiler_params=_params(*sem),
        name="in_proj_lat" if grp.two_stage else "in_proj_ctx",
    )(*args)
    outs = list(outs)
    if not grp.two_stage:
        outs[4] = outs[4].reshape(B, 1, L, cvw)
    return outs


def _window_bias():
    qi = np.arange(BLOCK)[:, None]
    kj = np.arange(3 * BLOCK)[None, :]
    return np.where(np.abs(kj - BLOCK - qi) <= WINDOW, 0.0, NEG_INF).astype(np.float32)


def _attn_kernel(*refs, lq, local, nblocks, qblocks):
    if local:
        sink_ref, bias_ref, qq_ref, kk_ref, vv_ref, ck_ref, cv_ref, ga_ref, o_ref = refs
    else:
        sink_ref, qq_ref, kk_ref, vv_ref, ga_ref, o_ref = refs
    group = N_HEADS // N_KV_HEADS
    rows = group * lq
    nt = (((1,), (1,)), ((), ()))
    lt64 = _lane_lt64((lq, LANE))

    for qb in range(qblocks):
        qrows = slice(qb * lq, (qb + 1) * lq)
        if local:
            n = pl.program_id(1) * qblocks + qb
            starts = [pl.multiple_of(jnp.maximum(n - 1, 0) * BLOCK, BLOCK),
                      pl.multiple_of(n * BLOCK, BLOCK),
                      pl.multiple_of(jnp.minimum(n + 1, nblocks - 1) * BLOCK, BLOCK)]
            kj = lax.broadcasted_iota(jnp.int32, (1, 3 * BLOCK), 1)
            edge = (jnp.where((kj < BLOCK) & (n == 0), NEG_INF, 0.0)
                    + jnp.where((kj >= 2 * BLOCK) & (n == nblocks - 1), NEG_INF, 0.0))
            bias = bias_ref[...] + edge
        for g in range(N_KV_HEADS):
            lhs = jnp.concatenate(
                [qq_ref[qrows, (group * g + i) * LANE:(group * g + i + 1) * LANE] for i in range(group)], axis=0)
            gl = slice(g * LANE, (g + 1) * LANE)
            if local:
                kwin = jnp.concatenate([kk_ref[pl.ds(s, BLOCK), gl] for s in starts], axis=0)
                vals = jnp.concatenate([vv_ref[pl.ds(s, BLOCK), :] for s in starts] + [cv_ref[...]], axis=0)
                s_loc = lax.dot_general(lhs, kwin, nt, preferred_element_type=F32)
                s_ctx = lax.dot_general(lhs, ck_ref[:, gl], nt, preferred_element_type=F32)
            else:
                vals = vv_ref[qrows, :]
                s_loc = lax.dot_general(lhs, kk_ref[qrows, gl], nt, preferred_element_type=F32)
            row_head = lax.broadcasted_iota(jnp.int32, (rows, 1), 0) // lq
            snk = jnp.zeros((rows, 1), F32)
            for i in range(group):
                snk = jnp.where(row_head == i, sink_ref[group * g + i], snk)
            if local:
                s_loc = s_loc + jnp.concatenate([bias] * group, axis=0)
            m = jnp.maximum(jnp.max(s_loc, axis=-1, keepdims=True), snk)
            if local:
                m = jnp.maximum(m, jnp.max(s_ctx, axis=-1, keepdims=True))
            ones = jnp.ones((vals.shape[0], LANE), BF16)
            vext = jnp.concatenate([vals, ones], axis=1)
            nk = s_loc.shape[1]
            o = _mm(jnp.exp(s_loc - m).astype(BF16), vext[:nk])
            if local:
                o = o + _mm(jnp.exp(s_ctx - m).astype(BF16), vext[nk:])
            o = o[:, :LANE] / (o[:, LANE:] + jnp.exp(snk - m))
            for jj in range(group // 2):
                a = o[(2 * jj) * lq:(2 * jj + 1) * lq]
                b = o[(2 * jj + 1) * lq:(2 * jj + 2) * lq]
                if g == 0:
                    tile = jnp.where(lt64, a, pltpu.roll(b, HEAD_DIM, axis=1))
                else:
                    tile = jnp.where(lt64, pltpu.roll(a, HEAD_DIM, axis=1), b)
                j = (group // 2) * g + jj
                cols = slice(j * LANE, (j + 1) * LANE)
                o_ref[qrows, cols] = (tile * ga_ref[qrows, cols].astype(F32)).astype(BF16)


def _attention(grp, sink, qq, kk, vv, ga, layer=0, ctx=None):
    B, L = grp.B, grp.L
    smem = pl.BlockSpec(memory_space=pltpu.SMEM)
    if ctx is not None:
        ck, cv = ctx
        nb = L // BLOCK
        lc = ck.shape[2]
        qrows = ATTN_QBLOCKS * BLOCK
        grid = (B, nb // ATTN_QBLOCKS)
        blk = lambda wd: pl.BlockSpec((None, qrows, wd), lambda b, n: (b, n, 0))
        full = lambda rows, wd: pl.BlockSpec((None, rows, wd), lambda b, n: (b, 0, 0))
        cache = lambda wd: pl.BlockSpec((None, None, lc, wd), lambda b, n: (b, layer, 0, 0))
        in_specs = [smem, pl.BlockSpec((BLOCK, 3 * BLOCK), lambda b, n: (0, 0)),
                    blk(2 * ATTN_W), full(L, 2 * LANE), full(L, KV_W), cache(2 * LANE), cache(KV_W),
                    blk(ATTN_W)]
        args = [sink, jnp.asarray(_window_bias()), qq, kk, vv, ck, cv, ga]
        out_spec = blk(ATTN_W)
        out_shape = jax.ShapeDtypeStruct((B, L, ATTN_W), BF16)
        kern = functools.partial(_attn_kernel, lq=BLOCK, local=True, nblocks=nb, qblocks=ATTN_QBLOCKS)
        sem = ("arbitrary", "arbitrary")
    else:
        grid = (B // CTX_SEQS,)
        flat = lambda a: a.reshape(B * L, a.shape[-1])
        blk = lambda wd: pl.BlockSpec((CTX_SEQS * L, wd), lambda i: (i, 0))
        in_specs = [smem, blk(2 * ATTN_W), blk(2 * LANE), blk(KV_W), blk(ATTN_W)]
        args = [sink, flat(qq), flat(kk), flat(vv), flat(ga)]
        out_spec = blk(ATTN_W)
        out_shape = jax.ShapeDtypeStruct((B * L, ATTN_W), BF16)
        kern = functools.partial(_attn_kernel, lq=L, local=False, nblocks=1, qblocks=CTX_SEQS)
        sem = ("arbitrary",)
    out = pl.pallas_call(
        kern, grid=grid, in_specs=in_specs, out_specs=out_spec, out_shape=out_shape,
        compiler_params=_params(*sem), name="attn_lat" if ctx is not None else "attn_ctx",
    )(*args)
    return out.reshape(B, L, ATTN_W)


def _filter_kernel(t_ref, f_ref, w1_ref, b1_ref, w2_ref, b2_ref, w3_ref, fq_ref, dl_ref,
                   kf_ref, kb_ref, *, chunk):
    rows = t_ref.shape[0]
    width = 2 * HYENA_W
    fq = fq_ref[...]
    nn = (((0,), (0,)), ((), ()))

    def body(i, acc):
        r0 = pl.multiple_of(i * chunk, chunk)
        h = jnp.sin(fq * (_mm_hi(w1_ref[...], f_ref[:, pl.ds(r0, chunk)]) + b1_ref[...]))
        h = jnp.sin(fq * (_mm_hi(w2_ref[...], h) + b2_ref[...]))
        dec = jnp.exp(-(t_ref[pl.ds(r0, chunk), :] * dl_ref[...]))
        dec = jnp.concatenate([dec, dec], axis=1)
        hb = h.astype(BF16)
        kf = lax.dot_general(hb, w3_ref[:, 0:width].astype(BF16), nn, preferred_element_type=F32) * dec
        kb = lax.dot_general(hb, w3_ref[:, width:2 * width].astype(BF16), nn, preferred_element_type=F32) * dec
        rid = lax.broadcasted_iota(jnp.int32, kb.shape, 0) + r0
        kb = jnp.where(rid == 0, 0.0, kb)
        kf_ref[pl.ds(r0, chunk), :] = kf
        kb_ref[pl.ds(r0, chunk), :] = kb
        return acc + jnp.sum(jnp.abs(kf) + jnp.abs(kb), axis=0, keepdims=True)

    total = lax.fori_loop(0, rows // chunk, body, jnp.zeros((1, width), F32))

    def scale(i, c):
        r0 = pl.multiple_of(i * chunk, chunk)
        kf_ref[pl.ds(r0, chunk), :] = kf_ref[pl.ds(r0, chunk), :] / total
        kb_ref[pl.ds(r0, chunk), :] = kb_ref[pl.ds(r0, chunk), :] / total
        return c

    lax.fori_loop(0, rows // chunk, scale, 0)


def _filters(grp, feats, w1, b1, w2, b2, w3, fq, deltas):
    L = grp.L
    chunk = min(512, L)
    c2 = lambda l: (0, 0)
    per = lambda a, b: pl.BlockSpec((None, a, b), lambda l: (l, 0, 0))
    shape = jax.ShapeDtypeStruct((DEPTH, L, 2 * HYENA_W), F32)
    return pl.pallas_call(
        functools.partial(_filter_kernel, chunk=chunk), grid=(DEPTH,),
        in_specs=[pl.BlockSpec((L, 1), c2), pl.BlockSpec((LANE, L), c2),
                  per(FILTER_HIDDEN, LANE), per(FILTER_HIDDEN, 1), per(FILTER_HIDDEN, FILTER_HIDDEN),
                  per(FILTER_HIDDEN, 1), per(FILTER_HIDDEN, 4 * HYENA_W), per(FILTER_HIDDEN, 1),
                  pl.BlockSpec((1, HYENA_W), c2)],
        out_specs=[per(L, 2 * HYENA_W)] * 2, out_shape=[shape] * 2,
        compiler_params=_params("arbitrary"),
        name="filters_lat" if grp.two_stage else "filters_ctx",
    )(*feats, w1, b1, w2, b2, w3, fq, deltas)


def _store_spectrum_rows(s_ref, base, a, n1):
    for comp in range(2):
        for sl in range(2):
            s_ref[comp, sl, pl.ds(base, n1), :] = a[comp * n1:(comp + 1) * n1, sl * LANE:(sl + 1) * LANE]


def _load_column(s_ref, k1, n2, pitch):
    parts = [jnp.concatenate([s_ref[comp, sl, pl.ds(k1, n2, stride=pitch), :] for sl in range(2)], axis=1)
             for comp in range(2)]
    return jnp.concatenate(parts, axis=0).astype(BF16)


def _spectrum2_kernel(kf_ref, kb_ref, faf_ref, g_ref, o_ref, s_ref, *, n1, n2):
    n1h = n1 // 2
    pitch = n1 + PITCH_PAD
    faf = faf_ref[...]

    def stage_a(i, c):
        for t in range(UNROLL):
            j = i * UNROLL + t
            r0 = pl.multiple_of(j * n1h, n1h)
            rb = pl.multiple_of(((n2 - j) % n2) * n1h, n1h)
            back = kb_ref[pl.ds(rb, n1h), :]
            back = jnp.where(j == 0, pltpu.roll(back, n1h - 1, axis=0), back)
            rhs = jnp.concatenate([kf_ref[pl.ds(r0, n1h), :], back], axis=0).astype(BF16)
            _store_spectrum_rows(s_ref, pl.multiple_of(j * pitch, PITCH_PAD), _mm(faf, rhs), n1)
        return c

    lax.fori_loop(0, n2 // UNROLL, stage_a, 0)

    def stage_c(i, c):
        for t in range(UNROLL_WIDE):
            k1 = i * UNROLL_WIDE + t
            o_ref[k1] = _mm(g_ref[k1], _load_column(s_ref, k1, n2, pitch)).astype(BF16)
        return c

    lax.fori_loop(0, n1 // UNROLL_WIDE, stage_c, 0)


def _spectrum1_kernel(kf_ref, kb_ref, faf_ref, o_ref):
    back = kb_ref[...]
    rhs = jnp.concatenate([kf_ref[...], pltpu.roll(back, back.shape[0] - 1, axis=0)], axis=0).astype(BF16)
    o_ref[...] = _mm(faf_ref[...], rhs).astype(BF16)


def _spectrum(grp, kf, kb, faf, g):
    n1, n2, L = grp.N1, grp.N2, grp.L
    wd = 2 * LANE
    nblk = 2 * HYENA_W // wd
    kin = pl.BlockSpec((None, L, wd), lambda l, c: (l, 0, c))
    if grp.two_stage:
        pitch = n1 + PITCH_PAD
        return pl.pallas_call(
            functools.partial(_spectrum2_kernel, n1=n1, n2=n2), grid=(DEPTH, nblk),
            in_specs=[kin, kin, pl.BlockSpec((2 * n1, n1), lambda l, c: (0, 0)),
                      pl.BlockSpec((n1, 2 * n2, 2 * n2), lambda l, c: (0, 0, 0))],
            out_specs=pl.BlockSpec((None, n1, 2 * n2, wd), lambda l, c: (l, 0, 0, c)),
            out_shape=jax.ShapeDtypeStruct((DEPTH, n1, 2 * n2, 2 * HYENA_W), BF16),
            scratch_shapes=[pltpu.VMEM((2, 2, n2 * pitch, LANE), F32)],
            compiler_params=_params("arbitrary", "arbitrary"), name="spectrum_lat",
        )(kf, kb, faf, g)
    return pl.pallas_call(
        _spectrum1_kernel, grid=(DEPTH, nblk),
        in_specs=[kin, kin, pl.BlockSpec((2 * n1, n1), lambda l, c: (0, 0))],
        out_specs=pl.BlockSpec((None, 2 * n1, wd), lambda l, c: (l, 0, c)),
        out_shape=jax.ShapeDtypeStruct((DEPTH, 2 * n1, 2 * HYENA_W), BF16),
        compiler_params=_params("arbitrary", "arbitrary"), name="spectrum_ctx",
    )(kf, kb, faf)


def _pair_operand(u_ref, j):
    nb = u_ref.shape[0]
    re = jnp.concatenate([u_ref[b, j] for b in range(0, nb, 2)], axis=1)
    im = jnp.concatenate([u_ref[b, j] for b in range(1, nb, 2)], axis=1)
    return jnp.concatenate([re, im], axis=0)


def _cmul(x, k, half):
    k = k.astype(F32)
    npair = x.shape[1] // LANE
    kr = jnp.concatenate([k[:half]] * npair, axis=1)
    ki = jnp.concatenate([k[half:]] * npair, axis=1)
    xr, xi = x[:half], x[half:]
    return jnp.concatenate([xr * kr - xi * ki, xr * ki + xi * kr], axis=0).astype(BF16)


def _conv_epilogue(u_ref, m_ref, o_ref, d, y, j, n1h):
    for b in range(u_ref.shape[0]):
        yb = y[(b % 2) * n1h:(b % 2 + 1) * n1h, (b // 2) * LANE:(b // 2 + 1) * LANE]
        u = u_ref[b, j].astype(F32)
        o_ref[b, j] = (m_ref[b, j].astype(F32) * (yb + u * d)).astype(BF16)


def _conv2_kernel(u_ref, m_ref, fa_ref, fat_ref, g_ref, k_ref, d_ref, o_ref, s_ref, *, n1, n2):
    n1h = n1 // 2
    pitch = n1 + PITCH_PAD
    fa = fa_ref[...]
    fat = fat_ref[...]
    d = d_ref[...]

    def stage_a(i, c):
        for t in range(UNROLL_WIDE):
            j = i * UNROLL_WIDE + t
            _store_spectrum_rows(s_ref, pl.multiple_of(j * pitch, PITCH_PAD), _mm(fa, _pair_operand(u_ref, j)), n1)
        return c

    lax.fori_loop(0, n2 // UNROLL_WIDE, stage_a, 0)

    def stage_c(i, c):
        ks = [i * UNROLL_WIDE + t for t in range(UNROLL_WIDE)]
        cols = [_load_column(s_ref, k1, n2, pitch) for k1 in ks]
        backs = []
        for k1, col in zip(ks, cols):
            g = g_ref[k1]
            y = _cmul(_mm(g, col), k_ref[k1], n2)
            backs.append(lax.dot_general(g, y, (((0,), (0,)), ((), ())), preferred_element_type=F32))
        for k1, back in zip(ks, backs):
            for comp in range(2):
                for sl in range(2):
                    s_ref[comp, sl, pl.ds(k1, n2, stride=pitch), :] = (
                        back[comp * n2:(comp + 1) * n2, sl * LANE:(sl + 1) * LANE])
        return c

    lax.fori_loop(0, n1 // UNROLL_WIDE, stage_c, 0)

    def stage_inv(i, c):
        for t in range(UNROLL):
            j = i * UNROLL + t
            base = pl.multiple_of(j * pitch, PITCH_PAD)
            parts = [jnp.concatenate([s_ref[comp, sl, pl.ds(base, n1), :] for sl in range(2)], axis=1)
                     for comp in range(2)]
            y = _mm(fat, jnp.concatenate(parts, axis=0).astype(BF16))
            _conv_epilogue(u_ref, m_ref, o_ref, d, y, j, n1h)
        return c

    lax.fori_loop(0, n2 // UNROLL, stage_inv, 0)


def _conv1_kernel(u_ref, m_ref, fa_ref, fat_ref, k_ref, d_ref, o_ref, *, n1):
    x = _mm(fa_ref[...], _pair_operand(u_ref, 0))
    y = _mm(fat_ref[...], _cmul(x, k_ref[...], n1))
    _conv_epilogue(u_ref, m_ref, o_ref, d_ref[...], y, 0, n1 // 2)


def _long_conv(grp, u, m, spec, d, layer, order, tabs):
    B, n1, n2, n1h = grp.B, grp.N1, grp.N2, grp.N1h
    ncb = HYENA_W // LANE
    nbatch = 4 if grp.two_stage else CTX_CONV_SEQS
    grid = (ncb, B // nbatch)
    group = lambda k: pl.BlockSpec((nbatch, n2, n1h, LANE), lambda c, q: (q, 0, 0, k * ncb + c))
    (u, ku), (m, km) = u, m
    uspec, mspec = group(ku), group(km)
    data = group(0)
    dspec = pl.BlockSpec((None, 1, LANE), lambda c, q: (layer, 0, order * ncb + c))
    fa = pl.BlockSpec((2 * n1, n1), lambda c, q: (0, 0))
    fat = pl.BlockSpec((n1, 2 * n1), lambda c, q: (0, 0))
    out_shape = jax.ShapeDtypeStruct((B, n2, n1h, HYENA_W), BF16)
    if grp.two_stage:
        pitch = n1 + PITCH_PAD
        return pl.pallas_call(
            functools.partial(_conv2_kernel, n1=n1, n2=n2), grid=grid,
            in_specs=[uspec, mspec, fa, fat,
                      pl.BlockSpec((n1, 2 * n2, 2 * n2), lambda c, q: (0, 0, 0)),
                      pl.BlockSpec((None, n1, 2 * n2, LANE), lambda c, q: (layer, 0, 0, order * ncb + c)),
                      dspec],
            out_specs=data, out_shape=out_shape,
            scratch_shapes=[pltpu.VMEM((2, 2, n2 * pitch, LANE), F32)],
            compiler_params=_params("arbitrary", "arbitrary"), name="long_conv_lat",
        )(u, m, tabs["fa"], tabs["fat"], tabs["g"], spec, d)
    return pl.pallas_call(
        functools.partial(_conv1_kernel, n1=n1), grid=grid,
        in_specs=[uspec, mspec, fa, fat,
                  pl.BlockSpec((None, 2 * n1, LANE), lambda c, q: (layer, 0, order * ncb + c)), dspec],
        out_specs=data, out_shape=out_shape,
        compiler_params=_params("arbitrary", "arbitrary"), name="long_conv_ctx",
    )(u, m, tabs["fa"], tabs["fat"], spec, d)


def _out_proj_kernel(*refs, permute, final):
    if permute:
        x_ref, mod_ref, a_ref, z_ref, s_ref, w_ref, fg_ref, o_ref, tmp_ref = refs
    else:
        x_ref, mod_ref, a_ref, z_ref, s_ref, w_ref, fg_ref, o_ref = refs
    a = a_ref[...].reshape(ROWS, ATTN_W)
    nsl = HYENA_W // LANE
    if permute:
        n2 = z_ref.shape[0]
        for j in range(n2):
            for sl in range(nsl):
                cols = slice(sl * LANE, (sl + 1) * LANE)
                tmp_ref[sl, pl.ds(j, N1_STEP, stride=ROW_PITCH), :] = z_ref[j, :, cols].astype(F32)
                tmp_ref[nsl + sl, pl.ds(j, N1_STEP, stride=ROW_PITCH), :] = s_ref[j, :, cols].astype(F32)
        conv = jnp.concatenate(
            [jnp.concatenate([tmp_ref[i, k * ROW_PITCH:k * ROW_PITCH + n2, :] for k in range(N1_STEP)], axis=0)
             for i in range(2 * nsl)], axis=1).astype(BF16)
    else:
        conv = jnp.concatenate([z_ref[...].reshape(ROWS, HYENA_W), s_ref[...].reshape(ROWS, CONV_W)], axis=1)
    y = _mm(a, w_ref[0:ATTN_W, :]) + _mm(conv, w_ref[ATTN_W:, :])
    gate = mod_ref[:, 2 * D_MODEL:3 * D_MODEL]
    xn = x_ref[...].reshape(ROWS, D_MODEL) + gate * y
    if final:
        ms = jnp.mean(xn * xn, axis=-1, keepdims=True)
        xn = xn * lax.rsqrt(ms + RMS_EPS) * fg_ref[...]
    o_ref[...] = xn.reshape(o_ref.shape)


def _out_proj(grp, x, mod, attn, zg, scg, w, layer, final_g, final):
    B, L = grp.B, grp.L
    const2 = lambda *_: (0, 0)
    wspec = pl.BlockSpec((None, D_MODEL, D_MODEL), lambda *_: (layer, 0, 0))
    (zg, kz), (scg, ks) = zg, scg
    if grp.two_stage:
        n2 = grp.N2
        grid = (B, L // ROWS)
        nat = lambda wd: pl.BlockSpec((None, ROWS, wd), lambda b, j: (b, j, 0))
        cvl = lambda k: pl.BlockSpec((None, n2, N1_STEP, HYENA_W), lambda b, j: (b, 0, j, k))
        in_specs = [nat(D_MODEL), pl.BlockSpec((None, 1, 3 * D_MODEL), lambda b, j: (b, 0, 0)),
                    nat(ATTN_W), cvl(kz), cvl(ks), wspec, pl.BlockSpec((1, D_MODEL), const2)]
        scratch = [pltpu.VMEM((2 * HYENA_W // LANE, N1_STEP * ROW_PITCH, LANE), F32)]
        sem = ("arbitrary", "arbitrary")
    else:
        bb = ROWS // L
        grid = (B // bb,)
        nat = lambda wd: pl.BlockSpec((bb, L, wd), lambda i: (i, 0, 0))
        cvl = lambda k: pl.BlockSpec((bb, None, L, HYENA_W), lambda i: (i, 0, 0, k))
        in_specs = [nat(D_MODEL), pl.BlockSpec((None, 1, 3 * D_MODEL), lambda i: (0, 0, 0)),
                    nat(ATTN_W), cvl(kz), cvl(ks), wspec, pl.BlockSpec((1, D_MODEL), const2)]
        scratch = []
        sem = ("arbitrary",)
    return pl.pallas_call(
        functools.partial(_out_proj_kernel, permute=grp.two_stage, final=final),
        grid=grid, in_specs=in_specs, out_specs=nat(D_MODEL),
        out_shape=jax.ShapeDtypeStruct((B, L, D_MODEL), F32),
        scratch_shapes=scratch, compiler_params=_params(*sem),
        name="out_proj_lat" if grp.two_stage else "out_proj_ctx",
    )(x, mod, attn, zg, scg, w, final_g)


def _group_tables(grp):
    fa, faf, g = _dft_tables(grp)
    tabs = {"fa": jnp.asarray(fa).astype(BF16), "fat": jnp.asarray(fa.T.copy()).astype(BF16),
            "faf": jnp.asarray(faf).astype(BF16)}
    if g is not None:
        tabs["g"] = jnp.asarray(g).astype(BF16)
    tabs["feats"] = tuple(jnp.asarray(a) for a in _filter_features(grp))
    return tabs


def kernel(x_prompt, x_sample, cache_k, cache_v, c, c_ctx, norm_g, mod_w, mod_b, w_in, attn_sink,
           hy_conv_w, hy_filt_w1, hy_filt_b1, hy_filt_w2, hy_filt_b2, hy_filt_w3, hy_filt_freq, hy_d,
           sc_conv_w, w_out, final_g):
    ctx = Group(x_prompt.shape[0], x_prompt.shape[1], 2 * x_prompt.shape[1], 1)
    lat = Group(x_sample.shape[0], x_sample.shape[1], 128, 2 * x_sample.shape[1] // 128)
    nlat = lat.B

    cond = jnp.zeros((16, D_MODEL), F32).at[0].set(c_ctx).at[1:1 + nlat].set(c)
    mods = _modulation(cond, mod_w, mod_b)

    w1t = jnp.pad(hy_filt_w1, ((0, 0), (0, LANE - FILTER_EMB), (0, 0))).transpose(0, 2, 1)
    w2t = hy_filt_w2.transpose(0, 2, 1)
    b1 = hy_filt_b1[:, :, None]
    b2 = hy_filt_b2[:, :, None]
    fq = hy_filt_freq[:, :, None]
    deltas = jnp.asarray(_decay_rates())
    tabs, spec = {}, {}
    for grp in (ctx, lat):
        t = _group_tables(grp)
        kf, kb = _filters(grp, t["feats"], w1t, b1, w2t, b2, hy_filt_w3, fq, deltas)
        spec[grp] = _spectrum(grp, kf, kb, t["faf"], t.get("g"))
        tabs[grp] = t
    rope = tuple(jnp.asarray(a) for a in _rope_tables(lat.L))

    w_in_b = w_in.astype(BF16)
    w_out_b = w_out.astype(BF16)
    wconv = jnp.concatenate([hy_conv_w.reshape(DEPTH, 3, 3, HYENA_W), sc_conv_w[:, :, None, :]], axis=2)
    dskip = hy_d.reshape(DEPTH, 1, 2 * HYENA_W)
    fg = final_g[None, :]
    lc = cache_k.shape[2]
    zpad = jnp.zeros((nlat, DEPTH, lc, HEAD_DIM), F32)
    ck_pad = jnp.concatenate([zpad, cache_k[:, :, :, 0], zpad, cache_k[:, :, :, 1]], axis=-1).astype(BF16)
    cv_nat = cache_v.reshape(nlat, DEPTH, lc, KV_W).astype(BF16)

    xp, xs = x_prompt, x_sample
    caches = (jnp.zeros((ctx.B, DEPTH, ctx.L, KV_W), F32), jnp.zeros((ctx.B, DEPTH, ctx.L, KV_W), F32))
    for l in range(DEPTH):
        g = norm_g[l][None, :]
        last = l == DEPTH - 1
        for grp in (ctx, lat):
            is_lat = grp is lat
            x = xs if is_lat else xp
            mod = mods[l, 1:1 + nlat][:, None, :] if is_lat else mods[l, 0:1][:, None, :]
            if is_lat:
                qq, kk, vv, ga, cv = _in_proj(grp, x, mod, g, w_in_b, wconv[l], l, rope_tabs=rope)
                attn = _attention(grp, attn_sink[l], qq, kk, vv, ga, l, ctx=(ck_pad, cv_nat))
            else:
                qq, kk, vv, ga, cv, *caches = _in_proj(grp, x, mod, g, w_in_b, wconv[l], l, caches=caches)
                attn = _attention(grp, attn_sink[l], qq, kk, vv, ga)
            z1 = _long_conv(grp, (cv, 0), (cv, 1), spec[grp], dskip, l, 0, tabs[grp])
            zg = _long_conv(grp, (z1, 0), (cv, 2), spec[grp], dskip, l, 1, tabs[grp])
            xn = _out_proj(grp, x, mod, attn, (zg, 0), (cv, 3), w_out_b, l, fg, last)
            if is_lat:
                xs = xn
            else:
                xp = xn
    shape = (ctx.B, DEPTH, ctx.L, N_KV_HEADS, HEAD_DIM)
    return (xp, xs, caches[0].reshape(shape), caches[1].reshape(shape))
```

```python
import functools
import math

import numpy as np
import jax
import jax.numpy as jnp
from jax import lax
from jax.experimental import pallas as pl
from jax.experimental.pallas import tpu as pltpu

F32 = jnp.float32
BF16 = jnp.bfloat16

D_MODEL = 1024
DEPTH = 4
GRID_W = 64
N_HEADS = 8
N_KV_HEADS = 2
HEAD_DIM = 64
ATTN_W = N_HEADS * HEAD_DIM
KV_W = N_KV_HEADS * HEAD_DIM
HYENA_W = 256
CONV_W = 256
WINDOW = 128
BLOCK = 128
FILTER_EMB = 33
FILTER_HIDDEN = 64
HYENA_TARGET = 1e-2
FAST_DECAY_PCT = 0.3
SLOW_DECAY_PCT = 1.5
ROPE_BASE = 10000.0
RMS_EPS = 1e-6
NEG_INF = -1e30
IN_COLS = 3328
C_Q, C_K, C_V, C_GA, C_CV = 0, 512, 640, 768, 1280

LANE = 128
ROWS = 1024
N1_STEP = 16
ROW_PITCH = 68
VMEM_LIMIT = 56 * 1024 * 1024
PITCH_PAD = 4
UNROLL = 8
UNROLL_WIDE = 16
ATTN_QBLOCKS = 8
CTX_CONV_SEQS = 16
CTX_SEQS = 2


class Group:
    def __init__(self, batch, seq, n1, n2):
        self.B, self.L, self.N1, self.N2 = batch, seq, n1, n2
        self.N1h = n1 // 2
        assert self.N1h * n2 == seq
        self.two_stage = n2 > 1


def _mm(a, b):
    return jnp.dot(a, b, preferred_element_type=F32)


def _mm_hi(a, b):
    return jnp.dot(a, b, preferred_element_type=F32, precision=lax.Precision.HIGHEST)


def _silu(x):
    return x * (1.0 / (1.0 + jnp.exp(-x)))


def _params(*sem):
    return pltpu.CompilerParams(dimension_semantics=sem, vmem_limit_bytes=VMEM_LIMIT)


def _rope_tables(seq):
    t = np.arange(seq)
    n_freq = HEAD_DIM // 4
    inv = ROPE_BASE ** (-np.arange(n_freq, dtype=np.float64) / n_freq)
    row = (t // GRID_W)[:, None] * inv
    col = (t % GRID_W)[:, None] * inv
    cos = np.concatenate([np.cos(row), np.cos(row), np.cos(col), np.cos(col)], axis=1)
    sin = np.concatenate([-np.sin(row), np.sin(row), -np.sin(col), np.sin(col)], axis=1)
    return (np.tile(cos, (1, 2)).astype(np.float32), np.tile(sin, (1, 2)).astype(np.float32))


def _conv_order(grp, table):
    return table.reshape(grp.N1h, grp.N2, -1).transpose(1, 0, 2).reshape(grp.L, -1)


def _filter_features(grp):
    L = grp.L
    t = np.linspace(0.0, 1.0, L)[:, None]
    bands = (FILTER_EMB - 1) // 2
    ang = (2.0 * math.pi / L) * np.arange(L)[:, None]
    fr = np.linspace(1e-4, bands - 1, bands)[None, :]
    feats = np.concatenate([t, np.cos(fr * ang), -np.sin(fr * ang)], axis=-1)
    feats = np.pad(feats, ((0, 0), (0, LANE - FILTER_EMB)))
    feats = _conv_order(grp, feats).astype(np.float32)
    return feats[:, 0:1], np.ascontiguousarray(feats.T)


def _decay_rates():
    max_decay = math.log(HYENA_TARGET) / FAST_DECAY_PCT
    min_decay = math.log(HYENA_TARGET) / SLOW_DECAY_PCT
    return np.abs(np.linspace(min_decay, max_decay, HYENA_W))[None, :].astype(np.float32)


def _dft_tables(grp):
    n1, n2 = grp.N1, grp.N2
    n = n1 * n2
    k = np.arange(n1)
    f = np.exp(-2j * np.pi * ((k[:, None] * k[None, :]) % n1) / n1)
    fh = f[:, : grp.N1h]
    fa = np.block([[fh.real, -fh.imag], [fh.imag, fh.real]])
    frev = np.concatenate([f[:, : grp.N1h], f[:, grp.N1h:][:, ::-1]], axis=1)
    faf = np.concatenate([frev.real, frev.imag], axis=0) / n
    g = None
    if grp.two_stage:
        j = np.arange(n2)
        ph = (k[:, None, None] * j[None, None, :] + n1 * j[None, :, None] * j[None, None, :]) % n
        gc = np.exp(-2j * np.pi * ph / n)
        g = np.concatenate([np.concatenate([gc.real, -gc.imag], axis=2),
                            np.concatenate([gc.imag, gc.real], axis=2)], axis=1)
        g = g.astype(np.float32)
    return fa.astype(np.float32), faf.astype(np.float32), g


def _mod_kernel(c_ref, w_ref, b_ref, o_ref):
    s = _silu(c_ref[...])
    w = w_ref[...]
    s_hi = s.astype(BF16)
    s_lo = (s - s_hi.astype(F32)).astype(BF16)
    w_hi = w.astype(BF16)
    w_lo = (w - w_hi.astype(F32)).astype(BF16)
    rows = s.shape[0]
    both = _mm(jnp.concatenate([s_hi, s_lo], axis=0), w_hi)
    o_ref[...] = both[:rows] + both[rows:] + _mm(s_hi, w_lo) + b_ref[...]


def _modulation(cond, mod_w, mod_b):
    nb = 3 * D_MODEL // 1024
    return pl.pallas_call(
        _mod_kernel,
        grid=(DEPTH, nb),
        in_specs=[
            pl.BlockSpec((16, D_MODEL), lambda l, j: (0, 0)),
            pl.BlockSpec((None, D_MODEL, 1024), lambda l, j: (l, 0, j)),
            pl.BlockSpec((None, 1, 1024), lambda l, j: (l, 0, j)),
        ],
        out_specs=pl.BlockSpec((None, 16, 1024), lambda l, j: (l, 0, j)),
        out_shape=jax.ShapeDtypeStruct((DEPTH, 16, 3 * D_MODEL), F32),
        compiler_params=_params("arbitrary", "arbitrary"),
        name="modulation",
    )(cond, mod_w, mod_b.reshape(DEPTH, 1, 3 * D_MODEL))


def _lane_lt64(shape):
    return lax.broadcasted_iota(jnp.int32, shape, 1) < HEAD_DIM


def _rope(x, cos, sin):
    lane = lax.broadcasted_iota(jnp.int32, x.shape, 1)
    first = (lane % 32) < 16
    partner = jnp.where(first, pltpu.roll(x, LANE - 16, axis=1), pltpu.roll(x, 16, axis=1))
    return x * cos + partner * sin


def _in_proj_kernel(*refs, latent, seq):
    if latent:
        x_ref, xp_ref, xn_ref, mod_ref, g_ref, w_ref, wc_ref, cos_ref, sin_ref = refs[:9]
        qq_ref, kk_ref, vv_ref, ga_ref, cv_ref, tmp_ref = refs[9:]
        first_row = pl.program_id(1) * ROWS
    else:
        x_ref, mod_ref, g_ref, w_ref, wc_ref = refs[:5]
        qq_ref, kk_ref, vv_ref, ga_ref, cv_ref, kf_ref, vf_ref = refs[7:]
        first_row = 0
    rope = latent

    shift = mod_ref[:, 0:D_MODEL]
    scale = mod_ref[:, D_MODEL:2 * D_MODEL]

    def modulated_norm(x):
        ms = jnp.mean(x * x, axis=-1, keepdims=True)
        y = x * lax.rsqrt(ms + RMS_EPS) * g_ref[...]
        return (y * (1.0 + scale) + shift).astype(BF16)

    h = modulated_norm(x_ref[...].reshape(ROWS, D_MODEL))
    h_ext = (jnp.concatenate([h, modulated_norm(jnp.concatenate([xp_ref[...], xn_ref[...]], axis=0))], axis=0)
             if latent else h)

    if rope:
        cos = cos_ref[...]
        sin = sin_ref[...]
    lt64 = _lane_lt64((ROWS, LANE))

    def put(ref, lo, val):
        lead = ref.shape[:-1]
        width = val.shape[-1]
        ref[(slice(None),) * len(lead) + (slice(lo, lo + width),)] = val.reshape(lead + (width,))

    def project_rows(a, cols):
        if latent:
            return _mm(a, w_ref[:, cols])
        return jnp.concatenate([_mm(a[r:r + seq], w_ref[:, cols]) for r in range(0, ROWS, seq)], axis=0)

    yq = project_rows(h, slice(C_Q, C_Q + ATTN_W)) * (HEAD_DIM ** -0.5)
    for j in range(ATTN_W // LANE):
        xj = yq[:, j * LANE:(j + 1) * LANE]
        rj = _rope(xj, cos, sin) if rope else xj
        even = jnp.where(lt64, rj, pltpu.roll(xj, HEAD_DIM, axis=1))
        odd = jnp.where(lt64, pltpu.roll(rj, HEAD_DIM, axis=1), xj)
        put(qq_ref, (2 * j) * LANE, even.astype(BF16))
        put(qq_ref, (2 * j + 1) * LANE, odd.astype(BF16))

    ykv = project_rows(h, slice(C_K, C_V + KV_W))
    yk, yv = ykv[:, :KV_W], ykv[:, KV_W:]
    rk = _rope(yk, cos, sin) if rope else yk
    put(kk_ref, 0, jnp.where(lt64, rk, 0.0).astype(BF16))
    put(kk_ref, LANE, jnp.where(lt64, pltpu.roll(rk, HEAD_DIM, axis=1), 0.0).astype(BF16))
    put(vv_ref, 0, yv.astype(BF16))
    if not latent:
        put(kf_ref, 0, yk)
        put(vf_ref, 0, yv)

    put(ga_ref, 0, _silu(project_rows(h, slice(C_GA, C_GA + ATTN_W))).astype(BF16))

    wd = HYENA_W
    col = lambda k, n=1: slice(C_CV + k * wd, C_CV + (k + n) * wd)
    sub = lax.broadcasted_iota(jnp.int32, (8, 1), 0)

    def set_row(arr, r, value):
        g0 = r - r % 8
        fixed = jnp.where(sub == r % 8, value, arr[g0:g0 + 8])
        parts = ([arr[:g0]] if g0 else []) + [fixed] + ([arr[g0 + 8:]] if g0 + 8 < arr.shape[0] else [])
        return jnp.concatenate(parts, axis=0)

    def dwconv3(u, u_halo, k):
        prev = pltpu.roll(u, 1, axis=0)
        nxt = pltpu.roll(u, ROWS - 1, axis=0)
        if latent:
            prev = set_row(prev, 0, jnp.where(first_row == 0, 0.0, u_halo[7:8]))
            nxt = set_row(nxt, ROWS - 1, jnp.where(first_row == seq - ROWS, 0.0, u_halo[8:9]))
        else:
            for s in range(ROWS // seq):
                prev = set_row(prev, s * seq, 0.0)
                nxt = set_row(nxt, (s + 1) * seq - 1, 0.0)
        return prev * wc_ref[0:1, k, :] + u * wc_ref[1:2, k, :] + nxt * wc_ref[2:3, k, :]

    def emit(k, val):
        if not latent:
            put(cv_ref, k * wd, val.astype(BF16))
            return
        n2 = cv_ref.shape[0]
        nsl = wd // LANE
        for sl in range(nsl):
            for i in range(N1_STEP):
                tmp_ref[k * nsl + sl, i * ROW_PITCH:i * ROW_PITCH + n2, :] = (
                    val[i * n2:(i + 1) * n2, sl * LANE:(sl + 1) * LANE])
        for j in range(n2):
            for sl in range(nsl):
                lo = k * wd + sl * LANE
                cv_ref[j, :, lo:lo + LANE] = (
                    tmp_ref[k * nsl + sl, pl.ds(j, N1_STEP, stride=ROW_PITCH), :].astype(BF16))

    def project(cols):
        y = project_rows(h_ext, cols)
        return (y[:ROWS], y[ROWS:]) if latent else (y, None)

    def part(y_halo, lo):
        return None if y_halo is None else y_halo[:, lo:lo + wd]

    ya, ya_halo = project(col(0, 2))
    emit(0, dwconv3(ya[:, :wd], part(ya_halo, 0), 0))
    emit(1, dwconv3(ya[:, wd:], part(ya_halo, wd), 1))
    yb, yb_halo = project(col(2, 2))
    emit(2, dwconv3(yb[:, :wd], part(yb_halo, 0), 2) * _silu(yb[:, wd:]))
    yc, yc_halo = project(col(4, 2))
    yd, yd_halo = project(col(6, 2))
    inner_halo = part(yc_halo, wd) * part(yd_halo, 0) if latent else None
    inner = dwconv3(yc[:, wd:] * yd[:, :wd], inner_halo, 3)
    emit(3, yc[:, :wd] * inner * _silu(yd[:, wd:]))


def _in_proj(grp, x, mod, g, w, wconv, layer, rope_tabs=None, caches=None):
    B, L = grp.B, grp.L
    cvw = 4 * HYENA_W
    const2 = lambda *_: (0, 0)
    wspec = pl.BlockSpec((None, D_MODEL, IN_COLS), lambda *_: (layer, 0, 0))
    wcspec = pl.BlockSpec((3, 4, HYENA_W), lambda *_: (0, 0, 0))
    aliases = {}
    if grp.two_stage:
        n1h, n2 = grp.N1h, grp.N2
        assert ROWS == N1_STEP * n2
        nblk = L // ROWS
        grid = (B, nblk)
        nat = lambda wd: pl.BlockSpec((None, ROWS, wd), lambda b, j: (b, j, 0))
        nat_shape = lambda wd, dt: jax.ShapeDtypeStruct((B, L, wd), dt)
        x8 = x.reshape(B, L // 8, 8, D_MODEL)
        per8 = ROWS // 8
        in_specs = [nat(D_MODEL),
                    pl.BlockSpec((None, None, 8, D_MODEL), lambda b, j: (b, jnp.maximum(j * per8 - 1, 0), 0, 0)),
                    pl.BlockSpec((None, None, 8, D_MODEL),
                                 lambda b, j: (b, jnp.minimum((j + 1) * per8, L // 8 - 1), 0, 0)),
                    pl.BlockSpec((None, 1, 3 * D_MODEL), lambda b, j: (b, 0, 0)),
                    pl.BlockSpec((1, D_MODEL), const2),
                    wspec, wcspec,
                    pl.BlockSpec((ROWS, LANE), lambda b, j: (j, 0)),
                    pl.BlockSpec((ROWS, LANE), lambda b, j: (j, 0))]
        args = [x, x8, x8, mod, g, w, wconv, rope_tabs[0], rope_tabs[1]]
        out_specs = [nat(2 * ATTN_W), nat(2 * LANE), nat(KV_W), nat(ATTN_W),
                     pl.BlockSpec((None, n2, N1_STEP, cvw), lambda b, j: (b, 0, j, 0))]
        out_shape = [nat_shape(2 * ATTN_W, BF16), nat_shape(2 * LANE, BF16), nat_shape(KV_W, BF16),
                     nat_shape(ATTN_W, BF16), jax.ShapeDtypeStruct((B, n2, n1h, cvw), BF16)]
        scratch = [pltpu.VMEM((cvw // LANE, N1_STEP * ROW_PITCH, LANE), F32)]
        sem = ("arbitrary", "arbitrary")
    else:
        bb = ROWS // L
        grid = (B // bb,)
        nat = lambda wd: pl.BlockSpec((bb, L, wd), lambda i: (i, 0, 0))
        nat_shape = lambda wd, dt: jax.ShapeDtypeStruct((B, L, wd), dt)
        in_specs = [nat(D_MODEL),
                    pl.BlockSpec((None, 1, 3 * D_MODEL), lambda i: (0, 0, 0)),
                    pl.BlockSpec((1, D_MODEL), const2),
                    wspec, wcspec,
                    pl.BlockSpec(memory_space=pl.ANY),
                    pl.BlockSpec(memory_space=pl.ANY)]
        args = [x, mod, g, w, wconv, caches[0], caches[1]]
        aliases = {5: 5, 6: 6}
        cache_spec = pl.BlockSpec((bb, None, L, KV_W), lambda i: (i, layer, 0, 0))
        cache_shape = jax.ShapeDtypeStruct((B, DEPTH, L, KV_W), F32)
        out_specs = [nat(2 * ATTN_W), nat(2 * LANE), nat(KV_W), nat(ATTN_W), nat(cvw), cache_spec, cache_spec]
        out_shape = [nat_shape(2 * ATTN_W, BF16), nat_shape(2 * LANE, BF16), nat_shape(KV_W, BF16),
                     nat_shape(ATTN_W, BF16), nat_shape(cvw, BF16), cache_shape, cache_shape]
        scratch = []
        sem = ("arbitrary",)
    outs = pl.pallas_call(
        functools.partial(_in_proj_kernel, latent=grp.two_stage, seq=L),
        grid=grid, in_specs=in_specs, out_specs=out_specs, out_shape=out_shape,
        scratch_shapes=scratch, input_output_aliases=aliases, compiler_params=_params(*sem),
        name="in_proj_lat" if grp.two_stage else "in_proj_ctx",
    )(*args)
    outs = list(outs)
    if not grp.two_stage:
        outs[4] = outs[4].reshape(B, 1, L, cvw)
    return outs


def _window_bias():
    qi = np.arange(BLOCK)[:, None]
    kj = np.arange(3 * BLOCK)[None, :]
    return np.where(np.abs(kj - BLOCK - qi) <= WINDOW, 0.0, NEG_INF).astype(np.float32)


def _attn_kernel(*refs, lq, local, nblocks, qblocks):
    if local:
        sink_ref, bias_ref, qq_ref, kk_ref, vv_ref, ck_ref, cv_ref, ga_ref, o_ref = refs
    else:
        sink_ref, qq_ref, kk_ref, vv_ref, ga_ref, o_ref = refs
    group = N_HEADS // N_KV_HEADS
    rows = group * lq
    nt = (((1,), (1,)), ((), ()))
    lt64 = _lane_lt64((lq, LANE))

    for qb in range(qblocks):
        qrows = slice(qb * lq, (qb + 1) * lq)
        if local:
            n = pl.program_id(1) * qblocks + qb
            starts = [pl.multiple_of(jnp.maximum(n - 1, 0) * BLOCK, BLOCK),
                      pl.multiple_of(n * BLOCK, BLOCK),
                      pl.multiple_of(jnp.minimum(n + 1, nblocks - 1) * BLOCK, BLOCK)]
            kj = lax.broadcasted_iota(jnp.int32, (1, 3 * BLOCK), 1)
            edge = (jnp.where((kj < BLOCK) & (n == 0), NEG_INF, 0.0)
                    + jnp.where((kj >= 2 * BLOCK) & (n == nblocks - 1), NEG_INF, 0.0))
            bias = bias_ref[...] + edge
        for g in range(N_KV_HEADS):
            lhs = jnp.concatenate(
                [qq_ref[qrows, (group * g + i) * LANE:(group * g + i + 1) * LANE] for i in range(group)], axis=0)
            gl = slice(g * LANE, (g + 1) * LANE)
            if local:
                kwin = jnp.concatenate([kk_ref[pl.ds(s, BLOCK), gl] for s in starts], axis=0)
                vals = jnp.concatenate([vv_ref[pl.ds(s, BLOCK), :] for s in starts] + [cv_ref[...]], axis=0)
                s_loc = lax.dot_general(lhs, kwin, nt, preferred_element_type=F32)
                s_ctx = lax.dot_general(lhs, ck_ref[:, gl], nt, preferred_element_type=F32)
            else:
                vals = vv_ref[qrows, :]
                s_loc = lax.dot_general(lhs, kk_ref[qrows, gl], nt, preferred_element_type=F32)
            row_head = lax.broadcasted_iota(jnp.int32, (rows, 1), 0) // lq
            snk = jnp.zeros((rows, 1), F32)
            for i in range(group):
                snk = jnp.where(row_head == i, sink_ref[group * g + i], snk)
            if local:
                s_loc = s_loc + jnp.concatenate([bias] * group, axis=0)
            m = jnp.maximum(jnp.max(s_loc, axis=-1, keepdims=True), snk)
            if local:
                m = jnp.maximum(m, jnp.max(s_ctx, axis=-1, keepdims=True))
            ones = jnp.ones((vals.shape[0], LANE), BF16)
            vext = jnp.concatenate([vals, ones], axis=1)
            nk = s_loc.shape[1]
            o = _mm(jnp.exp(s_loc - m).astype(BF16), vext[:nk])
            if local:
                o = o + _mm(jnp.exp(s_ctx - m).astype(BF16), vext[nk:])
            o = o[:, :LANE] / (o[:, LANE:] + jnp.exp(snk - m))
            for jj in range(group // 2):
                a = o[(2 * jj) * lq:(2 * jj + 1) * lq]
                b = o[(2 * jj + 1) * lq:(2 * jj + 2) * lq]
                if g == 0:
                    tile = jnp.where(lt64, a, pltpu.roll(b, HEAD_DIM, axis=1))
                else:
                    tile = jnp.where(lt64, pltpu.roll(a, HEAD_DIM, axis=1), b)
                j = (group // 2) * g + jj
                cols = slice(j * LANE, (j + 1) * LANE)
                o_ref[qrows, cols] = (tile * ga_ref[qrows, cols].astype(F32)).astype(BF16)


def _attention(grp, sink, qq, kk, vv, ga, layer=0, ctx=None):
    B, L = grp.B, grp.L
    smem = pl.BlockSpec(memory_space=pltpu.SMEM)
    if ctx is not None:
        ck, cv = ctx
        nb = L // BLOCK
        lc = ck.shape[2]
        qrows = ATTN_QBLOCKS * BLOCK
        grid = (B, nb // ATTN_QBLOCKS)
        blk = lambda wd: pl.BlockSpec((None, qrows, wd), lambda b, n: (b, n, 0))
        full = lambda rows, wd: pl.BlockSpec((None, rows, wd), lambda b, n: (b, 0, 0))
        cache = lambda wd: pl.BlockSpec((None, None, lc, wd), lambda b, n: (b, layer, 0, 0))
        in_specs = [smem, pl.BlockSpec((BLOCK, 3 * BLOCK), lambda b, n: (0, 0)),
                    blk(2 * ATTN_W), full(L, 2 * LANE), full(L, KV_W), cache(2 * LANE), cache(KV_W),
                    blk(ATTN_W)]
        args = [sink, jnp.asarray(_window_bias()), qq, kk, vv, ck, cv, ga]
        out_spec = blk(ATTN_W)
        out_shape = jax.ShapeDtypeStruct((B, L, ATTN_W), BF16)
        kern = functools.partial(_attn_kernel, lq=BLOCK, local=True, nblocks=nb, qblocks=ATTN_QBLOCKS)
        sem = ("arbitrary", "arbitrary")
    else:
        grid = (B // CTX_SEQS,)
        flat = lambda a: a.reshape(B * L, a.shape[-1])
        blk = lambda wd: pl.BlockSpec((CTX_SEQS * L, wd), lambda i: (i, 0))
        in_specs = [smem, blk(2 * ATTN_W), blk(2 * LANE), blk(KV_W), blk(ATTN_W)]
        args = [sink, flat(qq), flat(kk), flat(vv), flat(ga)]
        out_spec = blk(ATTN_W)
        out_shape = jax.ShapeDtypeStruct((B * L, ATTN_W), BF16)
        kern = functools.partial(_attn_kernel, lq=L, local=False, nblocks=1, qblocks=CTX_SEQS)
        sem = ("arbitrary",)
    out = pl.pallas_call(
        kern, grid=grid, in_specs=in_specs, out_specs=out_spec, out_shape=out_shape,
        compiler_params=_params(*sem), name="attn_lat" if ctx is not None else "attn_ctx",
    )(*args)
    return out.reshape(B, L, ATTN_W)


def _filter_kernel(t_ref, f_ref, w1_ref, b1_ref, w2_ref, b2_ref, w3_ref, fq_ref, dl_ref,
                   kf_ref, kb_ref, *, chunk):
    rows = t_ref.shape[0]
    width = 2 * HYENA_W
    fq = fq_ref[...]
    nn = (((0,), (0,)), ((), ()))

    def body(i, acc):
        r0 = pl.multiple_of(i * chunk, chunk)
        h = jnp.sin(fq * (_mm_hi(w1_ref[...], f_ref[:, pl.ds(r0, chunk)]) + b1_ref[...]))
        h = jnp.sin(fq * (_mm_hi(w2_ref[...], h) + b2_ref[...]))
        dec = jnp.exp(-(t_ref[pl.ds(r0, chunk), :] * dl_ref[...]))
        dec = jnp.concatenate([dec, dec], axis=1)
        hb = h.astype(BF16)
        kf = lax.dot_general(hb, w3_ref[:, 0:width].astype(BF16), nn, preferred_element_type=F32) * dec
        kb = lax.dot_general(hb, w3_ref[:, width:2 * width].astype(BF16), nn, preferred_element_type=F32) * dec
        rid = lax.broadcasted_iota(jnp.int32, kb.shape, 0) + r0
        kb = jnp.where(rid == 0, 0.0, kb)
        kf_ref[pl.ds(r0, chunk), :] = kf
        kb_ref[pl.ds(r0, chunk), :] = kb
        return acc + jnp.sum(jnp.abs(kf) + jnp.abs(kb), axis=0, keepdims=True)

    total = lax.fori_loop(0, rows // chunk, body, jnp.zeros((1, width), F32))

    def scale(i, c):
        r0 = pl.multiple_of(i * chunk, chunk)
        kf_ref[pl.ds(r0, chunk), :] = kf_ref[pl.ds(r0, chunk), :] / total
        kb_ref[pl.ds(r0, chunk), :] = kb_ref[pl.ds(r0, chunk), :] / total
        return c

    lax.fori_loop(0, rows // chunk, scale, 0)


def _filters(grp, feats, w1, b1, w2, b2, w3, fq, deltas):
    L = grp.L
    chunk = min(512, L)
    c2 = lambda l: (0, 0)
    per = lambda a, b: pl.BlockSpec((None, a, b), lambda l: (l, 0, 0))
    shape = jax.ShapeDtypeStruct((DEPTH, L, 2 * HYENA_W), F32)
    return pl.pallas_call(
        functools.partial(_filter_kernel, chunk=chunk), grid=(DEPTH,),
        in_specs=[pl.BlockSpec((L, 1), c2), pl.BlockSpec((LANE, L), c2),
                  per(FILTER_HIDDEN, LANE), per(FILTER_HIDDEN, 1), per(FILTER_HIDDEN, FILTER_HIDDEN),
                  per(FILTER_HIDDEN, 1), per(FILTER_HIDDEN, 4 * HYENA_W), per(FILTER_HIDDEN, 1),
                  pl.BlockSpec((1, HYENA_W), c2)],
        out_specs=[per(L, 2 * HYENA_W)] * 2, out_shape=[shape] * 2,
        compiler_params=_params("arbitrary"),
        name="filters_lat" if grp.two_stage else "filters_ctx",
    )(*feats, w1, b1, w2, b2, w3, fq, deltas)


def _store_spectrum_rows(s_ref, base, a, n1):
    for comp in range(2):
        for sl in range(2):
            s_ref[comp, sl, pl.ds(base, n1), :] = a[comp * n1:(comp + 1) * n1, sl * LANE:(sl + 1) * LANE]


def _load_column(s_ref, k1, n2, pitch):
    parts = [jnp.concatenate([s_ref[comp, sl, pl.ds(k1, n2, stride=pitch), :] for sl in range(2)], axis=1)
             for comp in range(2)]
    return jnp.concatenate(parts, axis=0).astype(BF16)


def _spectrum2_kernel(kf_ref, kb_ref, faf_ref, g_ref, o_ref, s_ref, *, n1, n2):
    n1h = n1 // 2
    pitch = n1 + PITCH_PAD
    faf = faf_ref[...]

    def stage_a(i, c):
        for t in range(UNROLL):
            j = i * UNROLL + t
            r0 = pl.multiple_of(j * n1h, n1h)
            rb = pl.multiple_of(((n2 - j) % n2) * n1h, n1h)
            back = kb_ref[pl.ds(rb, n1h), :]
            back = jnp.where(j == 0, pltpu.roll(back, n1h - 1, axis=0), back)
            rhs = jnp.concatenate([kf_ref[pl.ds(r0, n1h), :], back], axis=0).astype(BF16)
            _store_spectrum_rows(s_ref, pl.multiple_of(j * pitch, PITCH_PAD), _mm(faf, rhs), n1)
        return c

    lax.fori_loop(0, n2 // UNROLL, stage_a, 0)

    def stage_c(i, c):
        for t in range(UNROLL_WIDE):
            k1 = i * UNROLL_WIDE + t
            o_ref[k1] = _mm(g_ref[k1], _load_column(s_ref, k1, n2, pitch)).astype(BF16)
        return c

    lax.fori_loop(0, n1 // UNROLL_WIDE, stage_c, 0)


def _spectrum1_kernel(kf_ref, kb_ref, faf_ref, o_ref):
    back = kb_ref[...]
    rhs = jnp.concatenate([kf_ref[...], pltpu.roll(back, back.shape[0] - 1, axis=0)], axis=0).astype(BF16)
    o_ref[...] = _mm(faf_ref[...], rhs).astype(BF16)


def _spectrum(grp, kf, kb, faf, g):
    n1, n2, L = grp.N1, grp.N2, grp.L
    wd = 2 * LANE
    nblk = 2 * HYENA_W // wd
    kin = pl.BlockSpec((None, L, wd), lambda l, c: (l, 0, c))
    if grp.two_stage:
        pitch = n1 + PITCH_PAD
        return pl.pallas_call(
            functools.partial(_spectrum2_kernel, n1=n1, n2=n2), grid=(DEPTH, nblk),
            in_specs=[kin, kin, pl.BlockSpec((2 * n1, n1), lambda l, c: (0, 0)),
                      pl.BlockSpec((n1, 2 * n2, 2 * n2), lambda l, c: (0, 0, 0))],
            out_specs=pl.BlockSpec((None, n1, 2 * n2, wd), lambda l, c: (l, 0, 0, c)),
            out_shape=jax.ShapeDtypeStruct((DEPTH, n1, 2 * n2, 2 * HYENA_W), BF16),
            scratch_shapes=[pltpu.VMEM((2, 2, n2 * pitch, LANE), F32)],
            compiler_params=_params("arbitrary", "arbitrary"), name="spectrum_lat",
        )(kf, kb, faf, g)
    return pl.pallas_call(
        _spectrum1_kernel, grid=(DEPTH, nblk),
        in_specs=[kin, kin, pl.BlockSpec((2 * n1, n1), lambda l, c: (0, 0))],
        out_specs=pl.BlockSpec((None, 2 * n1, wd), lambda l, c: (l, 0, c)),
        out_shape=jax.ShapeDtypeStruct((DEPTH, 2 * n1, 2 * HYENA_W), BF16),
        compiler_params=_params("arbitrary", "arbitrary"), name="spectrum_ctx",
    )(kf, kb, faf)


def _pair_operand(u_ref, j):
    nb = u_ref.shape[0]
    re = jnp.concatenate([u_ref[b, j] for b in range(0, nb, 2)], axis=1)
    im = jnp.concatenate([u_ref[b, j] for b in range(1, nb, 2)], axis=1)
    return jnp.concatenate([re, im], axis=0)


def _cmul(x, k, half):
    k = k.astype(F32)
    npair = x.shape[1] // LANE
    kr = jnp.concatenate([k[:half]] * npair, axis=1)
    ki = jnp.concatenate([k[half:]] * npair, axis=1)
    xr, xi = x[:half], x[half:]
    return jnp.concatenate([xr * kr - xi * ki, xr * ki + xi * kr], axis=0).astype(BF16)


def _conv_epilogue(u_ref, m_ref, o_ref, d, y, j, n1h):
    for b in range(u_ref.shape[0]):
        yb = y[(b % 2) * n1h:(b % 2 + 1) * n1h, (b // 2) * LANE:(b // 2 + 1) * LANE]
        u = u_ref[b, j].astype(F32)
        o_ref[b, j] = (m_ref[b, j].astype(F32) * (yb + u * d)).astype(BF16)


def _conv2_kernel(u_ref, m_ref, fa_ref, fat_ref, g_ref, k_ref, d_ref, o_ref, s_ref, *, n1, n2):
    n1h = n1 // 2
    pitch = n1 + PITCH_PAD
    fa = fa_ref[...]
    fat = fat_ref[...]
    d = d_ref[...]

    def stage_a(i, c):
        for t in range(UNROLL_WIDE):
            j = i * UNROLL_WIDE + t
            _store_spectrum_rows(s_ref, pl.multiple_of(j * pitch, PITCH_PAD), _mm(fa, _pair_operand(u_ref, j)), n1)
        return c

    lax.fori_loop(0, n2 // UNROLL_WIDE, stage_a, 0)

    def stage_c(i, c):
        ks = [i * UNROLL_WIDE + t for t in range(UNROLL_WIDE)]
        cols = [_load_column(s_ref, k1, n2, pitch) for k1 in ks]
        backs = []
        for k1, col in zip(ks, cols):
            g = g_ref[k1]
            y = _cmul(_mm(g, col), k_ref[k1], n2)
            backs.append(lax.dot_general(g, y, (((0,), (0,)), ((), ())), preferred_element_type=F32))
        for k1, back in zip(ks, backs):
            for comp in range(2):
                for sl in range(2):
                    s_ref[comp, sl, pl.ds(k1, n2, stride=pitch), :] = (
                        back[comp * n2:(comp + 1) * n2, sl * LANE:(sl + 1) * LANE])
        return c

    lax.fori_loop(0, n1 // UNROLL_WIDE, stage_c, 0)

    def stage_inv(i, c):
        for t in range(UNROLL):
            j = i * UNROLL + t
            base = pl.multiple_of(j * pitch, PITCH_PAD)
            parts = [jnp.concatenate([s_ref[comp, sl, pl.ds(base, n1), :] for sl in range(2)], axis=1)
                     for comp in range(2)]
            y = _mm(fat, jnp.concatenate(parts, axis=0).astype(BF16))
            _conv_epilogue(u_ref, m_ref, o_ref, d, y, j, n1h)
        return c

    lax.fori_loop(0, n2 // UNROLL, stage_inv, 0)


def _conv1_kernel(u_ref, m_ref, fa_ref, fat_ref, k_ref, d_ref, o_ref, *, n1):
    x = _mm(fa_ref[...], _pair_operand(u_ref, 0))
    y = _mm(fat_ref[...], _cmul(x, k_ref[...], n1))
    _conv_epilogue(u_ref, m_ref, o_ref, d_ref[...], y, 0, n1 // 2)


def _long_conv(grp, u, m, spec, d, layer, order, tabs):
    B, n1, n2, n1h = grp.B, grp.N1, grp.N2, grp.N1h
    ncb = HYENA_W // LANE
    nbatch = 4 if grp.two_stage else CTX_CONV_SEQS
    grid = (ncb, B // nbatch)
    group = lambda k: pl.BlockSpec((nbatch, n2, n1h, LANE), lambda c, q: (q, 0, 0, k * ncb + c))
    (u, ku), (m, km) = u, m
    uspec, mspec = group(ku), group(km)
    data = group(0)
    dspec = pl.BlockSpec((None, 1, LANE), lambda c, q: (layer, 0, order * ncb + c))
    fa = pl.BlockSpec((2 * n1, n1), lambda c, q: (0, 0))
    fat = pl.BlockSpec((n1, 2 * n1), lambda c, q: (0, 0))
    out_shape = jax.ShapeDtypeStruct((B, n2, n1h, HYENA_W), BF16)
    if grp.two_stage:
        pitch = n1 + PITCH_PAD
        return pl.pallas_call(
            functools.partial(_conv2_kernel, n1=n1, n2=n2), grid=grid,
            in_specs=[uspec, mspec, fa, fat,
                      pl.BlockSpec((n1, 2 * n2, 2 * n2), lambda c, q: (0, 0, 0)),
                      pl.BlockSpec((None, n1, 2 * n2, LANE), lambda c, q: (layer, 0, 0, order * ncb + c)),
                      dspec],
            out_specs=data, out_shape=out_shape,
            scratch_shapes=[pltpu.VMEM((2, 2, n2 * pitch, LANE), F32)],
            compiler_params=_params("arbitrary", "arbitrary"), name="long_conv_lat",
        )(u, m, tabs["fa"], tabs["fat"], tabs["g"], spec, d)
    return pl.pallas_call(
        functools.partial(_conv1_kernel, n1=n1), grid=grid,
        in_specs=[uspec, mspec, fa, fat,
                  pl.BlockSpec((None, 2 * n1, LANE), lambda c, q: (layer, 0, order * ncb + c)), dspec],
        out_specs=data, out_shape=out_shape,
        compiler_params=_params("arbitrary", "arbitrary"), name="long_conv_ctx",
    )(u, m, tabs["fa"], tabs["fat"], spec, d)


def _out_proj_kernel(*refs, permute, final):
    if permute:
        x_ref, mod_ref, a_ref, z_ref, s_ref, w_ref, fg_ref, o_ref, tmp_ref = refs
    else:
        x_ref, mod_ref, a_ref, z_ref, s_ref, w_ref, fg_ref, o_ref = refs
    a = a_ref[...].reshape(ROWS, ATTN_W)
    nsl = HYENA_W // LANE
    if permute:
        n2 = z_ref.shape[0]
        for j in range(n2):
            for sl in range(nsl):
                cols = slice(sl * LANE, (sl + 1) * LANE)
                tmp_ref[sl, pl.ds(j, N1_STEP, stride=ROW_PITCH), :] = z_ref[j, :, cols].astype(F32)
                tmp_ref[nsl + sl, pl.ds(j, N1_STEP, stride=ROW_PITCH), :] = s_ref[j, :, cols].astype(F32)
        conv = jnp.concatenate(
            [jnp.concatenate([tmp_ref[i, k * ROW_PITCH:k * ROW_PITCH + n2, :] for k in range(N1_STEP)], axis=0)
             for i in range(2 * nsl)], axis=1).astype(BF16)
    else:
        conv = jnp.concatenate([z_ref[...].reshape(ROWS, HYENA_W), s_ref[...].reshape(ROWS, CONV_W)], axis=1)
    y = _mm(a, w_ref[0:ATTN_W, :]) + _mm(conv, w_ref[ATTN_W:, :])
    gate = mod_ref[:, 2 * D_MODEL:3 * D_MODEL]
    xn = x_ref[...].reshape(ROWS, D_MODEL) + gate * y
    if final:
        ms = jnp.mean(xn * xn, axis=-1, keepdims=True)
        xn = xn * lax.rsqrt(ms + RMS_EPS) * fg_ref[...]
    o_ref[...] = xn.reshape(o_ref.shape)


def _out_proj(grp, x, mod, attn, zg, scg, w, layer, final_g, final):
    B, L = grp.B, grp.L
    const2 = lambda *_: (0, 0)
    wspec = pl.BlockSpec((None, D_MODEL, D_MODEL), lambda *_: (layer, 0, 0))
    (zg, kz), (scg, ks) = zg, scg
    if grp.two_stage:
        n2 = grp.N2
        grid = (B, L // ROWS)
        nat = lambda wd: pl.BlockSpec((None, ROWS, wd), lambda b, j: (b, j, 0))
        cvl = lambda k: pl.BlockSpec((None, n2, N1_STEP, HYENA_W), lambda b, j: (b, 0, j, k))
        in_specs = [nat(D_MODEL), pl.BlockSpec((None, 1, 3 * D_MODEL), lambda b, j: (b, 0, 0)),
                    nat(ATTN_W), cvl(kz), cvl(ks), wspec, pl.BlockSpec((1, D_MODEL), const2)]
        scratch = [pltpu.VMEM((2 * HYENA_W // LANE, N1_STEP * ROW_PITCH, LANE), F32)]
        sem = ("arbitrary", "arbitrary")
    else:
        bb = ROWS // L
        grid = (B // bb,)
        nat = lambda wd: pl.BlockSpec((bb, L, wd), lambda i: (i, 0, 0))
        cvl = lambda k: pl.BlockSpec((bb, None, L, HYENA_W), lambda i: (i, 0, 0, k))
        in_specs = [nat(D_MODEL), pl.BlockSpec((None, 1, 3 * D_MODEL), lambda i: (0, 0, 0)),
                    nat(ATTN_W), cvl(kz), cvl(ks), wspec, pl.BlockSpec((1, D_MODEL), const2)]
        scratch = []
        sem = ("arbitrary",)
    return pl.pallas_call(
        functools.partial(_out_proj_kernel, permute=grp.two_stage, final=final),
        grid=grid, in_specs=in_specs, out_specs=nat(D_MODEL),
        out_shape=jax.ShapeDtypeStruct((B, L, D_MODEL), F32),
        scratch_shapes=scratch, compiler_params=_params(*sem),
        name="out_proj_lat" if grp.two_stage else "out_proj_ctx",
    )(x, mod, attn, zg, scg, w, final_g)


def _group_tables(grp):
    fa, faf, g = _dft_tables(grp)
    tabs = {"fa": jnp.asarray(fa).astype(BF16), "fat": jnp.asarray(fa.T.copy()).astype(BF16),
            "faf": jnp.asarray(faf).astype(BF16)}
    if g is not None:
        tabs["g"] = jnp.asarray(g).astype(BF16)
    tabs["feats"] = tuple(jnp.asarray(a) for a in _filter_features(grp))
    return tabs


def kernel(x_prompt, x_sample, cache_k, cache_v, c, c_ctx, norm_g, mod_w, mod_b, w_in, attn_sink,
           hy_conv_w, hy_filt_w1, hy_filt_b1, hy_filt_w2, hy_filt_b2, hy_filt_w3, hy_filt_freq, hy_d,
           sc_conv_w, w_out, final_g):
    ctx = Group(x_prompt.shape[0], x_prompt.shape[1], 2 * x_prompt.shape[1], 1)
    lat = Group(x_sample.shape[0], x_sample.shape[1], 128, 2 * x_sample.shape[1] // 128)
    nlat = lat.B

    cond = jnp.zeros((16, D_MODEL), F32).at[0].set(c_ctx).at[1:1 + nlat].set(c)
    mods = _modulation(cond, mod_w, mod_b)

    w1t = jnp.pad(hy_filt_w1, ((0, 0), (0, LANE - FILTER_EMB), (0, 0))).transpose(0, 2, 1)
    w2t = hy_filt_w2.transpose(0, 2, 1)
    b1 = hy_filt_b1[:, :, None]
    b2 = hy_filt_b2[:, :, None]
    fq = hy_filt_freq[:, :, None]
    deltas = jnp.asarray(_decay_rates())
    tabs, spec = {}, {}
    for grp in (ctx, lat):
        t = _group_tables(grp)
        kf, kb = _filters(grp, t["feats"], w1t, b1, w2t, b2, hy_filt_w3, fq, deltas)
        spec[grp] = _spectrum(grp, kf, kb, t["faf"], t.get("g"))
        tabs[grp] = t
    rope = tuple(jnp.asarray(a) for a in _rope_tables(lat.L))

    w_in_b = w_in.astype(BF16)
    w_out_b = w_out.astype(BF16)
    wconv = jnp.concatenate([hy_conv_w.reshape(DEPTH, 3, 3, HYENA_W), sc_conv_w[:, :, None, :]], axis=2)
    dskip = hy_d.reshape(DEPTH, 1, 2 * HYENA_W)
    fg = final_g[None, :]
    lc = cache_k.shape[2]
    zpad = jnp.zeros((nlat, DEPTH, lc, HEAD_DIM), F32)
    ck_pad = jnp.concatenate([zpad, cache_k[:, :, :, 0], zpad, cache_k[:, :, :, 1]], axis=-1).astype(BF16)
    cv_nat = cache_v.reshape(nlat, DEPTH, lc, KV_W).astype(BF16)

    xp, xs = x_prompt, x_sample
    caches = (jnp.zeros((ctx.B, DEPTH, ctx.L, KV_W), F32), jnp.zeros((ctx.B, DEPTH, ctx.L, KV_W), F32))
    for l in range(DEPTH):
        g = norm_g[l][None, :]
        last = l == DEPTH - 1
        for grp in (ctx, lat):
            is_lat = grp is lat
            x = xs if is_lat else xp
            mod = mods[l, 1:1 + nlat][:, None, :] if is_lat else mods[l, 0:1][:, None, :]
            if is_lat:
                qq, kk, vv, ga, cv = _in_proj(grp, x, mod, g, w_in_b, wconv[l], l, rope_tabs=rope)
                attn = _attention(grp, attn_sink[l], qq, kk, vv, ga, l, ctx=(ck_pad, cv_nat))
            else:
                qq, kk, vv, ga, cv, *caches = _in_proj(grp, x, mod, g, w_in_b, wconv[l], l, caches=caches)
                attn = _attention(grp, attn_sink[l], qq, kk, vv, ga)
            z1 = _long_conv(grp, (cv, 0), (cv, 1), spec[grp], dskip, l, 0, tabs[grp])
            zg = _long_conv(grp, (z1, 0), (cv, 2), spec[grp], dskip, l, 1, tabs[grp])
            xn = _out_proj(grp, x, mod, attn, (zg, 0), (cv, 3), w_out_b, l, fg, last)
            if is_lat:
                xs = xn
            else:
                xp = xn
    shape = (ctx.B, DEPTH, ctx.L, N_KV_HEADS, HEAD_DIM)
    return (xp, xs, caches[0].reshape(shape), caches[1].reshape(shape))
```
